```python
import math
import jax, jax.numpy as jnp
from jax import lax
import numpy as np

D_MODEL = 1024
BATCH = 32
SEQ = 256
DEPTH = 4
DEC_BATCH = 2
DEC_SEQ = 2048
PAST_LEN = 512

GRID_W = 64
HEAD_DIM = 64
N_HEADS_A = 8
N_KV_A = 2
N_HEADS_B = 8
N_KV_B = 2
A_WIDTH = N_HEADS_A * HEAD_DIM
B_WIDTH = N_HEADS_B * HEAD_DIM
KV_WIDTH_A = N_KV_A * HEAD_DIM
KV_WIDTH_B = N_KV_B * HEAD_DIM
C_WIDTH = 512
C_GROUPS = 4
C_GROUP_DIM = C_WIDTH // C_GROUPS
CHUNK = 128
BLOCK = 128
WINDOW = 128
N_BRANCH = 3
IN_WIDTH = 2 * A_WIDTH + 2 * KV_WIDTH_A + 2 * B_WIDTH + 2 * KV_WIDTH_B + 3 * C_WIDTH
ROPE_BASE = 10000.0
ROPE_FREQS = HEAD_DIM // 4
EPS = 1e-6
NEG_INF = -1e30
DEEPNORM_ALPHA = (2 * DEPTH) ** 0.25
DEEPNORM_BETA = (8 * DEPTH) ** -0.25

kernel_name = 'hybrid_flow_prefix_trunk_step'


def layer_norm(x, g=None, b=None):
    xf = x.astype(jnp.float32)
    mu = jnp.mean(xf, axis=-1, keepdims=True)
    var = jnp.mean(jnp.square(xf - mu), axis=-1, keepdims=True)
    y = (xf - mu) * lax.rsqrt(var + EPS)
    if g is not None:
        y = y * g.astype(jnp.float32) + b.astype(jnp.float32)
    return y.astype(x.dtype)


def rms_norm(x, g):
    xf = x.astype(jnp.float32)
    y = xf * lax.rsqrt(jnp.mean(jnp.square(xf), axis=-1, keepdims=True) + EPS) * g.astype(jnp.float32)
    return y.astype(x.dtype)


def axial_rope_tables(n_tokens):
    rows = n_tokens // GRID_W
    row = jnp.repeat(jnp.arange(rows, dtype=jnp.float32), GRID_W)
    col = jnp.tile(jnp.arange(GRID_W, dtype=jnp.float32), rows)
    inv_freq = jnp.power(ROPE_BASE, -jnp.arange(ROPE_FREQS, dtype=jnp.float32) / ROPE_FREQS)
    ang = jnp.concatenate([row[:, None] * inv_freq, col[:, None] * inv_freq], axis=-1)
    return jnp.cos(ang), jnp.sin(ang)


def apply_rope(x, cos, sin):
    half = HEAD_DIM // 2
    xf = x.astype(jnp.float32)
    x1, x2 = xf[..., :half], xf[..., half:]
    c = cos[None, :, None, :]
    s = sin[None, :, None, :]
    return jnp.concatenate([x1 * c - x2 * s, x2 * c + x1 * s], axis=-1).astype(x.dtype)


def ada_modulation(cvec, w_mod, b_mod):
    m = (jax.nn.silu(cvec) @ w_mod + b_mod)[..., None, :]
    shift, scale, gate = jnp.split(m, 3, axis=-1)
    return shift, scale, gate


def split_projection(p):
    B, T, _ = p.shape
    sizes = [A_WIDTH, KV_WIDTH_A, KV_WIDTH_A, A_WIDTH,
             B_WIDTH, KV_WIDTH_B, KV_WIDTH_B, B_WIDTH,
             C_WIDTH, C_WIDTH, C_WIDTH]
    qa, ka, va, ga, qb, kb, vb, gb, uc, vc, gc = jnp.split(p, np.cumsum(sizes)[:-1].tolist(), axis=-1)
    heads = lambda t, n: t.reshape(B, T, n, HEAD_DIM)
    return dict(qa=heads(qa, N_HEADS_A), ka=heads(ka, N_KV_A), va=heads(va, N_KV_A), ga=ga,
                qb=heads(qb, N_HEADS_B), kb=heads(kb, N_KV_B), vb=heads(vb, N_KV_B), gb=gb,
                uc=uc, vc=vc, gc=gc)


def dense_gqa(q, k, v, n_kv, sink=None):
    B, T, H, d = q.shape
    G = H // n_kv
    nb = T // BLOCK
    qb = (q * d ** -0.5).reshape(B, nb, BLOCK, n_kv, G, d).transpose(1, 0, 2, 3, 4, 5)

    def one_block(qblk):
        s = jnp.einsum('bqkgd,blkd->bkgql', qblk, k).astype(jnp.float32)
        if sink is not None:
            col = jnp.broadcast_to(sink.astype(jnp.float32).reshape(n_kv, G)[:, :, None, None], s.shape[:-1] + (1,))
            s = jnp.concatenate([s, col], axis=-1)
        p = jax.nn.softmax(s, axis=-1)
        if sink is not None:
            p = p[..., :-1]
        return jnp.einsum('bkgql,blkd->bqkgd', p.astype(v.dtype), v)

    o = lax.map(one_block, qb)
    return o.transpose(1, 0, 2, 3, 4, 5).reshape(B, T, H, d)


def window_gqa(q, k, v, k_ctx, v_ctx, sink):
    B, N, H, d = q.shape
    G = H // N_KV_B
    nb = N // BLOCK
    P = k_ctx.shape[1]
    qb = (q * d ** -0.5).reshape(B, nb, BLOCK, N_KV_B, G, d)
    pad = ((0, 0), (WINDOW, WINDOW), (0, 0), (0, 0))

    def bands(t):
        tp = jnp.pad(t, pad).reshape(B, nb + 2, BLOCK, N_KV_B, d)
        return jnp.concatenate([tp[:, :-2], tp[:, 1:-1], tp[:, 2:]], axis=2)

    kband, vband = bands(k), bands(v)
    nband = 3 * BLOCK
    s_band = jnp.einsum('bnqkgd,bnlkd->bnkgql', qb, kband).astype(jnp.float32)
    blk = jnp.arange(nb)[:, None, None] * BLOCK
    q_pos = blk + jnp.arange(BLOCK)[None, :, None]
    k_pos = blk - WINDOW + jnp.arange(nband)[None, None, :]
    valid = (jnp.abs(q_pos - k_pos) <= WINDOW) & (k_pos >= 0) & (k_pos < N)
    s_band = jnp.where(valid[None, :, None, None], s_band, NEG_INF)
    s_ctx = jnp.einsum('bnqkgd,blkd->bnkgql', qb, k_ctx).astype(jnp.float32)
    s_sink = jnp.broadcast_to(sink.astype(jnp.float32).reshape(N_KV_B, G)[:, :, None, None], s_ctx.shape[:-1] + (1,))
    p = jax.nn.softmax(jnp.concatenate([s_band, s_ctx, s_sink], axis=-1), axis=-1)
    p_band = p[..., :nband].astype(v.dtype)
    p_ctx = p[..., nband:nband + P].astype(v.dtype)
    o = (jnp.einsum('bnkgql,bnlkd->bnqkgd', p_band, vband)
         + jnp.einsum('bnkgql,blkd->bnqkgd', p_ctx, v_ctx))
    return o.reshape(B, N, H, d)


def chunk_sgu(u, v, g, b, w_s, b_s):
    B, T, _ = v.shape
    vn = layer_norm(v, g, b).reshape(B, T // CHUNK, CHUNK, C_GROUPS, C_GROUP_DIM)
    mixed = jnp.einsum('gpq,bnqgc->bnpgc', w_s, vn) + b_s.T[:, :, None]
    return u * mixed.reshape(B, T, C_WIDTH)


def merge_and_residual(x, h, gate, br, a_out, b_out, c_out, lw):
    B, T, _ = x.shape
    ya = (a_out.reshape(B, T, A_WIDTH) * jax.nn.silu(br['ga'])) @ lw['w_proj_a']
    yb = (b_out.reshape(B, T, B_WIDTH) * jax.nn.silu(br['gb'])) @ lw['w_proj_b']
    yc = (c_out * jax.nn.silu(br['gc'])) @ lw['w_proj_c']
    g_a, g_b, g_c = jnp.split(jax.nn.sigmoid(h @ lw['w_gate'] + lw['b_gate']), N_BRANCH, axis=-1)
    y = (g_a * ya + g_b * yb + g_c * yc) @ lw['w_out']
    return layer_norm(DEEPNORM_ALPHA * x + gate * y, lw['ln_g'], lw['ln_b'])


def context_layer(x, c_ctx, lw):
    shift, scale, gate = ada_modulation(c_ctx, lw['w_mod'], lw['b_mod'])
    h = layer_norm(x) * (1 + scale) + shift
    br = split_projection(h @ lw['w_in'])
    qa = rms_norm(br['qa'], lw['q_norm'])
    ka = rms_norm(br['ka'], lw['k_norm'])
    a_out = dense_gqa(qa, ka, br['va'], N_KV_A)
    b_out = dense_gqa(br['qb'], br['kb'], br['vb'], N_KV_B, lw['sink'])
    c_out = chunk_sgu(br['uc'], br['vc'], lw['sgu_g'], lw['sgu_b'], lw['w_spatial'], lw['b_spatial'])
    x_new = merge_and_residual(x, h, gate, br, a_out, b_out, c_out, lw)
    return x_new, ka, br['va'], br['kb'], br['vb']


def latent_layer(x, c, lw, ka_ctx, va_ctx, kb_ctx, vb_ctx, cos, sin):
    shift, scale, gate = ada_modulation(c, lw['w_mod'], lw['b_mod'])
    h = layer_norm(x) * (1 + scale) + shift
    br = split_projection(h @ lw['w_in'])
    qa = apply_rope(rms_norm(br['qa'], lw['q_norm']), cos, sin)
    ka = apply_rope(rms_norm(br['ka'], lw['k_norm']), cos, sin)
    a_out = dense_gqa(qa, jnp.concatenate([ka, ka_ctx], axis=1),
                      jnp.concatenate([br['va'], va_ctx], axis=1), N_KV_A)
    qb = apply_rope(br['qb'], cos, sin)
    kb = apply_rope(br['kb'], cos, sin)
    b_out = window_gqa(qb, kb, br['vb'], kb_ctx, vb_ctx, lw['sink'])
    c_out = chunk_sgu(br['uc'], br['vc'], lw['sgu_g'], lw['sgu_b'], lw['w_spatial'], lw['b_spatial'])
    return merge_and_residual(x, h, gate, br, a_out, b_out, c_out, lw)


def setup_inputs(seed: int = 0) -> dict:
    key = jax.random.key(seed)
    ks = jax.random.split(key, 32)
    nrm = lambda k, shape, s: jax.random.normal(k, shape, jnp.float32) * s
    D = D_MODEL
    cache_shape = (DEC_BATCH, DEPTH, PAST_LEN, N_KV_A, HEAD_DIM)
    return {
        'x_prompt': nrm(ks[0], (BATCH, SEQ, D), 1.0),
        'x_sample': nrm(ks[1], (DEC_BATCH, DEC_SEQ, D), 1.0),
        'cache_a_k': nrm(ks[2], cache_shape, 1.0),
        'cache_a_v': nrm(ks[3], cache_shape, 1.0),
        'cache_b_k': nrm(ks[4], (DEC_BATCH, DEPTH, PAST_LEN, N_KV_B, HEAD_DIM), 1.0),
        'cache_b_v': nrm(ks[5], (DEC_BATCH, DEPTH, PAST_LEN, N_KV_B, HEAD_DIM), 1.0),
        'c': nrm(ks[6], (DEC_BATCH, D), 1.0),
        'c_ctx': nrm(ks[7], (D,), 1.0),
        'w_mod': nrm(ks[8], (DEPTH, D, 3 * D), D ** -0.5),
        'b_mod': nrm(ks[9], (DEPTH, 3 * D), 0.02),
        'w_in': nrm(ks[10], (DEPTH, D, IN_WIDTH), D ** -0.5),
        'q_norm': 1.0 + nrm(ks[11], (DEPTH, HEAD_DIM), 0.02),
        'k_norm': 1.0 + nrm(ks[12], (DEPTH, HEAD_DIM), 0.02),
        'sink': nrm(ks[13], (DEPTH, N_HEADS_B), 0.5),
        'sgu_g': 1.0 + nrm(ks[14], (DEPTH, C_WIDTH), 0.02),
        'sgu_b': nrm(ks[15], (DEPTH, C_WIDTH), 0.02),
        'w_spatial': nrm(ks[16], (DEPTH, C_GROUPS, CHUNK, CHUNK), CHUNK ** -0.5),
        'b_spatial': nrm(ks[17], (DEPTH, C_GROUPS, CHUNK), 0.02),
        'w_proj_a': nrm(ks[18], (DEPTH, A_WIDTH, D), A_WIDTH ** -0.5 * DEEPNORM_BETA),
        'w_proj_b': nrm(ks[19], (DEPTH, B_WIDTH, D), B_WIDTH ** -0.5 * DEEPNORM_BETA),
        'w_proj_c': nrm(ks[20], (DEPTH, C_WIDTH, D), C_WIDTH ** -0.5 * DEEPNORM_BETA),
        'w_gate': nrm(ks[21], (DEPTH, D, N_BRANCH * D), D ** -0.5),
        'b_gate': nrm(ks[22], (DEPTH, N_BRANCH * D), 0.02),
        'w_out': nrm(ks[23], (DEPTH, D, D), D ** -0.5 * DEEPNORM_BETA),
        'ln_g': 1.0 + nrm(ks[24], (DEPTH, D), 0.02),
        'ln_b': nrm(ks[25], (DEPTH, D), 0.02),
    }


def reference(x_prompt, x_sample, cache_a_k, cache_a_v, cache_b_k, cache_b_v, c, c_ctx,
              w_mod, b_mod, w_in, q_norm, k_norm, sink, sgu_g, sgu_b, w_spatial, b_spatial,
              w_proj_a, w_proj_b, w_proj_c, w_gate, b_gate, w_out, ln_g, ln_b):
    cos, sin = axial_rope_tables(x_sample.shape[1])
    xp, xs = x_prompt, x_sample
    ka_l, va_l, kb_l, vb_l = [], [], [], []
    for l in range(DEPTH):
        lw = dict(w_mod=w_mod[l], b_mod=b_mod[l], w_in=w_in[l], q_norm=q_norm[l], k_norm=k_norm[l],
                  sink=sink[l], sgu_g=sgu_g[l], sgu_b=sgu_b[l], w_spatial=w_spatial[l],
                  b_spatial=b_spatial[l], w_proj_a=w_proj_a[l], w_proj_b=w_proj_b[l],
                  w_proj_c=w_proj_c[l], w_gate=w_gate[l], b_gate=b_gate[l], w_out=w_out[l],
                  ln_g=ln_g[l], ln_b=ln_b[l])
        xp, ka, va, kb, vb = context_layer(xp, c_ctx, lw)
        ka_l.append(ka)
        va_l.append(va)
        kb_l.append(kb)
        vb_l.append(vb)
        xs = latent_layer(xs, c, lw, cache_a_k[:, l], cache_a_v[:, l], cache_b_k[:, l], cache_b_v[:, l], cos, sin)
    state_a_k = jnp.stack(ka_l, axis=1)
    state_a_v = jnp.stack(va_l, axis=1)
    state_b_k = jnp.stack(kb_l, axis=1)
    state_b_v = jnp.stack(vb_l, axis=1)
    return (xp, xs, state_a_k, state_a_v, state_b_k, state_b_v)
```

```python
import functools

import jax
import jax.numpy as jnp
from jax import lax
from jax.experimental import pallas as pl
from jax.experimental.pallas import tpu as pltpu

D_MODEL = 1024
DEPTH = 4
GRID_W = 64
HEAD_DIM = 64
N_HEADS = 8
N_KV = 2
KV_WIDTH = N_KV * HEAD_DIM
WIDTH = 512
C_GROUPS = 4
CHUNK = 128
WINDOW = 128
IN_WIDTH = 4096
ROPE_BASE = 10000.0
ROPE_FREQS = HEAD_DIM // 4
EPS = 1e-6
NEG_INF = -1e30
DEEPNORM_ALPHA = (2 * DEPTH) ** 0.25

LANES = 128
MOD_ROWS = 8

_QA, _KA, _VA, _GA = 0, 512, 640, 768
_QB, _KB, _VB, _GB = 1280, 1792, 1920, 2048
_UC, _VC, _GC = 2560, 3072, 3584

(PZ_QA, PZ_KA, PZ_VA, PZ_GA, PZ_QB, PZ_KB, PZ_VB, PZ_GB, PZ_UC, PZ_VN, PZ_GC) = range(11)
PZ_WIDTH = 11 * WIDTH

BF16 = jnp.bfloat16
F32 = jnp.float32

VMEM_LIMIT = 56 * 1024 * 1024


def _silu(t):
    return t * jax.nn.sigmoid(t)


def _dot(a, b):
    return jnp.dot(a, b, preferred_element_type=F32)


def _dot_nt(a, b):
    return lax.dot_general(a, b, (((1,), (1,)), ((), ())), preferred_element_type=F32)


def _mod_kernel(c_ref, w_ref, b_ref, o_ref):
    s = _silu(c_ref[...]).astype(BF16)
    o_ref[0] = _dot(s, w_ref[0].astype(BF16)) + b_ref[0]


def _modulation(cvecs, w_mod, b_mod):
    tn = 1024
    return pl.pallas_call(
        _mod_kernel,
        grid=(DEPTH, 3 * D_MODEL // tn),
        in_specs=[
            pl.BlockSpec((MOD_ROWS, D_MODEL), lambda l, j: (0, 0)),
            pl.BlockSpec((1, D_MODEL, tn), lambda l, j: (l, 0, j)),
            pl.BlockSpec((1, 1, tn), lambda l, j: (l, 0, j)),
        ],
        out_specs=pl.BlockSpec((1, MOD_ROWS, tn), lambda l, j: (l, 0, j)),
        out_shape=jax.ShapeDtypeStruct((DEPTH, MOD_ROWS, 3 * D_MODEL), F32),
        compiler_params=pltpu.CompilerParams(
            dimension_semantics=("arbitrary", "arbitrary"), vmem_limit_bytes=VMEM_LIMIT),
        name="modulation",
    )(cvecs, w_mod, b_mod.reshape(DEPTH, 1, 3 * D_MODEL))


def _modulated_norm(x, mod):
    mu = jnp.mean(x, axis=-1, keepdims=True)
    xc = x - mu
    var = jnp.mean(xc * xc, axis=-1, keepdims=True)
    xn = xc * lax.rsqrt(var + EPS)
    return xn * (1.0 + mod[:, D_MODEL:2 * D_MODEL]) + mod[:, :D_MODEL]


def _front_kernel(*refs, rope, with_state):
    refs = list(refs)
    x_ref, mod_ref, w_ref, bd_ref, qn_ref, kn_ref, sg_ref, sb_ref = refs[:8]
    refs = refs[8:]
    if rope:
        cos_ref, sin_ref = refs[:2]
        refs = refs[2:]
    pz_ref = refs[0]
    st_ref = refs[1] if with_state else None

    tm = x_ref.shape[0]
    h = _modulated_norm(x_ref[...], mod_ref[0]).astype(BF16)

    lane = lax.broadcasted_iota(jnp.int32, (tm, LANES), 1)
    low_head = lane < HEAD_DIM
    first_half = (lane & (HEAD_DIM // 2)) == 0

    def proj(lo, width):
        return _dot(h, w_ref[:, lo:lo + width])

    def head_rms(t, gain):
        width = t.shape[1]
        ms = _dot((t * t).astype(BF16), bd_ref[:width, :width])
        return t * lax.rsqrt(ms + EPS) * gain

    def rotary(t):
        if not rope:
            return t
        swapped = jnp.where(first_half, pltpu.roll(t, LANES - HEAD_DIM // 2, 1),
                            pltpu.roll(t, HEAD_DIM // 2, 1))
        return t * cos_ref[...] + swapped * sin_ref[...]

    def store(block, off, val):
        pz_ref[:, block * WIDTH + off: block * WIDTH + off + val.shape[1]] = val.astype(BF16)

    def store_kv(block, t):
        r = pltpu.roll(t, HEAD_DIM, 1)
        zero = jnp.zeros_like(t)
        store(block, 0 * LANES, jnp.where(low_head, t, zero))
        store(block, 1 * LANES, jnp.where(low_head, zero, r))
        store(block, 2 * LANES, jnp.where(low_head, r, zero))
        store(block, 3 * LANES, jnp.where(low_head, zero, t))

    def q_path(lo, block, gain):
        q = proj(lo, WIDTH)
        if gain is not None:
            q = head_rms(q, gain)
        q = q * (HEAD_DIM ** -0.5)
        for c in range(WIDTH // LANES):
            store(block, c * LANES, rotary(q[:, c * LANES:(c + 1) * LANES]))

    q_path(_QA, PZ_QA, qn_ref[...])
    ka = head_rms(proj(_KA, KV_WIDTH), kn_ref[...])
    va = proj(_VA, KV_WIDTH)
    store_kv(PZ_KA, rotary(ka))
    store_kv(PZ_VA, va)
    store(PZ_GA, 0, _silu(proj(_GA, WIDTH)))

    q_path(_QB, PZ_QB, None)
    kb = proj(_KB, KV_WIDTH)
    vb = proj(_VB, KV_WIDTH)
    store_kv(PZ_KB, rotary(kb))
    store_kv(PZ_VB, vb)
    store(PZ_GB, 0, _silu(proj(_GB, WIDTH)))

    if with_state:
        st_ref[:, 0 * KV_WIDTH:1 * KV_WIDTH] = ka
        st_ref[:, 1 * KV_WIDTH:2 * KV_WIDTH] = va
        st_ref[:, 2 * KV_WIDTH:3 * KV_WIDTH] = kb
        st_ref[:, 3 * KV_WIDTH:4 * KV_WIDTH] = vb

    store(PZ_UC, 0, proj(_UC, WIDTH))
    vc = proj(_VC, WIDTH)
    mu = jnp.mean(vc, axis=-1, keepdims=True)
    vcc = vc - mu
    var = jnp.mean(vcc * vcc, axis=-1, keepdims=True)
    store(PZ_VN, 0, vcc * lax.rsqrt(var + EPS) * sg_ref[...] + sb_ref[...])
    store(PZ_GC, 0, _silu(proj(_GC, WIDTH)))


def _front(x, mod, w_in, bd, qn, kn, sg, sb, rope_tables, *, tm, rows_per_mod, with_state):
    m = x.shape[0]
    rope = rope_tables is not None
    tiles_per_mod = rows_per_mod // tm
    const = lambda i: (0, 0)
    in_specs = [
        pl.BlockSpec((tm, D_MODEL), lambda i: (i, 0)),
        pl.BlockSpec((1, 1, 3 * D_MODEL), lambda i: (i // tiles_per_mod, 0, 0)),
        pl.BlockSpec((D_MODEL, IN_WIDTH), const),
        pl.BlockSpec((WIDTH, WIDTH), const),
        pl.BlockSpec((1, WIDTH), const),
        pl.BlockSpec((1, KV_WIDTH), const),
        pl.BlockSpec((1, WIDTH), const),
        pl.BlockSpec((1, WIDTH), const),
    ]
    args = [x, mod, w_in, bd, qn, kn, sg, sb]
    if rope:
        n_tok = rope_tables[0].shape[0]
        tiles_per_seq = n_tok // tm
        in_specs += [pl.BlockSpec((tm, LANES), lambda i: (i % tiles_per_seq, 0))] * 2
        args += list(rope_tables)
    out_specs = [pl.BlockSpec((tm, PZ_WIDTH), lambda i: (i, 0))]
    out_shape = [jax.ShapeDtypeStruct((m, PZ_WIDTH), BF16)]
    if with_state:
        out_specs.append(pl.BlockSpec((tm, 4 * KV_WIDTH), lambda i: (i, 0)))
        out_shape.append(jax.ShapeDtypeStruct((m, 4 * KV_WIDTH), F32))
    return pl.pallas_call(
        functools.partial(_front_kernel, rope=rope, with_state=with_state),
        grid=(m // tm,),
        in_specs=in_specs,
        out_specs=out_specs,
        out_shape=out_shape,
        compiler_params=pltpu.CompilerParams(
            dimension_semantics=("arbitrary",), vmem_limit_bytes=VMEM_LIMIT),
        name="front_latent" if rope else "front_context",
    )(*args)


def _kv_views(ref, rows, head, block=0):
    base = block * WIDTH + 2 * head * LANES
    return ref[rows, base:base + LANES], ref[rows, base + LANES:base + 2 * LANES]


def _pair_attention(q_pair, sources, sinks, low_head):
    scores = ([], [])
    for k_lo, k_hi, _, _, mask in sources:
        for hh, kk in enumerate((k_lo, k_hi)):
            s = _dot_nt(q_pair, kk)
            if mask is not None:
                s = jnp.where(mask, s, NEG_INF)
            scores[hh].append(s)
    out = None
    inv = []
    for hh in range(2):
        m = None
        for s in scores[hh]:
            ms = jnp.max(s, axis=-1, keepdims=True)
            m = ms if m is None else jnp.maximum(m, ms)
        if sinks is not None:
            m = jnp.maximum(m, sinks[hh])
            denom = jnp.exp(sinks[hh] - m)
        else:
            denom = jnp.zeros_like(m)
        for s, src in zip(scores[hh], sources):
            p = jnp.exp(s - m)
            denom = denom + jnp.sum(p, axis=-1, keepdims=True)
            pv = _dot(p.astype(BF16), src[2 + hh])
            out = pv if out is None else out + pv
        inv.append(1.0 / denom)
    return out * jnp.where(low_head, inv[0], inv[1])


def _spatial_gate(pz_ref, ws_ref, bs_ref, z_ref, n_rows):
    for ch in range(n_rows // CHUNK):
        rows = slice(ch * CHUNK, (ch + 1) * CHUNK)
        for g in range(C_GROUPS):
            cols = lambda blk: slice(blk * WIDTH + g * LANES, blk * WIDTH + (g + 1) * LANES)
            mixed = _dot(ws_ref[g], pz_ref[rows, cols(PZ_VN)]) + bs_ref[:, g * LANES:(g + 1) * LANES]
            zc = pz_ref[rows, cols(PZ_UC)].astype(F32) * mixed * pz_ref[rows, cols(PZ_GC)].astype(F32)
            z_ref[rows, 2 * WIDTH + g * LANES: 2 * WIDTH + (g + 1) * LANES] = zc.astype(BF16)


def _mixer_context_kernel(pz_ref, sink_ref, ws_ref, bs_ref, z_ref):
    t = pz_ref.shape[0]
    rows = slice(0, t)
    low_head = lax.broadcasted_iota(jnp.int32, (t, LANES), 1) < HEAD_DIM
    for branch, (pq, pk, pv, pg) in enumerate(((PZ_QA, PZ_KA, PZ_VA, PZ_GA), (PZ_QB, PZ_KB, PZ_VB, PZ_GB))):
        for pair in range(N_HEADS // 2):
            head = pair // 2
            k_lo, k_hi = _kv_views(pz_ref, rows, head, pk)
            v_lo, v_hi = _kv_views(pz_ref, rows, head, pv)
            q_pair = pz_ref[rows, pq * WIDTH + pair * LANES: pq * WIDTH + (pair + 1) * LANES]
            sinks = (sink_ref[2 * pair], sink_ref[2 * pair + 1]) if branch == 1 else None
            o = _pair_attention(q_pair, [(k_lo, k_hi, v_lo, v_hi, None)], sinks, low_head)
            gate = pz_ref[rows, pg * WIDTH + pair * LANES: pg * WIDTH + (pair + 1) * LANES].astype(F32)
            z_ref[rows, branch * WIDTH + pair * LANES: branch * WIDTH + (pair + 1) * LANES] = (o * gate).astype(BF16)
    _spatial_gate(pz_ref, ws_ref, bs_ref, z_ref, t)


def _mixer_context(pz, sink, ws, bs, *, seq):
    m = pz.shape[0]
    return pl.pallas_call(
        _mixer_context_kernel,
        grid=(m // seq,),
        in_specs=[
            pl.BlockSpec((seq, PZ_WIDTH), lambda i: (i, 0)),
            pl.BlockSpec(memory_space=pltpu.SMEM),
            pl.BlockSpec((C_GROUPS, CHUNK, CHUNK), lambda i: (0, 0, 0)),
            pl.BlockSpec((CHUNK, WIDTH), lambda i: (0, 0)),
        ],
        out_specs=pl.BlockSpec((seq, 3 * WIDTH), lambda i: (i, 0)),
        out_shape=jax.ShapeDtypeStruct((m, 3 * WIDTH), BF16),
        compiler_params=pltpu.CompilerParams(
            dimension_semantics=("arbitrary",), vmem_limit_bytes=VMEM_LIMIT),
        name="mixer_context",
    )(pz, sink, ws, bs)


def _mixer_latent_kernel(pzq_ref, ka_ref, va_ref, kb_ref, vb_ref,
                         cak_ref, cav_ref, cbk_ref, cbv_ref, sink_ref, ws_ref, bs_ref,
                         z_ref, cak_s, cav_s, cbk_s, cbv_s, *, seq):
    tq = pzq_ref.shape[0]
    past = cak_ref.shape[2]
    band = tq + 2 * WINDOW
    qi = pl.program_id(1)

    @pl.when(qi == 0)
    def _():
        low = lax.broadcasted_iota(jnp.int32, (past, LANES), 1) < HEAD_DIM
        for src, dst in ((cak_ref, cak_s), (cav_ref, cav_s), (cbk_ref, cbk_s), (cbv_ref, cbv_s)):
            t = src[0, 0]
            r = pltpu.roll(t, HEAD_DIM, 1)
            zero = jnp.zeros_like(t)
            dst[:, 0 * LANES:1 * LANES] = jnp.where(low, t, zero).astype(BF16)
            dst[:, 1 * LANES:2 * LANES] = jnp.where(low, zero, r).astype(BF16)
            dst[:, 2 * LANES:3 * LANES] = jnp.where(low, r, zero).astype(BF16)
            dst[:, 3 * LANES:4 * LANES] = jnp.where(low, zero, t).astype(BF16)

    low_head = lax.broadcasted_iota(jnp.int32, (tq, LANES), 1) < HEAD_DIM
    all_rows = slice(0, seq)
    ctx_rows = slice(0, past)

    start = pl.multiple_of(jnp.clip(qi * tq - WINDOW, 0, seq - band), WINDOW)
    win_rows = pl.ds(start, band)
    q_pos = qi * tq + lax.broadcasted_iota(jnp.int32, (tq, band), 0)
    k_pos = start + lax.broadcasted_iota(jnp.int32, (tq, band), 1)
    in_window = jnp.abs(q_pos - k_pos) <= WINDOW

    for pair in range(N_HEADS // 2):
        head = pair // 2
        q_cols = lambda blk: slice(blk * WIDTH + pair * LANES, blk * WIDTH + (pair + 1) * LANES)

        k_lo, k_hi = _kv_views(ka_ref, all_rows, head)
        v_lo, v_hi = _kv_views(va_ref, all_rows, head)
        ck_lo, ck_hi = _kv_views(cak_s, ctx_rows, head)
        cv_lo, cv_hi = _kv_views(cav_s, ctx_rows, head)
        o = _pair_attention(pzq_ref[:, q_cols(PZ_QA)],
                            [(k_lo, k_hi, v_lo, v_hi, None), (ck_lo, ck_hi, cv_lo, cv_hi, None)],
                            None, low_head)
        z_ref[:, pair * LANES:(pair + 1) * LANES] = (
            o * pzq_ref[:, q_cols(PZ_GA)].astype(F32)).astype(BF16)

        k_lo, k_hi = _kv_views(kb_ref, win_rows, head)
        v_lo, v_hi = _kv_views(vb_ref, win_rows, head)
        ck_lo, ck_hi = _kv_views(cbk_s, ctx_rows, head)
        cv_lo, cv_hi = _kv_views(cbv_s, ctx_rows, head)
        o = _pair_attention(pzq_ref[:, q_cols(PZ_QB)],
                            [(k_lo, k_hi, v_lo, v_hi, in_window), (ck_lo, ck_hi, cv_lo, cv_hi, None)],
                            (sink_ref[2 * pair], sink_ref[2 * pair + 1]), low_head)
        z_ref[:, WIDTH + pair * LANES: WIDTH + (pair + 1) * LANES] = (
            o * pzq_ref[:, q_cols(PZ_GB)].astype(F32)).astype(BF16)

    _spatial_gate(pzq_ref, ws_ref, bs_ref, z_ref, tq)


def _mixer_latent(pz, cak, cav, cbk, cbv, sink, ws, bs, *, layer, seq, tq):
    m = pz.shape[0]
    n_seq = m // seq
    nq = seq // tq
    past = cak.shape[2]
    kv_spec = lambda blk: pl.BlockSpec((seq, WIDTH), lambda b, q: (b, blk))
    cache_spec = pl.BlockSpec((1, 1, past, KV_WIDTH), lambda b, q: (b, layer, 0, 0))
    return pl.pallas_call(
        functools.partial(_mixer_latent_kernel, seq=seq),
        grid=(n_seq, nq),
        in_specs=[
            pl.BlockSpec((tq, PZ_WIDTH), lambda b, q: (b * nq + q, 0)),
            kv_spec(PZ_KA), kv_spec(PZ_VA), kv_spec(PZ_KB), kv_spec(PZ_VB),
            cache_spec, cache_spec, cache_spec, cache_spec,
            pl.BlockSpec(memory_space=pltpu.SMEM),
            pl.BlockSpec((C_GROUPS, CHUNK, CHUNK), lambda b, q: (0, 0, 0)),
            pl.BlockSpec((CHUNK, WIDTH), lambda b, q: (0, 0)),
        ],
        out_specs=pl.BlockSpec((tq, 3 * WIDTH), lambda b, q: (b * nq + q, 0)),
        out_shape=jax.ShapeDtypeStruct((m, 3 * WIDTH), BF16),
        scratch_shapes=[pltpu.VMEM((past, 4 * LANES), BF16)] * 4,
        compiler_params=pltpu.CompilerParams(
            dimension_semantics=("arbitrary", "arbitrary"), vmem_limit_bytes=VMEM_LIMIT),
        name="mixer_latent",
    )(pz, pz, pz, pz, pz, cak, cav, cbk, cbv, sink, ws, bs)


def _back_kernel(x_ref, z_ref, mod_ref, wg_ref, bg_ref, wpa_ref, wpb_ref, wpc_ref, wo_ref,
                 lg_ref, lb_ref, o_ref):
    x = x_ref[...]
    mod = mod_ref[0]
    h = _modulated_norm(x, mod).astype(BF16)
    mix = None
    for br, wp_ref in enumerate((wpa_ref, wpb_ref, wpc_ref)):
        cols = slice(br * D_MODEL, (br + 1) * D_MODEL)
        g = jax.nn.sigmoid(_dot(h, wg_ref[:, cols]) + bg_ref[:, cols])
        y = _dot(z_ref[:, br * WIDTH:(br + 1) * WIDTH], wp_ref[...])
        mix = g * y if mix is None else mix + g * y
    y = _dot(mix.astype(BF16), wo_ref[...])
    r = DEEPNORM_ALPHA * x + mod[:, 2 * D_MODEL:] * y
    mu = jnp.mean(r, axis=-1, keepdims=True)
    rc = r - mu
    var = jnp.mean(rc * rc, axis=-1, keepdims=True)
    o_ref[...] = rc * lax.rsqrt(var + EPS) * lg_ref[...] + lb_ref[...]


def _back(x, z, mod, wg, bg, wpa, wpb, wpc, wo, lg, lb, *, tm, rows_per_mod, name):
    m = x.shape[0]
    tiles_per_mod = rows_per_mod // tm
    const = lambda i: (0, 0)
    return pl.pallas_call(
        _back_kernel,
        grid=(m // tm,),
        in_specs=[
            pl.BlockSpec((tm, D_MODEL), lambda i: (i, 0)),
            pl.BlockSpec((tm, 3 * WIDTH), lambda i: (i, 0)),
            pl.BlockSpec((1, 1, 3 * D_MODEL), lambda i: (i // tiles_per_mod, 0, 0)),
            pl.BlockSpec((D_MODEL, 3 * D_MODEL), const),
            pl.BlockSpec((1, 3 * D_MODEL), const),
            pl.BlockSpec((WIDTH, D_MODEL), const),
            pl.BlockSpec((WIDTH, D_MODEL), const),
            pl.BlockSpec((WIDTH, D_MODEL), const),
            pl.BlockSpec((D_MODEL, D_MODEL), const),
            pl.BlockSpec((1, D_MODEL), const),
            pl.BlockSpec((1, D_MODEL), const),
        ],
        out_specs=pl.BlockSpec((tm, D_MODEL), lambda i: (i, 0)),
        out_shape=jax.ShapeDtypeStruct((m, D_MODEL), F32),
        compiler_params=pltpu.CompilerParams(
            dimension_semantics=("arbitrary",), vmem_limit_bytes=VMEM_LIMIT),
        name=name,
    )(x, z, mod, wg, bg, wpa, wpb, wpc, wo, lg, lb)


def _rope_tables(n_tokens):
    rows = n_tokens // GRID_W
    row = jnp.repeat(jnp.arange(rows, dtype=F32), GRID_W)
    col = jnp.tile(jnp.arange(GRID_W, dtype=F32), rows)
    inv_freq = jnp.power(ROPE_BASE, -jnp.arange(ROPE_FREQS, dtype=F32) / ROPE_FREQS)
    ang = jnp.concatenate([row[:, None] * inv_freq, col[:, None] * inv_freq], axis=-1)
    cos, sin = jnp.cos(ang), jnp.sin(ang)
    return jnp.tile(cos, (1, 4)), jnp.concatenate([-sin, sin, -sin, sin], axis=-1)


def kernel(x_prompt, x_sample, cache_a_k, cache_a_v, cache_b_k, cache_b_v, c, c_ctx, w_mod, b_mod,
           w_in, q_norm, k_norm, sink, sgu_g, sgu_b, w_spatial, b_spatial, w_proj_a, w_proj_b,
           w_proj_c, w_gate, b_gate, w_out, ln_g, ln_b):
    batch, seq, _ = x_prompt.shape
    dec_batch, dec_seq, _ = x_sample.shape
    past = cache_a_k.shape[2]

    cvecs = jnp.concatenate(
        [c_ctx[None], c, jnp.zeros((MOD_ROWS - 1 - dec_batch, D_MODEL), F32)], axis=0)
    mod = _modulation(cvecs, w_mod, b_mod)

    rope_tables = _rope_tables(dec_seq)
    block_avg = jnp.kron(jnp.eye(N_HEADS, dtype=F32),
                         jnp.full((HEAD_DIM, HEAD_DIM), 1.0 / HEAD_DIM, F32)).astype(BF16)
    caches = [t.reshape(dec_batch, DEPTH, past, KV_WIDTH)
              for t in (cache_a_k, cache_a_v, cache_b_k, cache_b_v)]

    xp = x_prompt.reshape(batch * seq, D_MODEL)
    xs = x_sample.reshape(dec_batch * dec_seq, D_MODEL)
    states = []
    for l in range(DEPTH):
        w_in_l = w_in[l].astype(BF16)
        wg = w_gate[l].astype(BF16)
        wpa, wpb, wpc = (w[l].astype(BF16) for w in (w_proj_a, w_proj_b, w_proj_c))
        wo = w_out[l].astype(BF16)
        ws = w_spatial[l].astype(BF16)
        bs = jnp.repeat(b_spatial[l].T, LANES, axis=1)
        qn = jnp.tile(q_norm[l], N_HEADS)[None]
        kn = jnp.tile(k_norm[l], N_KV)[None]
        sg, sb = sgu_g[l][None], sgu_b[l][None]
        bg, lg, lb = b_gate[l][None], ln_g[l][None], ln_b[l][None]
        mod_ctx = mod[l, 0:1][None]
        mod_lat = mod[l, 1:1 + dec_batch][:, None]

        pz, st = _front(xp, mod_ctx, w_in_l, block_avg, qn, kn, sg, sb, None,
                        tm=512, rows_per_mod=batch * seq, with_state=True)
        states.append(st)
        z = _mixer_context(pz, sink[l], ws, bs, seq=seq)
        xp = _back(xp, z, mod_ctx, wg, bg, wpa, wpb, wpc, wo, lg, lb,
                   tm=512, rows_per_mod=batch * seq, name="back_context")

        (pz,) = _front(xs, mod_lat, w_in_l, block_avg, qn, kn, sg, sb, rope_tables,
                       tm=512, rows_per_mod=dec_seq, with_state=False)
        z = _mixer_latent(pz, *caches, sink[l], ws, bs, layer=l, seq=dec_seq, tq=256)
        xs = _back(xs, z, mod_lat, wg, bg, wpa, wpb, wpc, wo, lg, lb,
                   tm=512, rows_per_mod=dec_seq, name="back_latent")

    st = jnp.stack(states, axis=0).reshape(DEPTH, batch, seq, 4, N_KV, HEAD_DIM)
    st = jnp.transpose(st, (3, 1, 0, 2, 4, 5))
    return (xp.reshape(batch, seq, D_MODEL), xs.reshape(dec_batch, dec_seq, D_MODEL),
            st[0], st[1], st[2], st[3])
```

```python
import functools

import jax
import jax.numpy as jnp
from jax import lax
from jax.experimental import pallas as pl
from jax.experimental.pallas import tpu as pltpu

D_MODEL = 1024
DEPTH = 4
GRID_W = 64
HEAD_DIM = 64
HALF = HEAD_DIM // 2
N_HEADS = 8
N_KV = 2
KV_WIDTH = N_KV * HEAD_DIM
WIDTH = 512
C_GROUPS = 4
CHUNK = 128
WINDOW = 128
IN_WIDTH = 4096
ROPE_BASE = 10000.0
ROPE_FREQS = HEAD_DIM // 4
EPS = 1e-6
NEG_INF = -1e30
DEEPNORM_ALPHA = (2 * DEPTH) ** 0.25

LANES = 128
MOD_ROWS = 8

_QA, _KA, _VA, _GA = 0, 512, 640, 768
_QB, _KB, _VB, _GB = 1280, 1792, 1920, 2048
_UC, _VC, _GC = 2560, 3072, 3584

(PZ_QA, PZ_VA, PZ_GA, PZ_QB, PZ_VB, PZ_GB, PZ_UC, PZ_VN, PZ_GC) = range(9)
PZ_WIDTH = 9 * WIDTH

BF16 = jnp.bfloat16
F32 = jnp.float32

VMEM_LIMIT = 56 * 1024 * 1024


def _silu(t):
    return t * jax.nn.sigmoid(t)


def _dot(a, b):
    return jnp.dot(a, b, preferred_element_type=F32)


def _dot_nt(a, b):
    return lax.dot_general(a, b, (((1,), (1,)), ((), ())), preferred_element_type=F32)


def _layer_block(shape, layer):
    zeros = (0,) * len(shape)
    return pl.BlockSpec((None,) + tuple(shape), lambda *_: (layer,) + zeros)


def _mod_kernel(c_ref, w_ref, b_ref, o_ref):
    s = _silu(c_ref[...]).astype(BF16)
    o_ref[0] = _dot(s, w_ref[0].astype(BF16)) + b_ref[0]


def _modulation(cvecs, w_mod, b_mod):
    tn = 1024
    return pl.pallas_call(
        _mod_kernel,
        grid=(DEPTH, 3 * D_MODEL // tn),
        in_specs=[
            pl.BlockSpec((MOD_ROWS, D_MODEL), lambda l, j: (0, 0)),
            pl.BlockSpec((1, D_MODEL, tn), lambda l, j: (l, 0, j)),
            pl.BlockSpec((1, 1, tn), lambda l, j: (l, 0, j)),
        ],
        out_specs=pl.BlockSpec((1, MOD_ROWS, tn), lambda l, j: (l, 0, j)),
        out_shape=jax.ShapeDtypeStruct((DEPTH, MOD_ROWS, 3 * D_MODEL), F32),
        compiler_params=pltpu.CompilerParams(
            dimension_semantics=("arbitrary", "arbitrary"), vmem_limit_bytes=VMEM_LIMIT),
        name="modulation",
    )(cvecs, w_mod, b_mod.reshape(DEPTH, 1, 3 * D_MODEL))


def _modulated_norm(x, mod):
    mu = jnp.mean(x, axis=-1, keepdims=True)
    xc = x - mu
    var = jnp.mean(xc * xc, axis=-1, keepdims=True)
    xn = xc * lax.rsqrt(var + EPS)
    return xn * (1.0 + mod[:, D_MODEL:2 * D_MODEL]) + mod[:, :D_MODEL]


def _write_token_blocks(ref, t):
    if len(ref.shape) == 4:
        for j in range(ref.shape[1]):
            ref[0, j] = t[:, j * LANES:(j + 1) * LANES].astype(ref.dtype)
    else:
        w = ref.shape[2]
        for j in range(ref.shape[0]):
            ref[j] = t[:, j * w:(j + 1) * w].astype(ref.dtype)


def _front_kernel(*refs, rope, with_state):
    refs = list(refs)
    x_ref, mod_ref, w_ref, wk_ref, bd_ref, qn_ref, kn_ref, sg_ref, sb_ref = refs[:9]
    refs = refs[9:]
    if rope:
        cos_ref, sin_ref, cos_t_ref, sin_t_ref = refs[:4]
        refs = refs[4:]
    pz_ref, kta_ref, ktb_ref = refs[:3]
    st_refs = refs[3:] if with_state else None

    tm = x_ref.shape[0]
    h = _modulated_norm(x_ref[...], mod_ref[0]).astype(BF16)

    lane = lax.broadcasted_iota(jnp.int32, (tm, LANES), 1)
    low_head = lane < HEAD_DIM
    first_half = (lane & HALF) == 0

    def proj(lo, width):
        return _dot(h, w_ref[:, lo:lo + width])

    def rotary(t):
        if not rope:
            return t
        swapped = jnp.where(first_half, pltpu.roll(t, LANES - HALF, 1), pltpu.roll(t, HALF, 1))
        return t * cos_ref[...] + swapped * sin_ref[...]

    def rotary_t(t):
        if not rope:
            return t
        x1, x2 = t[:HALF], t[HALF:]
        c, s = cos_t_ref[...], sin_t_ref[...]
        return jnp.concatenate([x1 * c - x2 * s, x2 * c + x1 * s], axis=0)

    def store(block, off, val):
        pz_ref[:, block * WIDTH + off: block * WIDTH + off + val.shape[1]] = val.astype(BF16)

    def store_v(block, t):
        r = pltpu.roll(t, HEAD_DIM, 1)
        zero = jnp.zeros_like(t)
        store(block, 0 * LANES, jnp.where(low_head, t, zero))
        store(block, 1 * LANES, jnp.where(low_head, zero, r))
        store(block, 2 * LANES, jnp.where(low_head, r, zero))
        store(block, 3 * LANES, jnp.where(low_head, zero, t))

    def q_path(lo, block, gain):
        q = proj(lo, WIDTH)
        if gain is not None:
            ms = _dot((q * q).astype(BF16), bd_ref[...])
            q = q * lax.rsqrt(ms + EPS) * gain
        q = q * (HEAD_DIM ** -0.5)
        for c in range(WIDTH // LANES):
            store(block, c * LANES, rotary(q[:, c * LANES:(c + 1) * LANES]))

    k_t = _dot_nt(wk_ref[...], h)
    ka_heads, kb_heads = [], []
    for g in range(N_KV):
        t = k_t[g * HEAD_DIM:(g + 1) * HEAD_DIM]
        ms = jnp.mean(t * t, axis=0, keepdims=True)
        ka_heads.append(t * lax.rsqrt(ms + EPS) * kn_ref[g * HEAD_DIM:(g + 1) * HEAD_DIM])
        kb_heads.append(k_t[KV_WIDTH + g * HEAD_DIM: KV_WIDTH + (g + 1) * HEAD_DIM])
    _write_token_blocks(kta_ref, jnp.concatenate([rotary_t(t) for t in ka_heads], axis=0))
    _write_token_blocks(ktb_ref, jnp.concatenate([rotary_t(t) for t in kb_heads], axis=0))

    q_path(_QA, PZ_QA, qn_ref[...])
    va = proj(_VA, KV_WIDTH)
    store_v(PZ_VA, va)
    store(PZ_GA, 0, _silu(proj(_GA, WIDTH)))

    q_path(_QB, PZ_QB, None)
    vb = proj(_VB, KV_WIDTH)
    store_v(PZ_VB, vb)
    store(PZ_GB, 0, _silu(proj(_GB, WIDTH)))

    if with_state:
        _write_token_blocks(st_refs[0], jnp.concatenate(ka_heads, axis=0))
        _write_token_blocks(st_refs[1], va.T)
        _write_token_blocks(st_refs[2], jnp.concatenate(kb_heads, axis=0))
        _write_token_blocks(st_refs[3], vb.T)

    store(PZ_UC, 0, proj(_UC, WIDTH))
    vc = proj(_VC, WIDTH)
    mu = jnp.mean(vc, axis=-1, keepdims=True)
    vcc = vc - mu
    var = jnp.mean(vcc * vcc, axis=-1, keepdims=True)
    store(PZ_VN, 0, vcc * lax.rsqrt(var + EPS) * sg_ref[...] + sb_ref[...])
    store(PZ_GC, 0, _silu(proj(_GC, WIDTH)))


def _front(x, mod, w_in, wk, bd, qn, kn, sg, sb, rope_tables, *, layer, tm, seq, with_state):
    m = x.shape[0]
    n_seq = m // seq
    rope = rope_tables is not None
    const = lambda i: (0, 0)
    if tm >= seq:
        nb = tm // seq
        mod_idx = lambda i: (0, 0, 0)
        kt_shape, kt_spec = (n_seq, KV_WIDTH, seq), pl.BlockSpec((nb, KV_WIDTH, seq), lambda i: (i, 0, 0))
        ktb_shape, ktb_spec = kt_shape, kt_spec
    else:
        tps = seq // tm
        mod_idx = lambda i: (i // tps, 0, 0)
        kt_shape = (n_seq, KV_WIDTH, seq)
        kt_spec = pl.BlockSpec((1, KV_WIDTH, tm), lambda i: (i // tps, 0, i % tps))
        ktb_shape = (n_seq, seq // LANES, KV_WIDTH, LANES)
        ktb_spec = pl.BlockSpec((1, tm // LANES, KV_WIDTH, LANES), lambda i: (i // tps, i % tps, 0, 0))
    in_specs = [
        pl.BlockSpec((tm, D_MODEL), lambda i: (i, 0)),
        pl.BlockSpec((1, 1, 3 * D_MODEL), mod_idx),
        _layer_block((D_MODEL, IN_WIDTH), layer),
        _layer_block((2 * KV_WIDTH, D_MODEL), layer),
        pl.BlockSpec((WIDTH, WIDTH), const),
        pl.BlockSpec((1, WIDTH), const),
        pl.BlockSpec((KV_WIDTH, tm), const),
        pl.BlockSpec((1, WIDTH), const),
        pl.BlockSpec((1, WIDTH), const),
    ]
    args = [x, mod, w_in, wk, bd, qn, kn, sg, sb]
    if rope:
        cos, sin, cos_t, sin_t = rope_tables
        in_specs += [pl.BlockSpec((tm, LANES), lambda i: (i % tps, 0))] * 2
        in_specs += [pl.BlockSpec((HALF, tm), lambda i: (0, i % tps))] * 2
        args += [cos, sin, cos_t, sin_t]
    out_specs = [pl.BlockSpec((tm, PZ_WIDTH), lambda i: (i, 0)), kt_spec, ktb_spec]
    out_shape = [jax.ShapeDtypeStruct((m, PZ_WIDTH), BF16),
                 jax.ShapeDtypeStruct(kt_shape, BF16), jax.ShapeDtypeStruct(ktb_shape, BF16)]
    if with_state:
        out_specs += [kt_spec] * 4
        out_shape += [jax.ShapeDtypeStruct(kt_shape, F32)] * 4
    return pl.pallas_call(
        functools.partial(_front_kernel, rope=rope, with_state=with_state),
        grid=(m // tm,),
        in_specs=in_specs,
        out_specs=out_specs,
        out_shape=out_shape,
        compiler_params=pltpu.CompilerParams(
            dimension_semantics=("arbitrary",), vmem_limit_bytes=VMEM_LIMIT),
        name="front_latent" if rope else "front_context",
    )(*args)


def _v_views(ref, rows, head, block=0):
    base = block * WIDTH + 2 * head * LANES
    return ref[rows, base:base + LANES], ref[rows, base + LANES:base + 2 * LANES]


def _dup_rows(kt):
    return jnp.concatenate([kt, kt], axis=0)


def _pair_attention(q_pair, sources, sinks, low_head):
    zero = jnp.zeros_like(q_pair)
    q_heads = (jnp.where(low_head, q_pair, zero), jnp.where(low_head, zero, q_pair))
    out = None
    inv = []
    for hh in range(2):
        scores = []
        for k_t, _, _, mask in sources:
            s = _dot(q_heads[hh], k_t)
            if mask is not None:
                s = jnp.where(mask, s, NEG_INF)
            scores.append(s)
        m = None
        for s in scores:
            ms = jnp.max(s, axis=-1, keepdims=True)
            m = ms if m is None else jnp.maximum(m, ms)
        if sinks is not None:
            m = jnp.maximum(m, sinks[hh])
            denom = jnp.exp(sinks[hh] - m)
        else:
            denom = jnp.zeros_like(m)
        for s, src in zip(scores, sources):
            p = jnp.exp(s - m)
            denom = denom + jnp.sum(p, axis=-1, keepdims=True)
            pv = _dot(p.astype(BF16), src[1 + hh])
            out = pv if out is None else out + pv
        inv.append(1.0 / denom)
    return out * jnp.where(low_head, inv[0], inv[1])


def _spatial_gate(pz_ref, ws_ref, bs_ref, z_ref, n_rows):
    for ch in range(n_rows // CHUNK):
        rows = slice(ch * CHUNK, (ch + 1) * CHUNK)
        for g in range(C_GROUPS):
            cols = lambda blk: slice(blk * WIDTH + g * LANES, blk * WIDTH + (g + 1) * LANES)
            mixed = _dot(ws_ref[g], pz_ref[rows, cols(PZ_VN)]) + bs_ref[:, g * LANES:(g + 1) * LANES]
            zc = pz_ref[rows, cols(PZ_UC)].astype(F32) * mixed * pz_ref[rows, cols(PZ_GC)].astype(F32)
            z_ref[rows, 2 * WIDTH + g * LANES: 2 * WIDTH + (g + 1) * LANES] = zc.astype(BF16)


def _mixer_context_kernel(pz_ref, kta_ref, ktb_ref, sink_ref, ws_ref, bs_ref, z_ref):
    t = pz_ref.shape[0]
    rows = slice(0, t)
    low_head = lax.broadcasted_iota(jnp.int32, (t, LANES), 1) < HEAD_DIM
    for branch, (pq, kt_ref, pv, pg) in enumerate(((PZ_QA, kta_ref, PZ_VA, PZ_GA),
                                                   (PZ_QB, ktb_ref, PZ_VB, PZ_GB))):
        for head in range(N_KV):
            k_t = _dup_rows(kt_ref[0, head * HEAD_DIM:(head + 1) * HEAD_DIM, :])
            v_lo, v_hi = _v_views(pz_ref, rows, head, pv)
            for pair in range(2 * head, 2 * head + 2):
                cols = lambda blk: slice(blk * WIDTH + pair * LANES, blk * WIDTH + (pair + 1) * LANES)
                sinks = (sink_ref[2 * pair], sink_ref[2 * pair + 1]) if branch == 1 else None
                o = _pair_attention(pz_ref[rows, cols(pq)], [(k_t, v_lo, v_hi, None)], sinks, low_head)
                z_ref[rows, cols(branch)] = (o * pz_ref[rows, cols(pg)].astype(F32)).astype(BF16)
    _spatial_gate(pz_ref, ws_ref, bs_ref, z_ref, t)


def _mixer_context(pz, kta, ktb, sink, ws, bs, *, layer, seq):
    m = pz.shape[0]
    kt_spec = pl.BlockSpec((1, KV_WIDTH, seq), lambda i: (i, 0, 0))
    return pl.pallas_call(
        _mixer_context_kernel,
        grid=(m // seq,),
        in_specs=[
            pl.BlockSpec((seq, PZ_WIDTH), lambda i: (i, 0)),
            kt_spec, kt_spec,
            pl.BlockSpec(memory_space=pltpu.SMEM),
            _layer_block((C_GROUPS, CHUNK, CHUNK), layer),
            pl.BlockSpec((CHUNK, WIDTH), lambda i: (0, 0)),
        ],
        out_specs=pl.BlockSpec((seq, 3 * WIDTH), lambda i: (i, 0)),
        out_shape=jax.ShapeDtypeStruct((m, 3 * WIDTH), BF16),
        compiler_params=pltpu.CompilerParams(
            dimension_semantics=("arbitrary",), vmem_limit_bytes=VMEM_LIMIT),
        name="mixer_context",
    )(pz, kta, ktb, sink, ws, bs)


def _mixer_latent_kernel(pzq_ref, va_ref, vb_ref, kta_ref, ktb_ref,
                         cak_ref, cav_ref, cbk_ref, cbv_ref, sink_ref, ws_ref, bs_ref,
                         z_ref, cak_s, cav_s, cbk_s, cbv_s, *, seq):
    tq = pzq_ref.shape[0]
    past = cak_ref.shape[3]
    band = tq + 2 * WINDOW
    qi = pl.program_id(1)

    @pl.when(qi == 0)
    def _():
        cak_s[...] = cak_ref[0, 0].astype(BF16)
        cbk_s[...] = cbk_ref[0, 0].astype(BF16)
        low = lax.broadcasted_iota(jnp.int32, (past, LANES), 1) < HEAD_DIM
        for src, dst in ((cav_ref, cav_s), (cbv_ref, cbv_s)):
            t = src[0, 0].T
            r = pltpu.roll(t, HEAD_DIM, 1)
            zero = jnp.zeros_like(t)
            dst[:, 0 * LANES:1 * LANES] = jnp.where(low, t, zero).astype(BF16)
            dst[:, 1 * LANES:2 * LANES] = jnp.where(low, zero, r).astype(BF16)
            dst[:, 2 * LANES:3 * LANES] = jnp.where(low, r, zero).astype(BF16)
            dst[:, 3 * LANES:4 * LANES] = jnp.where(low, zero, t).astype(BF16)

    low_head = lax.broadcasted_iota(jnp.int32, (tq, LANES), 1) < HEAD_DIM
    all_rows = slice(0, seq)
    ctx_rows = slice(0, past)

    start_blk = jnp.clip(qi * (tq // LANES) - WINDOW // LANES, 0, (seq - band) // LANES)
    start = pl.multiple_of(start_blk * LANES, LANES)
    win_rows = pl.ds(start, band)
    q_pos = qi * tq + lax.broadcasted_iota(jnp.int32, (tq, band), 0)
    k_pos = start + lax.broadcasted_iota(jnp.int32, (tq, band), 1)
    in_window = jnp.abs(q_pos - k_pos) <= WINDOW

    for head in range(N_KV):
        head_rows = slice(head * HEAD_DIM, (head + 1) * HEAD_DIM)
        a_sources = [
            (_dup_rows(kta_ref[0, head_rows, :]),) + _v_views(va_ref, all_rows, head) + (None,),
            (_dup_rows(cak_s[head_rows, :]),) + _v_views(cav_s, ctx_rows, head) + (None,),
        ]
        kb_win = jnp.concatenate(
            [ktb_ref[0, start_blk + j, head_rows, :] for j in range(band // LANES)], axis=1)
        b_sources = [
            (_dup_rows(kb_win),) + _v_views(vb_ref, win_rows, head) + (in_window,),
            (_dup_rows(cbk_s[head_rows, :]),) + _v_views(cbv_s, ctx_rows, head) + (None,),
        ]
        for pair in range(2 * head, 2 * head + 2):
            cols = lambda blk: slice(blk * WIDTH + pair * LANES, blk * WIDTH + (pair + 1) * LANES)
            o = _pair_attention(pzq_ref[:, cols(PZ_QA)], a_sources, None, low_head)
            z_ref[:, cols(0)] = (o * pzq_ref[:, cols(PZ_GA)].astype(F32)).astype(BF16)
            o = _pair_attention(pzq_ref[:, cols(PZ_QB)], b_sources,
                                (sink_ref[2 * pair], sink_ref[2 * pair + 1]), low_head)
            z_ref[:, cols(1)] = (o * pzq_ref[:, cols(PZ_GB)].astype(F32)).astype(BF16)

    _spatial_gate(pzq_ref, ws_ref, bs_ref, z_ref, tq)


def _mixer_latent(pz, kta, ktb, cak, cav, cbk, cbv, sink, ws, bs, *, layer, seq, tq):
    m = pz.shape[0]
    n_seq = m // seq
    nq = seq // tq
    past = cak.shape[3]
    v_spec = lambda blk: pl.BlockSpec((seq, WIDTH), lambda b, q: (b, blk))
    cache_spec = pl.BlockSpec((1, 1, KV_WIDTH, past), lambda b, q: (b, layer, 0, 0))
    return pl.pallas_call(
        functools.partial(_mixer_latent_kernel, seq=seq),
        grid=(n_seq, nq),
        in_specs=[
            pl.BlockSpec((tq, PZ_WIDTH), lambda b, q: (b * nq + q, 0)),
            v_spec(PZ_VA), v_spec(PZ_VB),
            pl.BlockSpec((1, KV_WIDTH, seq), lambda b, q: (b, 0, 0)),
            pl.BlockSpec((1, seq // LANES, KV_WIDTH, LANES), lambda b, q: (b, 0, 0, 0)),
            cache_spec, cache_spec, cache_spec, cache_spec,
            pl.BlockSpec(memory_space=pltpu.SMEM),
            _layer_block((C_GROUPS, CHUNK, CHUNK), layer),
            pl.BlockSpec((CHUNK, WIDTH), lambda b, q: (0, 0)),
        ],
        out_specs=pl.BlockSpec((tq, 3 * WIDTH), lambda b, q: (b * nq + q, 0)),
        out_shape=jax.ShapeDtypeStruct((m, 3 * WIDTH), BF16),
        scratch_shapes=[pltpu.VMEM((KV_WIDTH, past), BF16), pltpu.VMEM((past, 4 * LANES), BF16),
                        pltpu.VMEM((KV_WIDTH, past), BF16), pltpu.VMEM((past, 4 * LANES), BF16)],
        compiler_params=pltpu.CompilerParams(
            dimension_semantics=("arbitrary", "arbitrary"), vmem_limit_bytes=VMEM_LIMIT),
        name="mixer_latent",
    )(pz, pz, pz, kta, ktb, cak, cav, cbk, cbv, sink, ws, bs)


def _back_kernel(x_ref, z_ref, mod_ref, wg_ref, bg_ref, wpa_ref, wpb_ref, wpc_ref, wo_ref,
                 lg_ref, lb_ref, o_ref):
    x = x_ref[...]
    mod = mod_ref[0]
    h = _modulated_norm(x, mod).astype(BF16)
    mix = None
    for br, wp_ref in enumerate((wpa_ref, wpb_ref, wpc_ref)):
        cols = slice(br * D_MODEL, (br + 1) * D_MODEL)
        g = jax.nn.sigmoid(_dot(h, wg_ref[:, cols]) + bg_ref[:, cols])
        y = _dot(z_ref[:, br * WIDTH:(br + 1) * WIDTH], wp_ref[...])
        mix = g * y if mix is None else mix + g * y
    y = _dot(mix.astype(BF16), wo_ref[...])
    r = DEEPNORM_ALPHA * x + mod[:, 2 * D_MODEL:] * y
    mu = jnp.mean(r, axis=-1, keepdims=True)
    rc = r - mu
    var = jnp.mean(rc * rc, axis=-1, keepdims=True)
    o_ref[...] = rc * lax.rsqrt(var + EPS) * lg_ref[...] + lb_ref[...]


def _back(x, z, mod, wg, bg, wpa, wpb, wpc, wo, lg, lb, *, layer, tm, rows_per_mod, name):
    m = x.shape[0]
    tiles_per_mod = rows_per_mod // tm
    return pl.pallas_call(
        _back_kernel,
        grid=(m // tm,),
        in_specs=[
            pl.BlockSpec((tm, D_MODEL), lambda i: (i, 0)),
            pl.BlockSpec((tm, 3 * WIDTH), lambda i: (i, 0)),
            pl.BlockSpec((1, 1, 3 * D_MODEL), lambda i: (i // tiles_per_mod, 0, 0)),
            _layer_block((D_MODEL, 3 * D_MODEL), layer),
            _layer_block((1, 3 * D_MODEL), layer),
            _layer_block((WIDTH, D_MODEL), layer),
            _layer_block((WIDTH, D_MODEL), layer),
            _layer_block((WIDTH, D_MODEL), layer),
            _layer_block((D_MODEL, D_MODEL), layer),
            _layer_block((1, D_MODEL), layer),
            _layer_block((1, D_MODEL), layer),
        ],
        out_specs=pl.BlockSpec((tm, D_MODEL), lambda i: (i, 0)),
        out_shape=jax.ShapeDtypeStruct((m, D_MODEL), F32),
        compiler_params=pltpu.CompilerParams(
            dimension_semantics=("arbitrary",), vmem_limit_bytes=VMEM_LIMIT),
        name=name,
    )(x, z, mod, wg, bg, wpa, wpb, wpc, wo, lg, lb)


def _rope_tables(n_tokens):
    rows = n_tokens // GRID_W
    row = jnp.repeat(jnp.arange(rows, dtype=F32), GRID_W)
    col = jnp.tile(jnp.arange(GRID_W, dtype=F32), rows)
    inv_freq = jnp.power(ROPE_BASE, -jnp.arange(ROPE_FREQS, dtype=F32) / ROPE_FREQS)
    ang = jnp.concatenate([row[:, None] * inv_freq, col[:, None] * inv_freq], axis=-1)
    cos, sin = jnp.cos(ang), jnp.sin(ang)
    return (jnp.tile(cos, (1, 4)), jnp.concatenate([-sin, sin, -sin, sin], axis=-1), cos.T, sin.T)


def _transposed_cache(t):
    b, depth, past = t.shape[:3]
    return jnp.transpose(t, (0, 1, 3, 4, 2)).reshape(b, depth, KV_WIDTH, past)


def kernel(x_prompt, x_sample, cache_a_k, cache_a_v, cache_b_k, cache_b_v, c, c_ctx, w_mod, b_mod,
           w_in, q_norm, k_norm, sink, sgu_g, sgu_b, w_spatial, b_spatial, w_proj_a, w_proj_b,
           w_proj_c, w_gate, b_gate, w_out, ln_g, ln_b):
    batch, seq, _ = x_prompt.shape
    dec_batch, dec_seq, _ = x_sample.shape
    tm = 512

    cvecs = jnp.concatenate(
        [c_ctx[None], c, jnp.zeros((MOD_ROWS - 1 - dec_batch, D_MODEL), F32)], axis=0)
    mod = _modulation(cvecs, w_mod, b_mod)

    rope_tables = _rope_tables(dec_seq)
    block_avg = jnp.kron(jnp.eye(N_HEADS, dtype=F32),
                         jnp.full((HEAD_DIM, HEAD_DIM), 1.0 / HEAD_DIM, F32)).astype(BF16)
    caches = [_transposed_cache(t) for t in (cache_a_k, cache_a_v, cache_b_k, cache_b_v)]

    w_in_bf = w_in.astype(BF16)
    wk_bf = jnp.swapaxes(jnp.concatenate(
        [w_in[:, :, _KA:_KA + KV_WIDTH], w_in[:, :, _KB:_KB + KV_WIDTH]], axis=-1), 1, 2).astype(BF16)
    wg_bf, wo_bf, ws_bf = w_gate.astype(BF16), w_out.astype(BF16), w_spatial.astype(BF16)
    wpa_bf, wpb_bf, wpc_bf = w_proj_a.astype(BF16), w_proj_b.astype(BF16), w_proj_c.astype(BF16)
    bg3, lg3, lb3 = b_gate[:, None], ln_g[:, None], ln_b[:, None]

    xp = x_prompt.reshape(batch * seq, D_MODEL)
    xs = x_sample.reshape(dec_batch * dec_seq, D_MODEL)
    states = []
    for l in range(DEPTH):
        bs = jnp.repeat(b_spatial[l].T, LANES, axis=1)
        qn = jnp.tile(q_norm[l], N_HEADS)[None]
        kn = jnp.broadcast_to(jnp.tile(k_norm[l], N_KV)[:, None], (KV_WIDTH, tm))
        sg, sb = sgu_g[l][None], sgu_b[l][None]
        mod_ctx = mod[l, 0:1][None]
        mod_lat = mod[l, 1:1 + dec_batch][:, None]

        pz, kta, ktb, *st = _front(xp, mod_ctx, w_in_bf, wk_bf, block_avg, qn, kn, sg, sb, None,
                                   layer=l, tm=tm, seq=seq, with_state=True)
        states.append(st)
        z = _mixer_context(pz, kta, ktb, sink[l], ws_bf, bs, layer=l, seq=seq)
        xp = _back(xp, z, mod_ctx, wg_bf, bg3, wpa_bf, wpb_bf, wpc_bf, wo_bf, lg3, lb3,
                   layer=l, tm=tm, rows_per_mod=batch * seq, name="back_context")

        pz, kta, ktb = _front(xs, mod_lat, w_in_bf, wk_bf, block_avg, qn, kn, sg, sb, rope_tables,
                              layer=l, tm=tm, seq=dec_seq, with_state=False)
        z = _mixer_latent(pz, kta, ktb, *caches, sink[l], ws_bf, bs, layer=l, seq=dec_seq, tq=256)
        xs = _back(xs, z, mod_lat, wg_bf, bg3, wpa_bf, wpb_bf, wpc_bf, wo_bf, lg3, lb3,
                   layer=l, tm=tm, rows_per_mod=dec_seq, name="back_latent")

    def state(kind):
        t = jnp.stack([st[kind] for st in states], axis=1)
        return jnp.transpose(t.reshape(batch, DEPTH, N_KV, HEAD_DIM, seq), (0, 1, 4, 2, 3))

    return (xp.reshape(batch, seq, D_MODEL), xs.reshape(dec_batch, dec_seq, D_MODEL),
            state(0), state(1), state(2), state(3))
```

```python
import functools

import jax
import jax.numpy as jnp
from jax import lax
from jax.experimental import pallas as pl
from jax.experimental.pallas import tpu as pltpu

D_MODEL = 1024
DEPTH = 4
GRID_W = 64
HEAD_DIM = 64
HALF = HEAD_DIM // 2
N_HEADS = 8
N_KV = 2
KV_WIDTH = N_KV * HEAD_DIM
WIDTH = 512
C_GROUPS = 4
CHUNK = 128
WINDOW = 128
IN_WIDTH = 4096
ROPE_BASE = 10000.0
ROPE_FREQS = HEAD_DIM // 4
EPS = 1e-6
NEG_INF = -1e30
DEEPNORM_ALPHA = (2 * DEPTH) ** 0.25
LOG2E = 1.4426950408889634
Q_SCALE = HEAD_DIM ** -0.5 * LOG2E

LANES = 128
MOD_ROWS = 8

_QA, _KA, _VA, _GA = 0, 512, 640, 768
_QB, _KB, _VB, _GB = 1280, 1792, 1920, 2048
_UC, _VC, _GC = 2560, 3072, 3584

(PZ_QA, PZ_VA, PZ_GA, PZ_QB, PZ_VB, PZ_GB, PZ_UC, PZ_VN, PZ_GC) = range(9)
PZ_WIDTH = 9 * WIDTH

BF16 = jnp.bfloat16
F32 = jnp.float32

VMEM_LIMIT = 56 * 1024 * 1024


def _silu(t):
    return t * jax.nn.sigmoid(t)


def _dot(a, b):
    return jnp.dot(a, b, preferred_element_type=F32)


def _dot_nt(a, b):
    return lax.dot_general(a, b, (((1,), (1,)), ((), ())), preferred_element_type=F32)


def _layer_block(shape, layer):
    zeros = (0,) * len(shape)
    return pl.BlockSpec((None,) + tuple(shape), lambda *_: (layer,) + zeros,
                        pipeline_mode=pl.Buffered(1))


def _mod_kernel(c_ref, w_ref, b_ref, o_ref):
    s = _silu(c_ref[...]).astype(BF16)
    o_ref[0] = _dot(s, w_ref[0].astype(BF16)) + b_ref[0]


def _modulation(cvecs, w_mod, b_mod):
    tn = 1024
    return pl.pallas_call(
        _mod_kernel,
        grid=(DEPTH, 3 * D_MODEL // tn),
        in_specs=[
            pl.BlockSpec((MOD_ROWS, D_MODEL), lambda l, j: (0, 0)),
            pl.BlockSpec((1, D_MODEL, tn), lambda l, j: (l, 0, j)),
            pl.BlockSpec((1, 1, tn), lambda l, j: (l, 0, j)),
        ],
        out_specs=pl.BlockSpec((1, MOD_ROWS, tn), lambda l, j: (l, 0, j)),
        out_shape=jax.ShapeDtypeStruct((DEPTH, MOD_ROWS, 3 * D_MODEL), F32),
        compiler_params=pltpu.CompilerParams(
            dimension_semantics=("arbitrary", "arbitrary"), vmem_limit_bytes=VMEM_LIMIT),
        name="modulation",
    )(cvecs, w_mod, b_mod.reshape(DEPTH, 1, 3 * D_MODEL))


def _modulated_norm(x, mod):
    mu = jnp.mean(x, axis=-1, keepdims=True)
    xc = x - mu
    var = jnp.mean(xc * xc, axis=-1, keepdims=True)
    xn = xc * lax.rsqrt(var + EPS)
    return xn * (1.0 + mod[:, D_MODEL:2 * D_MODEL]) + mod[:, :D_MODEL]


def _write_token_blocks(ref, t):
    if len(ref.shape) == 4:
        for j in range(ref.shape[1]):
            ref[0, j] = t[:, j * LANES:(j + 1) * LANES].astype(ref.dtype)
    else:
        w = ref.shape[2]
        for j in range(ref.shape[0]):
            ref[j] = t[:, j * w:(j + 1) * w].astype(ref.dtype)


def _front_kernel(*refs, rope, with_state):
    refs = list(refs)
    x_ref, mod_ref, w_ref, wk_ref, bd_ref, qn_ref, kn_ref, sg_ref, sb_ref = refs[:9]
    refs = refs[9:]
    if rope:
        cos_ref, sin_ref, cos_t_ref, sin_t_ref = refs[:4]
        refs = refs[4:]
    pz_ref, kta_ref, ktb_ref = refs[:3]
    st_refs = refs[3:] if with_state else None

    tm = x_ref.shape[0]
    h = _modulated_norm(x_ref[...], mod_ref[0]).astype(BF16)

    lane = lax.broadcasted_iota(jnp.int32, (tm, LANES), 1)
    low_head = lane < HEAD_DIM
    first_half = (lane & HALF) == 0

    def proj(lo, width):
        return _dot(h, w_ref[:, lo:lo + width])

    def rotary(t):
        if not rope:
            return t
        swapped = jnp.where(first_half, pltpu.roll(t, LANES - HALF, 1), pltpu.roll(t, HALF, 1))
        return t * cos_ref[...] + swapped * sin_ref[...]

    def rotary_t(t):
        if not rope:
            return t
        x1, x2 = t[:HALF], t[HALF:]
        c, s = cos_t_ref[...], sin_t_ref[...]
        return jnp.concatenate([x1 * c - x2 * s, x2 * c + x1 * s], axis=0)

    def store(block, off, val):
        pz_ref[:, block * WIDTH + off: block * WIDTH + off + val.shape[1]] = val.astype(BF16)

    def store_v(block, t):
        r = pltpu.roll(t, HEAD_DIM, 1)
        zero = jnp.zeros_like(t)
        store(block, 0 * LANES, jnp.where(low_head, t, zero))
        store(block, 1 * LANES, jnp.where(low_head, zero, r))
        store(block, 2 * LANES, jnp.where(low_head, r, zero))
        store(block, 3 * LANES, jnp.where(low_head, zero, t))

    def q_path(lo, block, gain):
        q = proj(lo, WIDTH)
        if gain is not None:
            ms = _dot((q * q).astype(BF16), bd_ref[...])
            q = q * lax.rsqrt(ms + EPS) * gain
        q = q * Q_SCALE
        for c in range(WIDTH // LANES):
            store(block, c * LANES, rotary(q[:, c * LANES:(c + 1) * LANES]))

    k_t = _dot_nt(wk_ref[...], h)
    ka_heads, kb_heads = [], []
    for g in range(N_KV):
        t = k_t[g * HEAD_DIM:(g + 1) * HEAD_DIM]
        ms = jnp.mean(t * t, axis=0, keepdims=True)
        ka_heads.append(t * lax.rsqrt(ms + EPS) * kn_ref[g * HEAD_DIM:(g + 1) * HEAD_DIM])
        kb_heads.append(k_t[KV_WIDTH + g * HEAD_DIM: KV_WIDTH + (g + 1) * HEAD_DIM])
    _write_token_blocks(kta_ref, jnp.concatenate([rotary_t(t) for t in ka_heads], axis=0))
    _write_token_blocks(ktb_ref, jnp.concatenate([rotary_t(t) for t in kb_heads], axis=0))

    q_path(_QA, PZ_QA, qn_ref[...])
    va = proj(_VA, KV_WIDTH)
    store_v(PZ_VA, va)
    store(PZ_GA, 0, _silu(proj(_GA, WIDTH)))

    q_path(_QB, PZ_QB, None)
    vb = proj(_VB, KV_WIDTH)
    store_v(PZ_VB, vb)
    store(PZ_GB, 0, _silu(proj(_GB, WIDTH)))

    if with_state:
        _write_token_blocks(st_refs[0], jnp.concatenate(ka_heads, axis=0))
        _write_token_blocks(st_refs[1], va.T)
        _write_token_blocks(st_refs[2], jnp.concatenate(kb_heads, axis=0))
        _write_token_blocks(st_refs[3], vb.T)

    store(PZ_UC, 0, proj(_UC, WIDTH))
    vc = proj(_VC, WIDTH)
    mu = jnp.mean(vc, axis=-1, keepdims=True)
    vcc = vc - mu
    var = jnp.mean(vcc * vcc, axis=-1, keepdims=True)
    store(PZ_VN, 0, vcc * lax.rsqrt(var + EPS) * sg_ref[...] + sb_ref[...])
    store(PZ_GC, 0, _silu(proj(_GC, WIDTH)))


def _front(x, mod, w_in, wk, bd, qn, kn, sg, sb, rope_tables, *, layer, tm, seq, with_state):
    m = x.shape[0]
    n_seq = m // seq
    rope = rope_tables is not None
    const = lambda i: (0, 0)
    if tm >= seq:
        nb = tm // seq
        mod_idx = lambda i: (0, 0, 0)
        kt_shape, kt_spec = (n_seq, KV_WIDTH, seq), pl.BlockSpec((nb, KV_WIDTH, seq), lambda i: (i, 0, 0))
        ktb_shape, ktb_spec = kt_shape, kt_spec
    else:
        tps = seq // tm
        mod_idx = lambda i: (i // tps, 0, 0)
        kt_shape = (n_seq, KV_WIDTH, seq)
        kt_spec = pl.BlockSpec((1, KV_WIDTH, tm), lambda i: (i // tps, 0, i % tps))
        ktb_shape = (n_seq, seq // LANES, KV_WIDTH, LANES)
        ktb_spec = pl.BlockSpec((1, tm // LANES, KV_WIDTH, LANES), lambda i: (i // tps, i % tps, 0, 0))
    in_specs = [
        pl.BlockSpec((tm, D_MODEL), lambda i: (i, 0)),
        pl.BlockSpec((1, 1, 3 * D_MODEL), mod_idx),
        _layer_block((D_MODEL, IN_WIDTH), layer),
        _layer_block((2 * KV_WIDTH, D_MODEL), layer),
        pl.BlockSpec((WIDTH, WIDTH), const),
        pl.BlockSpec((1, WIDTH), const),
        pl.BlockSpec((KV_WIDTH, tm), const),
        pl.BlockSpec((1, WIDTH), const),
        pl.BlockSpec((1, WIDTH), const),
    ]
    args = [x, mod, w_in, wk, bd, qn, kn, sg, sb]
    if rope:
        cos, sin, cos_t, sin_t = rope_tables
        in_specs += [pl.BlockSpec((tm, LANES), lambda i: (i % tps, 0))] * 2
        in_specs += [pl.BlockSpec((HALF, tm), lambda i: (0, i % tps))] * 2
        args += [cos, sin, cos_t, sin_t]
    out_specs = [pl.BlockSpec((tm, PZ_WIDTH), lambda i: (i, 0)), kt_spec, ktb_spec]
    out_shape = [jax.ShapeDtypeStruct((m, PZ_WIDTH), BF16),
                 jax.ShapeDtypeStruct(kt_shape, BF16), jax.ShapeDtypeStruct(ktb_shape, BF16)]
    if with_state:
        out_specs += [kt_spec] * 4
        out_shape += [jax.ShapeDtypeStruct(kt_shape, F32)] * 4
    return pl.pallas_call(
        functools.partial(_front_kernel, rope=rope, with_state=with_state),
        grid=(m // tm,),
        in_specs=in_specs,
        out_specs=out_specs,
        out_shape=out_shape,
        compiler_params=pltpu.CompilerParams(
            dimension_semantics=("arbitrary",), vmem_limit_bytes=VMEM_LIMIT),
        name="front_latent" if rope else "front_context",
    )(*args)


def _v_views(ref, rows, head, block=0):
    base = block * WIDTH + 2 * head * LANES
    return ref[rows, base:base + LANES], ref[rows, base + LANES:base + 2 * LANES]


def _dup_rows(kt):
    return jnp.concatenate([kt, kt], axis=0)


def _pair_attention(q_pair, sources, sinks, low_head):
    zero = jnp.zeros_like(q_pair)
    q_heads = (jnp.where(low_head, q_pair, zero), jnp.where(low_head, zero, q_pair))
    out = None
    inv = []
    for hh in range(2):
        scores = []
        for k_t, _, _, mask in sources:
            s = _dot(q_heads[hh], k_t)
            if mask is not None:
                s = jnp.where(mask, s, NEG_INF)
            scores.append(s)
        m = None
        for s in scores:
            ms = jnp.max(s, axis=-1, keepdims=True)
            m = ms if m is None else jnp.maximum(m, ms)
        if sinks is not None:
            sink = sinks[hh] * LOG2E
            m = jnp.maximum(m, sink)
            denom = jnp.exp2(sink - m)
        else:
            denom = jnp.zeros_like(m)
        for s, src in zip(scores, sources):
            p = jnp.exp2(s - m)
            denom = denom + jnp.sum(p, axis=-1, keepdims=True)
            pv = _dot(p.astype(BF16), src[1 + hh])
            out = pv if out is None else out + pv
        inv.append(1.0 / denom)
    return out * jnp.where(low_head, inv[0], inv[1])


def _spatial_gate(pz_ref, ws_ref, bs_ref, z_ref, n_rows):
    for ch in range(n_rows // CHUNK):
        rows = slice(ch * CHUNK, (ch + 1) * CHUNK)
        for g in range(C_GROUPS):
            cols = lambda blk: slice(blk * WIDTH + g * LANES, blk * WIDTH + (g + 1) * LANES)
            mixed = _dot(ws_ref[g], pz_ref[rows, cols(PZ_VN)]) + bs_ref[:, g * LANES:(g + 1) * LANES]
            zc = pz_ref[rows, cols(PZ_UC)].astype(F32) * mixed * pz_ref[rows, cols(PZ_GC)].astype(F32)
            z_ref[rows, 2 * WIDTH + g * LANES: 2 * WIDTH + (g + 1) * LANES] = zc.astype(BF16)


def _mixer_context_kernel(pz_ref, kta_ref, ktb_ref, sink_ref, ws_ref, bs_ref, z_ref):
    t = pz_ref.shape[0]
    rows = slice(0, t)
    low_head = lax.broadcasted_iota(jnp.int32, (t, LANES), 1) < HEAD_DIM
    for branch, (pq, kt_ref, pv, pg) in enumerate(((PZ_QA, kta_ref, PZ_VA, PZ_GA),
                                                   (PZ_QB, ktb_ref, PZ_VB, PZ_GB))):
        for head in range(N_KV):
            k_t = _dup_rows(kt_ref[0, head * HEAD_DIM:(head + 1) * HEAD_DIM, :])
            v_lo, v_hi = _v_views(pz_ref, rows, head, pv)
            for pair in range(2 * head, 2 * head + 2):
                cols = lambda blk: slice(blk * WIDTH + pair * LANES, blk * WIDTH + (pair + 1) * LANES)
                sinks = (sink_ref[2 * pair], sink_ref[2 * pair + 1]) if branch == 1 else None
                o = _pair_attention(pz_ref[rows, cols(pq)], [(k_t, v_lo, v_hi, None)], sinks, low_head)
                z_ref[rows, cols(branch)] = (o * pz_ref[rows, cols(pg)].astype(F32)).astype(BF16)
    _spatial_gate(pz_ref, ws_ref, bs_ref, z_ref, t)


def _mixer_context(pz, kta, ktb, sink, ws, bs, *, layer, seq):
    m = pz.shape[0]
    kt_spec = pl.BlockSpec((1, KV_WIDTH, seq), lambda i: (i, 0, 0))
    return pl.pallas_call(
        _mixer_context_kernel,
        grid=(m // seq,),
        in_specs=[
            pl.BlockSpec((seq, PZ_WIDTH), lambda i: (i, 0)),
            kt_spec, kt_spec,
            pl.BlockSpec(memory_space=pltpu.SMEM),
            _layer_block((C_GROUPS, CHUNK, CHUNK), layer),
            pl.BlockSpec((CHUNK, WIDTH), lambda i: (0, 0)),
        ],
        out_specs=pl.BlockSpec((seq, 3 * WIDTH), lambda i: (i, 0)),
        out_shape=jax.ShapeDtypeStruct((m, 3 * WIDTH), BF16),
        compiler_params=pltpu.CompilerParams(
            dimension_semantics=("arbitrary",), vmem_limit_bytes=VMEM_LIMIT),
        name="mixer_context",
    )(pz, kta, ktb, sink, ws, bs)


def _mixer_latent_kernel(pzq_ref, va_ref, vb_ref, kta_ref, ktb_ref,
                         cak_ref, cav_ref, cbk_ref, cbv_ref, sink_ref, ws_ref, bs_ref,
                         z_ref, cak_s, cav_s, cbk_s, cbv_s, *, seq):
    tq = pzq_ref.shape[0]
    past = cak_ref.shape[3]
    band = tq + 2 * WINDOW
    qi = pl.program_id(1)

    @pl.when(qi == 0)
    def _():
        cak_s[...] = cak_ref[0, 0].astype(BF16)
        cbk_s[...] = cbk_ref[0, 0].astype(BF16)
        low = lax.broadcasted_iota(jnp.int32, (past, LANES), 1) < HEAD_DIM
        for src, dst in ((cav_ref, cav_s), (cbv_ref, cbv_s)):
            t = src[0, 0].T
            r = pltpu.roll(t, HEAD_DIM, 1)
            zero = jnp.zeros_like(t)
            dst[:, 0 * LANES:1 * LANES] = jnp.where(low, t, zero).astype(BF16)
            dst[:, 1 * LANES:2 * LANES] = jnp.where(low, zero, r).astype(BF16)
            dst[:, 2 * LANES:3 * LANES] = jnp.where(low, r, zero).astype(BF16)
            dst[:, 3 * LANES:4 * LANES] = jnp.where(low, zero, t).astype(BF16)

    low_head = lax.broadcasted_iota(jnp.int32, (tq, LANES), 1) < HEAD_DIM
    all_rows = slice(0, seq)
    ctx_rows = slice(0, past)

    start_blk = jnp.clip(qi * (tq // LANES) - WINDOW // LANES, 0, (seq - band) // LANES)
    start = pl.multiple_of(start_blk * LANES, LANES)
    win_rows = pl.ds(start, band)
    q_pos = qi * tq + lax.broadcasted_iota(jnp.int32, (tq, band), 0)
    k_pos = start + lax.broadcasted_iota(jnp.int32, (tq, band), 1)
    in_window = jnp.abs(q_pos - k_pos) <= WINDOW

    for head in range(N_KV):
        head_rows = slice(head * HEAD_DIM, (head + 1) * HEAD_DIM)
        a_sources = [
            (_dup_rows(kta_ref[0, head_rows, :]),) + _v_views(va_ref, all_rows, head) + (None,),
            (_dup_rows(cak_s[head_rows, :]),) + _v_views(cav_s, ctx_rows, head) + (None,),
        ]
        kb_win = jnp.concatenate(
            [ktb_ref[0, start_blk + j, head_rows, :] for j in range(band // LANES)], axis=1)
        b_sources = [
            (_dup_rows(kb_win),) + _v_views(vb_ref, win_rows, head) + (in_window,),
            (_dup_rows(cbk_s[head_rows, :]),) + _v_views(cbv_s, ctx_rows, head) + (None,),
        ]
        for pair in range(2 * head, 2 * head + 2):
            cols = lambda blk: slice(blk * WIDTH + pair * LANES, blk * WIDTH + (pair + 1) * LANES)
            o = _pair_attention(pzq_ref[:, cols(PZ_QA)], a_sources, None, low_head)
            z_ref[:, cols(0)] = (o * pzq_ref[:, cols(PZ_GA)].astype(F32)).astype(BF16)
            o = _pair_attention(pzq_ref[:, cols(PZ_QB)], b_sources,
                                (sink_ref[2 * pair], sink_ref[2 * pair + 1]), low_head)
            z_ref[:, cols(1)] = (o * pzq_ref[:, cols(PZ_GB)].astype(F32)).astype(BF16)

    _spatial_gate(pzq_ref, ws_ref, bs_ref, z_ref, tq)


def _mixer_latent(pz, kta, ktb, cak, cav, cbk, cbv, sink, ws, bs, *, layer, seq, tq):
    m = pz.shape[0]
    n_seq = m // seq
    nq = seq // tq
    past = cak.shape[3]
    v_spec = lambda blk: pl.BlockSpec((seq, WIDTH), lambda b, q: (b, blk))
    cache_spec = pl.BlockSpec((1, 1, KV_WIDTH, past), lambda b, q: (b, layer, 0, 0))
    return pl.pallas_call(
        functools.partial(_mixer_latent_kernel, seq=seq),
        grid=(n_seq, nq),
        in_specs=[
            pl.BlockSpec((tq, PZ_WIDTH), lambda b, q: (b * nq + q, 0)),
            v_spec(PZ_VA), v_spec(PZ_VB),
            pl.BlockSpec((1, KV_WIDTH, seq), lambda b, q: (b, 0, 0)),
            pl.BlockSpec((1, seq // LANES, KV_WIDTH, LANES), lambda b, q: (b, 0, 0, 0)),
            cache_spec, cache_spec, cache_spec, cache_spec,
            pl.BlockSpec(memory_space=pltpu.SMEM),
            _layer_block((C_GROUPS, CHUNK, CHUNK), layer),
            pl.BlockSpec((CHUNK, WIDTH), lambda b, q: (0, 0)),
        ],
        out_specs=pl.BlockSpec((tq, 3 * WIDTH), lambda b, q: (b * nq + q, 0)),
        out_shape=jax.ShapeDtypeStruct((m, 3 * WIDTH), BF16),
        scratch_shapes=[pltpu.VMEM((KV_WIDTH, past), BF16), pltpu.VMEM((past, 4 * LANES), BF16),
                        pltpu.VMEM((KV_WIDTH, past), BF16), pltpu.VMEM((past, 4 * LANES), BF16)],
        compiler_params=pltpu.CompilerParams(
            dimension_semantics=("arbitrary", "arbitrary"), vmem_limit_bytes=VMEM_LIMIT),
        name="mixer_latent",
    )(pz, pz, pz, kta, ktb, cak, cav, cbk, cbv, sink, ws, bs)


def _back_kernel(x_ref, z_ref, mod_ref, wg_ref, bg_ref, wpa_ref, wpb_ref, wpc_ref, wo_ref,
                 lg_ref, lb_ref, o_ref, *, n_sub):
    mod = mod_ref[0]
    sub = x_ref.shape[0] // n_sub
    for s in range(n_sub):
        rows = slice(s * sub, (s + 1) * sub)
        x = x_ref[rows, :]
        h = _modulated_norm(x, mod).astype(BF16)
        mix = None
        for br, wp_ref in enumerate((wpa_ref, wpb_ref, wpc_ref)):
            cols = slice(br * D_MODEL, (br + 1) * D_MODEL)
            g = jax.nn.sigmoid(_dot(h, wg_ref[:, cols]) + bg_ref[:, cols])
            y = _dot(z_ref[rows, br * WIDTH:(br + 1) * WIDTH], wp_ref[...])
            mix = g * y if mix is None else mix + g * y
        y = _dot(mix.astype(BF16), wo_ref[...])
        r = DEEPNORM_ALPHA * x + mod[:, 2 * D_MODEL:] * y
        mu = jnp.mean(r, axis=-1, keepdims=True)
        rc = r - mu
        var = jnp.mean(rc * rc, axis=-1, keepdims=True)
        o_ref[rows, :] = rc * lax.rsqrt(var + EPS) * lg_ref[...] + lb_ref[...]


def _back(x, z, mod, wg, bg, wpa, wpb, wpc, wo, lg, lb, *, layer, tm, rows_per_mod, name):
    m = x.shape[0]
    tiles_per_mod = rows_per_mod // tm
    return pl.pallas_call(
        functools.partial(_back_kernel, n_sub=2),
        grid=(m // tm,),
        in_specs=[
            pl.BlockSpec((tm, D_MODEL), lambda i: (i, 0)),
            pl.BlockSpec((tm, 3 * WIDTH), lambda i: (i, 0)),
            pl.BlockSpec((1, 1, 3 * D_MODEL), lambda i: (i // tiles_per_mod, 0, 0)),
            _layer_block((D_MODEL, 3 * D_MODEL), layer),
            _layer_block((1, 3 * D_MODEL), layer),
            _layer_block((WIDTH, D_MODEL), layer),
            _layer_block((WIDTH, D_MODEL), layer),
            _layer_block((WIDTH, D_MODEL), layer),
            _layer_block((D_MODEL, D_MODEL), layer),
            _layer_block((1, D_MODEL), layer),
            _layer_block((1, D_MODEL), layer),
        ],
        out_specs=pl.BlockSpec((tm, D_MODEL), lambda i: (i, 0)),
        out_shape=jax.ShapeDtypeStruct((m, D_MODEL), F32),
        compiler_params=pltpu.CompilerParams(
            dimension_semantics=("arbitrary",), vmem_limit_bytes=VMEM_LIMIT),
        name=name,
    )(x, z, mod, wg, bg, wpa, wpb, wpc, wo, lg, lb)


def _rope_tables(n_tokens):
    rows = n_tokens // GRID_W
    row = jnp.repeat(jnp.arange(rows, dtype=F32), GRID_W)
    col = jnp.tile(jnp.arange(GRID_W, dtype=F32), rows)
    inv_freq = jnp.power(ROPE_BASE, -jnp.arange(ROPE_FREQS, dtype=F32) / ROPE_FREQS)
    ang = jnp.concatenate([row[:, None] * inv_freq, col[:, None] * inv_freq], axis=-1)
    cos, sin = jnp.cos(ang), jnp.sin(ang)
    return (jnp.tile(cos, (1, 4)), jnp.concatenate([-sin, sin, -sin, sin], axis=-1), cos.T, sin.T)


def _transposed_cache(t):
    b, depth, past = t.shape[:3]
    return jnp.transpose(t, (0, 1, 3, 4, 2)).reshape(b, depth, KV_WIDTH, past)


def kernel(x_prompt, x_sample, cache_a_k, cache_a_v, cache_b_k, cache_b_v, c, c_ctx, w_mod, b_mod,
           w_in, q_norm, k_norm, sink, sgu_g, sgu_b, w_spatial, b_spatial, w_proj_a, w_proj_b,
           w_proj_c, w_gate, b_gate, w_out, ln_g, ln_b):
    batch, seq, _ = x_prompt.shape
    dec_batch, dec_seq, _ = x_sample.shape
    tm = 512

    cvecs = jnp.concatenate(
        [c_ctx[None], c, jnp.zeros((MOD_ROWS - 1 - dec_batch, D_MODEL), F32)], axis=0)
    mod = _modulation(cvecs, w_mod, b_mod)

    rope_tables = _rope_tables(dec_seq)
    block_avg = jnp.kron(jnp.eye(N_HEADS, dtype=F32),
                         jnp.full((HEAD_DIM, HEAD_DIM), 1.0 / HEAD_DIM, F32)).astype(BF16)
    caches = [_transposed_cache(t) for t in (cache_a_k, cache_a_v, cache_b_k, cache_b_v)]

    w_in_bf = w_in.astype(BF16)
    wk_bf = jnp.swapaxes(lax.optimization_barrier(jnp.concatenate(
        [w_in[:, :, _KA:_KA + KV_WIDTH], w_in[:, :, _KB:_KB + KV_WIDTH]], axis=-1)), 1, 2).astype(BF16)
    wg_bf, wo_bf, ws_bf = w_gate.astype(BF16), w_out.astype(BF16), w_spatial.astype(BF16)
    wpa_bf, wpb_bf, wpc_bf = w_proj_a.astype(BF16), w_proj_b.astype(BF16), w_proj_c.astype(BF16)
    bg3, lg3, lb3 = b_gate[:, None], ln_g[:, None], ln_b[:, None]

    xp = x_prompt.reshape(batch * seq, D_MODEL)
    xs = x_sample.reshape(dec_batch * dec_seq, D_MODEL)
    states = []
    for l in range(DEPTH):
        bs = jnp.repeat(b_spatial[l].T, LANES, axis=1)
        qn = jnp.tile(q_norm[l], N_HEADS)[None]
        kn = jnp.broadcast_to(jnp.tile(k_norm[l], N_KV)[:, None], (KV_WIDTH, tm))
        sg, sb = sgu_g[l][None], sgu_b[l][None]
        mod_ctx = mod[l, 0:1][None]
        mod_lat = mod[l, 1:1 + dec_batch][:, None]

        pz, kta, ktb, *st = _front(xp, mod_ctx, w_in_bf, wk_bf, block_avg, qn, kn, sg, sb, None,
                                   layer=l, tm=tm, seq=seq, with_state=True)
        states.append(st)
        z = _mixer_context(pz, kta, ktb, sink[l], ws_bf, bs, layer=l, seq=seq)
        xp = _back(xp, z, mod_ctx, wg_bf, bg3, wpa_bf, wpb_bf, wpc_bf, wo_bf, lg3, lb3,
                   layer=l, tm=2 * tm, rows_per_mod=batch * seq, name="back_context")

        pz, kta, ktb = _front(xs, mod_lat, w_in_bf, wk_bf, block_avg, qn, kn, sg, sb, rope_tables,
                              layer=l, tm=tm, seq=dec_seq, with_state=False)
        z = _mixer_latent(pz, kta, ktb, *caches, sink[l], ws_bf, bs, layer=l, seq=dec_seq, tq=256)
        xs = _back(xs, z, mod_lat, wg_bf, bg3, wpa_bf, wpb_bf, wpc_bf, wo_bf, lg3, lb3,
                   layer=l, tm=2 * tm, rows_per_mod=dec_seq, name="back_latent")

    def state(kind):
        t = jnp.stack([st[kind] for st in states], axis=1)
        return jnp.transpose(t.reshape(batch, DEPTH, N_KV, HEAD_DIM, seq), (0, 1, 4, 2, 3))

    return (xp.reshape(batch, seq, D_MODEL), xs.reshape(dec_batch, dec_seq, D_MODEL),
            state(0), state(1), state(2), state(3))
```

```python
import functools

import jax
import jax.numpy as jnp
from jax import lax
from jax.experimental import pallas as pl
from jax.experimental.pallas import tpu as pltpu

D_MODEL = 1024
DEPTH = 4
GRID_W = 64
HEAD_DIM = 64
HALF = HEAD_DIM // 2
N_HEADS = 8
N_KV = 2
GROUP = N_HEADS // N_KV
KV_WIDTH = N_KV * HEAD_DIM
WIDTH = 512
C_GROUPS = 4
CHUNK = 128
WINDOW = 128
IN_WIDTH = 4096
ROPE_BASE = 10000.0
ROPE_FREQS = HEAD_DIM // 4
EPS = 1e-6
NEG_INF = -1e30
DEEPNORM_ALPHA = (2 * DEPTH) ** 0.25
LOG2E = 1.4426950408889634
Q_SCALE = HEAD_DIM ** -0.5 * LOG2E

LANES = 128
BF16_SUBLANES = 16
MOD_ROWS = 8

_QA, _KA, _VA, _GA = 0, 512, 640, 768
_QB, _KB, _VB, _GB = 1280, 1792, 1920, 2048
_UC, _VC, _GC = 2560, 3072, 3584
_T_SEGMENTS = ((_QA, WIDTH), (_KA, KV_WIDTH), (_VA, KV_WIDTH), (_QB, WIDTH), (_KB, KV_WIDTH), (_VB, KV_WIDTH))
T_QA, T_KA, T_VA, T_QB, T_KB, T_VB = 0, 512, 640, 768, 1280, 1408
T_ROWS = 1536

(PZ_GA, PZ_GB, PZ_UC, PZ_VN, PZ_GC) = range(5)
PZ_WIDTH = 5 * WIDTH

BF16 = jnp.bfloat16
F32 = jnp.float32

VMEM_LIMIT = 56 * 1024 * 1024


def _silu(t):
    return t * jax.nn.sigmoid(t)


def _dot(a, b):
    return jnp.dot(a, b, preferred_element_type=F32)


def _dot_nt(a, b):
    return lax.dot_general(a, b, (((1,), (1,)), ((), ())), preferred_element_type=F32)


def _layer_block(shape, layer):
    zeros = (0,) * len(shape)
    return pl.BlockSpec((None,) + tuple(shape), lambda *_: (layer,) + zeros,
                        pipeline_mode=pl.Buffered(1))


def _mod_kernel(c_ref, w_ref, b_ref, o_ref):
    s = _silu(c_ref[...]).astype(BF16)
    o_ref[0] = _dot(s, w_ref[0].astype(BF16)) + b_ref[0]


def _modulation(cvecs, w_mod, b_mod):
    tn = 1024
    return pl.pallas_call(
        _mod_kernel,
        grid=(DEPTH, 3 * D_MODEL // tn),
        in_specs=[
            pl.BlockSpec((MOD_ROWS, D_MODEL), lambda l, j: (0, 0)),
            pl.BlockSpec((1, D_MODEL, tn), lambda l, j: (l, 0, j)),
            pl.BlockSpec((1, 1, tn), lambda l, j: (l, 0, j)),
        ],
        out_specs=pl.BlockSpec((1, MOD_ROWS, tn), lambda l, j: (l, 0, j)),
        out_shape=jax.ShapeDtypeStruct((DEPTH, MOD_ROWS, 3 * D_MODEL), F32),
        compiler_params=pltpu.CompilerParams(
            dimension_semantics=("arbitrary", "arbitrary"), vmem_limit_bytes=VMEM_LIMIT),
        name="modulation",
    )(cvecs, w_mod, b_mod.reshape(DEPTH, 1, 3 * D_MODEL))


def _modulated_norm(x, mod):
    mu = jnp.mean(x, axis=-1, keepdims=True)
    xc = x - mu
    var = jnp.mean(xc * xc, axis=-1, keepdims=True)
    xn = xc * lax.rsqrt(var + EPS)
    return xn * (1.0 + mod[:, D_MODEL:2 * D_MODEL]) + mod[:, :D_MODEL]


def _write_token_blocks(ref, t):
    w = ref.shape[2]
    for j in range(ref.shape[0]):
        ref[j] = t[:, j * w:(j + 1) * w].astype(ref.dtype)


def _front_kernel(*refs, rope, with_state):
    refs = list(refs)
    x_ref, mod_ref, w_ref, wt_ref, qn_ref, kn_ref, sg_ref, sb_ref = refs[:8]
    refs = refs[8:]
    if rope:
        cos_ref, sin_ref = refs[:2]
        refs = refs[2:]
    qat_ref, qbt_ref, ka_ref, kb_ref, vat_ref, vbt_ref, pz_ref = refs[:7]
    st_refs = refs[7:] if with_state else None

    h = _modulated_norm(x_ref[...], mod_ref[0]).astype(BF16)

    def rotary_t(t):
        if not rope:
            return t
        x1, x2 = t[:HALF], t[HALF:]
        c, s = cos_ref[...], sin_ref[...]
        return jnp.concatenate([x1 * c - x2 * s, x2 * c + x1 * s], axis=0)

    def rms_t(t, gain):
        ms = jnp.mean(t * t, axis=0, keepdims=True)
        return t * lax.rsqrt(ms + EPS) * gain

    pt = _dot_nt(wt_ref[...], h)

    def heads_t(row0, n):
        return [pt[row0 + j * HEAD_DIM: row0 + (j + 1) * HEAD_DIM] for j in range(n)]

    qn, kn = qn_ref[...], kn_ref[...]
    qa = [rotary_t(rms_t(t, qn)) * Q_SCALE for t in heads_t(T_QA, N_HEADS)]
    qat_ref[...] = jnp.concatenate(qa, axis=0).astype(BF16)
    qb = [rotary_t(t) * Q_SCALE for t in heads_t(T_QB, N_HEADS)]
    qbt_ref[...] = jnp.concatenate(qb, axis=0).astype(BF16)

    ka = [rms_t(t, kn) for t in heads_t(T_KA, N_KV)]
    kb = heads_t(T_KB, N_KV)
    ka_ref[...] = jnp.concatenate([rotary_t(t) for t in ka], axis=0).T.astype(BF16)
    kb_ref[...] = jnp.concatenate([rotary_t(t) for t in kb], axis=0).T.astype(BF16)
    va = pt[T_VA:T_VA + KV_WIDTH]
    vb = pt[T_VB:T_VB + KV_WIDTH]
    vat_ref[...] = va.astype(BF16)
    if len(vbt_ref.shape) == 3:
        _write_token_blocks(vbt_ref, vb)
    else:
        vbt_ref[...] = vb.astype(BF16)

    if with_state:
        _write_token_blocks(st_refs[0], jnp.concatenate(ka, axis=0))
        _write_token_blocks(st_refs[1], va)
        _write_token_blocks(st_refs[2], jnp.concatenate(kb, axis=0))
        _write_token_blocks(st_refs[3], vb)

    def proj(lo):
        return _dot(h, w_ref[:, lo:lo + WIDTH])

    def store(block, val):
        pz_ref[:, block * WIDTH:(block + 1) * WIDTH] = val.astype(BF16)

    store(PZ_GA, _silu(proj(_GA)))
    store(PZ_GB, _silu(proj(_GB)))
    store(PZ_UC, proj(_UC))
    vc = proj(_VC)
    mu = jnp.mean(vc, axis=-1, keepdims=True)
    vcc = vc - mu
    var = jnp.mean(vcc * vcc, axis=-1, keepdims=True)
    store(PZ_VN, vcc * lax.rsqrt(var + EPS) * sg_ref[...] + sb_ref[...])
    store(PZ_GC, _silu(proj(_GC)))


def _front(x, mod, w_in, wt, qn, kn, sg, sb, rope_tables, *, layer, tm, seq, with_state):
    m = x.shape[0]
    n_seq = m // seq
    rope = rope_tables is not None
    const = lambda i: (0, 0)
    tps = max(seq // tm, 1)
    mod_idx = (lambda i: (0, 0, 0)) if tm >= seq else (lambda i: (i // tps, 0, 0))
    in_specs = [
        pl.BlockSpec((tm, D_MODEL), lambda i: (i, 0)),
        pl.BlockSpec((1, 1, 3 * D_MODEL), mod_idx),
        _layer_block((D_MODEL, IN_WIDTH), layer),
        _layer_block((T_ROWS, D_MODEL), layer),
        pl.BlockSpec((HEAD_DIM, tm), const),
        pl.BlockSpec((HEAD_DIM, tm), const),
        pl.BlockSpec((1, WIDTH), const),
        pl.BlockSpec((1, WIDTH), const),
    ]
    args = [x, mod, w_in, wt, qn, kn, sg, sb]
    if rope:
        in_specs += [pl.BlockSpec((HALF, tm), lambda i: (0, i % tps))] * 2
        args += list(rope_tables)
    qt_spec = pl.BlockSpec((WIDTH, tm), lambda i: (0, i))
    k_spec = pl.BlockSpec((tm, KV_WIDTH), lambda i: (i, 0))
    vt_shape = (KV_WIDTH, m)
    vt_spec = pl.BlockSpec((KV_WIDTH, tm), lambda i: (0, i))
    if rope:
        vbt_shape = (m // LANES, KV_WIDTH, LANES)
        vbt_spec = pl.BlockSpec((tm // LANES, KV_WIDTH, LANES), lambda i: (i, 0, 0))
    else:
        vbt_shape, vbt_spec = vt_shape, vt_spec
    out_specs = [qt_spec, qt_spec, k_spec, k_spec, vt_spec, vbt_spec,
                 pl.BlockSpec((tm, PZ_WIDTH), lambda i: (i, 0))]
    out_shape = [jax.ShapeDtypeStruct((WIDTH, m), BF16)] * 2
    out_shape += [jax.ShapeDtypeStruct((m, KV_WIDTH), BF16)] * 2
    out_shape += [jax.ShapeDtypeStruct(vt_shape, BF16), jax.ShapeDtypeStruct(vbt_shape, BF16)]
    out_shape += [jax.ShapeDtypeStruct((m, PZ_WIDTH), BF16)]
    if with_state:
        out_specs += [pl.BlockSpec((tm // seq, KV_WIDTH, seq), lambda i: (i, 0, 0))] * 4
        out_shape += [jax.ShapeDtypeStruct((n_seq, KV_WIDTH, seq), F32)] * 4
    return pl.pallas_call(
        functools.partial(_front_kernel, rope=rope, with_state=with_state),
        grid=(m // tm,),
        in_specs=in_specs,
        out_specs=out_specs,
        out_shape=out_shape,
        compiler_params=pltpu.CompilerParams(
            dimension_semantics=("arbitrary",), vmem_limit_bytes=VMEM_LIMIT),
        name="front_latent" if rope else "front_context",
    )(*args)


def _scores_t(q_heads_t, sources, kv_head):
    top = jnp.concatenate(q_heads_t, axis=1) if len(q_heads_t) > 1 else q_heads_t[0]
    zeros = jnp.zeros_like(top)
    q_aug = jnp.concatenate([top, zeros] if kv_head == 0 else [zeros, top], axis=0)
    scores = []
    for k, _, mask in sources:
        s = _dot(k, q_aug)
        if mask is not None:
            s = jnp.where(mask, s, NEG_INF)
        scores.append(s)
    return scores


def _softmax_pv_t(scores, sources, sinks, kv_head, tq):
    m = None
    for s in scores:
        ms = jnp.max(s, axis=0, keepdims=True)
        m = ms if m is None else jnp.maximum(m, ms)
    if sinks is not None:
        sink = jnp.concatenate([jnp.full((1, tq), sk * LOG2E, F32) for sk in sinks], axis=1)
        m = jnp.maximum(m, sink)
    p_t = [jnp.exp2(s - m).astype(BF16) for s in scores]
    p_t = jnp.concatenate(p_t, axis=0) if len(p_t) > 1 else p_t[0]
    rows = slice(kv_head * HEAD_DIM, (kv_head + 1) * HEAD_DIM)
    v_t = [src[1][rows, :] for src in sources]
    v_t = jnp.concatenate(v_t, axis=1) if len(v_t) > 1 else v_t[0]
    v_aug = jnp.concatenate([v_t, jnp.ones((BF16_SUBLANES, v_t.shape[1]), BF16)], axis=0)
    r = _dot(v_aug, p_t)
    denom = r[HEAD_DIM:HEAD_DIM + 1]
    if sinks is not None:
        denom = denom + jnp.exp2(sink - m)
    return r[:HEAD_DIM] * (1.0 / denom)


def _attention_branches(branches, pz_ref, z_ref, heads_per_dot):
    tq = pz_ref.shape[0]
    items = [(br, kv_head, h0) for br in range(len(branches)) for kv_head in range(N_KV)
             for h0 in range(kv_head * GROUP, (kv_head + 1) * GROUP, heads_per_dot)]

    def scores_of(item):
        br, kv_head, h0 = item
        qt_ref, sources = branches[br][:2]
        q_heads = [qt_ref[h * HEAD_DIM:(h + 1) * HEAD_DIM, :] for h in range(h0, h0 + heads_per_dot)]
        return _scores_t(q_heads, sources, kv_head)

    pending = scores_of(items[0])
    for i, (br, kv_head, h0) in enumerate(items):
        scores = pending
        if i + 1 < len(items):
            pending = scores_of(items[i + 1])
        _, sources, sink_ref, gate_block, z_block = branches[br]
        sinks = None if sink_ref is None else [sink_ref[h] for h in range(h0, h0 + heads_per_dot)]
        o_t = _softmax_pv_t(scores, sources, sinks, kv_head, tq)
        for j in range(0, heads_per_dot, 2):
            pair = (h0 + j) // 2
            o = o_t[:, j * tq:(j + 2) * tq]
            o = jnp.concatenate([o[:, :tq], o[:, tq:]], axis=0).T
            gate = pz_ref[:, gate_block * WIDTH + pair * LANES: gate_block * WIDTH + (pair + 1) * LANES]
            z_ref[:, z_block * WIDTH + pair * LANES: z_block * WIDTH + (pair + 1) * LANES] = (
                o * gate.astype(F32)).astype(BF16)


def _spatial_gate(pz_ref, ws_ref, bs_ref, z_ref, n_rows):
    for ch in range(n_rows // CHUNK):
        rows = slice(ch * CHUNK, (ch + 1) * CHUNK)
        for g in range(C_GROUPS):
            cols = lambda blk: slice(blk * WIDTH + g * LANES, blk * WIDTH + (g + 1) * LANES)
            mixed = _dot(ws_ref[g], pz_ref[rows, cols(PZ_VN)]) + bs_ref[:, g * LANES:(g + 1) * LANES]
            zc = pz_ref[rows, cols(PZ_UC)].astype(F32) * mixed * pz_ref[rows, cols(PZ_GC)].astype(F32)
            z_ref[rows, 2 * WIDTH + g * LANES: 2 * WIDTH + (g + 1) * LANES] = zc.astype(BF16)


def _mixer_context_kernel(qat_ref, qbt_ref, ka_ref, kb_ref, vat_ref, vbt_ref, pz_ref,
                          sink_ref, ws_ref, bs_ref, z_ref):
    _attention_branches(
        [(qat_ref, [(ka_ref[...], vat_ref[...], None)], None, PZ_GA, 0),
         (qbt_ref, [(kb_ref[...], vbt_ref[...], None)], sink_ref, PZ_GB, 1)],
        pz_ref, z_ref, GROUP)
    _spatial_gate(pz_ref, ws_ref, bs_ref, z_ref, pz_ref.shape[0])


def _mixer_context(qat, qbt, ka, kb, vat, vbt, pz, sink, ws, bs, *, layer, seq):
    m = pz.shape[0]
    qt_spec = pl.BlockSpec((WIDTH, seq), lambda i: (0, i))
    k_spec = pl.BlockSpec((seq, KV_WIDTH), lambda i: (i, 0))
    vt_spec = pl.BlockSpec((KV_WIDTH, seq), lambda i: (0, i))
    return pl.pallas_call(
        _mixer_context_kernel,
        grid=(m // seq,),
        in_specs=[
            qt_spec, qt_spec, k_spec, k_spec, vt_spec, vt_spec,
            pl.BlockSpec((seq, PZ_WIDTH), lambda i: (i, 0)),
            pl.BlockSpec(memory_space=pltpu.SMEM),
            _layer_block((C_GROUPS, CHUNK, CHUNK), layer),
            pl.BlockSpec((CHUNK, WIDTH), lambda i: (0, 0)),
        ],
        out_specs=pl.BlockSpec((seq, 3 * WIDTH), lambda i: (i, 0)),
        out_shape=jax.ShapeDtypeStruct((m, 3 * WIDTH), BF16),
        compiler_params=pltpu.CompilerParams(
            dimension_semantics=("arbitrary",), vmem_limit_bytes=VMEM_LIMIT),
        name="mixer_context",
    )(qat, qbt, ka, kb, vat, vbt, pz, sink, ws, bs)


def _mixer_latent_kernel(qat_ref, qbt_ref, ka_ref, kb_ref, vat_ref, vbt_ref, pz_ref,
                         cak_ref, cav_ref, cbk_ref, cbv_ref, sink_ref, ws_ref, bs_ref,
                         z_ref, cak_s, cav_s, cbk_s, cbv_s, *, seq, heads_per_dot):
    tq = pz_ref.shape[0]
    band = tq + 2 * WINDOW
    qi = pl.program_id(1)

    @pl.when(qi == 0)
    def _():
        cak_s[...] = cak_ref[0, 0].T.astype(BF16)
        cbk_s[...] = cbk_ref[0, 0].T.astype(BF16)
        cav_s[...] = cav_ref[0, 0].astype(BF16)
        cbv_s[...] = cbv_ref[0, 0].astype(BF16)

    start_blk = jnp.clip(qi * (tq // LANES) - WINDOW // LANES, 0, (seq - band) // LANES)
    start = pl.multiple_of(start_blk * LANES, LANES)
    k_pos = start + lax.broadcasted_iota(jnp.int32, (band, heads_per_dot * tq), 0)
    q_pos = qi * tq + (lax.broadcasted_iota(jnp.int32, (band, heads_per_dot * tq), 1) & (tq - 1))
    in_window = jnp.abs(q_pos - k_pos) <= WINDOW
    vb_win = jnp.concatenate([vbt_ref[start_blk + j] for j in range(band // LANES)], axis=1)

    _attention_branches(
        [(qat_ref, [(ka_ref[...], vat_ref[...], None), (cak_s[...], cav_s[...], None)], None, PZ_GA, 0),
         (qbt_ref, [(kb_ref[pl.ds(start, band), :], vb_win, in_window), (cbk_s[...], cbv_s[...], None)],
          sink_ref, PZ_GB, 1)],
        pz_ref, z_ref, heads_per_dot)
    _spatial_gate(pz_ref, ws_ref, bs_ref, z_ref, tq)


def _mixer_latent(qat, qbt, ka, kb, vat, vbt, pz, cak, cav, cbk, cbv, sink, ws, bs, *, layer, seq, tq):
    m = pz.shape[0]
    n_seq = m // seq
    nq = seq // tq
    past = cak.shape[3]
    qt_spec = pl.BlockSpec((WIDTH, tq), lambda b, q: (0, b * nq + q))
    k_spec = pl.BlockSpec((seq, KV_WIDTH), lambda b, q: (b, 0))
    cache_spec = pl.BlockSpec((1, 1, KV_WIDTH, past), lambda b, q: (b, layer, 0, 0))
    return pl.pallas_call(
        functools.partial(_mixer_latent_kernel, seq=seq, heads_per_dot=2),
        grid=(n_seq, nq),
        in_specs=[
            qt_spec, qt_spec, k_spec, k_spec,
            pl.BlockSpec((KV_WIDTH, seq), lambda b, q: (0, b)),
            pl.BlockSpec((seq // LANES, KV_WIDTH, LANES), lambda b, q: (b, 0, 0)),
            pl.BlockSpec((tq, PZ_WIDTH), lambda b, q: (b * nq + q, 0)),
            cache_spec, cache_spec, cache_spec, cache_spec,
            pl.BlockSpec(memory_space=pltpu.SMEM),
            _layer_block((C_GROUPS, CHUNK, CHUNK), layer),
            pl.BlockSpec((CHUNK, WIDTH), lambda b, q: (0, 0)),
        ],
        out_specs=pl.BlockSpec((tq, 3 * WIDTH), lambda b, q: (b * nq + q, 0)),
        out_shape=jax.ShapeDtypeStruct((m, 3 * WIDTH), BF16),
        scratch_shapes=[pltpu.VMEM((past, KV_WIDTH), BF16), pltpu.VMEM((KV_WIDTH, past), BF16),
                        pltpu.VMEM((past, KV_WIDTH), BF16), pltpu.VMEM((KV_WIDTH, past), BF16)],
        compiler_params=pltpu.CompilerParams(
            dimension_semantics=("arbitrary", "arbitrary"), vmem_limit_bytes=VMEM_LIMIT),
        name="mixer_latent",
    )(qat, qbt, ka, kb, vat, vbt, pz, cak, cav, cbk, cbv, sink, ws, bs)


def _back_kernel(x_ref, z_ref, mod_ref, wg_ref, bg_ref, wpa_ref, wpb_ref, wpc_ref, wo_ref,
                 lg_ref, lb_ref, o_ref, *, n_sub):
    mod = mod_ref[0]
    sub = x_ref.shape[0] // n_sub
    for s in range(n_sub):
        rows = slice(s * sub, (s + 1) * sub)
        x = x_ref[rows, :]
        h = _modulated_norm(x, mod).astype(BF16)
        mix = None
        for br, wp_ref in enumerate((wpa_ref, wpb_ref, wpc_ref)):
            cols = slice(br * D_MODEL, (br + 1) * D_MODEL)
            g = jax.nn.sigmoid(_dot(h, wg_ref[:, cols]) + bg_ref[:, cols])
            y = _dot(z_ref[rows, br * WIDTH:(br + 1) * WIDTH], wp_ref[...])
            mix = g * y if mix is None else mix + g * y
        y = _dot(mix.astype(BF16), wo_ref[...])
        r = DEEPNORM_ALPHA * x + mod[:, 2 * D_MODEL:] * y
        mu = jnp.mean(r, axis=-1, keepdims=True)
        rc = r - mu
        var = jnp.mean(rc * rc, axis=-1, keepdims=True)
        o_ref[rows, :] = rc * lax.rsqrt(var + EPS) * lg_ref[...] + lb_ref[...]


def _back(x, z, mod, wg, bg, wpa, wpb, wpc, wo, lg, lb, *, layer, tm, rows_per_mod, name):
    m = x.shape[0]
    tiles_per_mod = rows_per_mod // tm
    return pl.pallas_call(
        functools.partial(_back_kernel, n_sub=2),
        grid=(m // tm,),
        in_specs=[
            pl.BlockSpec((tm, D_MODEL), lambda i: (i, 0)),
            pl.BlockSpec((tm, 3 * WIDTH), lambda i: (i, 0)),
            pl.BlockSpec((1, 1, 3 * D_MODEL), lambda i: (i // tiles_per_mod, 0, 0)),
            _layer_block((D_MODEL, 3 * D_MODEL), layer),
            _layer_block((1, 3 * D_MODEL), layer),
            _layer_block((WIDTH, D_MODEL), layer),
            _layer_block((WIDTH, D_MODEL), layer),
            _layer_block((WIDTH, D_MODEL), layer),
            _layer_block((D_MODEL, D_MODEL), layer),
            _layer_block((1, D_MODEL), layer),
            _layer_block((1, D_MODEL), layer),
        ],
        out_specs=pl.BlockSpec((tm, D_MODEL), lambda i: (i, 0)),
        out_shape=jax.ShapeDtypeStruct((m, D_MODEL), F32),
        compiler_params=pltpu.CompilerParams(
            dimension_semantics=("arbitrary",), vmem_limit_bytes=VMEM_LIMIT),
        name=name,
    )(x, z, mod, wg, bg, wpa, wpb, wpc, wo, lg, lb)


def _rope_tables(n_tokens):
    rows = n_tokens // GRID_W
    row = jnp.repeat(jnp.arange(rows, dtype=F32), GRID_W)
    col = jnp.tile(jnp.arange(GRID_W, dtype=F32), rows)
    inv_freq = jnp.power(ROPE_BASE, -jnp.arange(ROPE_FREQS, dtype=F32) / ROPE_FREQS)
    ang = jnp.concatenate([inv_freq[:, None] * row[None], inv_freq[:, None] * col[None]], axis=0)
    return jnp.cos(ang), jnp.sin(ang)


def _transposed_cache(t):
    b, depth, past = t.shape[:3]
    return jnp.transpose(t, (0, 1, 3, 4, 2)).reshape(b, depth, KV_WIDTH, past)


def kernel(x_prompt, x_sample, cache_a_k, cache_a_v, cache_b_k, cache_b_v, c, c_ctx, w_mod, b_mod,
           w_in, q_norm, k_norm, sink, sgu_g, sgu_b, w_spatial, b_spatial, w_proj_a, w_proj_b,
           w_proj_c, w_gate, b_gate, w_out, ln_g, ln_b):
    batch, seq, _ = x_prompt.shape
    dec_batch, dec_seq, _ = x_sample.shape
    tm = 512

    cvecs = jnp.concatenate(
        [c_ctx[None], c, jnp.zeros((MOD_ROWS - 1 - dec_batch, D_MODEL), F32)], axis=0)
    mod = _modulation(cvecs, w_mod, b_mod)

    rope_tables = _rope_tables(dec_seq)
    caches = [_transposed_cache(t) for t in (cache_a_k, cache_a_v, cache_b_k, cache_b_v)]

    w_in_bf = w_in.astype(BF16)
    wt_bf = jnp.swapaxes(lax.optimization_barrier(jnp.concatenate(
        [w_in[:, :, lo:lo + width] for lo, width in _T_SEGMENTS], axis=-1)), 1, 2).astype(BF16)
    wg_bf, wo_bf, ws_bf = w_gate.astype(BF16), w_out.astype(BF16), w_spatial.astype(BF16)
    wpa_bf, wpb_bf, wpc_bf = w_proj_a.astype(BF16), w_proj_b.astype(BF16), w_proj_c.astype(BF16)
    bg3, lg3, lb3 = b_gate[:, None], ln_g[:, None], ln_b[:, None]

    xp = x_prompt.reshape(batch * seq, D_MODEL)
    xs = x_sample.reshape(dec_batch * dec_seq, D_MODEL)
    states = []
    for l in range(DEPTH):
        bs = jnp.repeat(b_spatial[l].T, LANES, axis=1)
        qn = jnp.broadcast_to(q_norm[l][:, None], (HEAD_DIM, tm))
        kn = jnp.broadcast_to(k_norm[l][:, None], (HEAD_DIM, tm))
        sg, sb = sgu_g[l][None], sgu_b[l][None]
        mod_ctx = mod[l, 0:1][None]
        mod_lat = mod[l, 1:1 + dec_batch][:, None]

        *acts, st_ak, st_av, st_bk, st_bv = _front(
            xp, mod_ctx, w_in_bf, wt_bf, qn, kn, sg, sb, None, layer=l, tm=tm, seq=seq, with_state=True)
        states.append((st_ak, st_av, st_bk, st_bv))
        z = _mixer_context(*acts, sink[l], ws_bf, bs, layer=l, seq=seq)
        xp = _back(xp, z, mod_ctx, wg_bf, bg3, wpa_bf, wpb_bf, wpc_bf, wo_bf, lg3, lb3,
                   layer=l, tm=2 * tm, rows_per_mod=batch * seq, name="back_context")

        acts = _front(xs, mod_lat, w_in_bf, wt_bf, qn, kn, sg, sb, rope_tables,
                      layer=l, tm=tm, seq=dec_seq, with_state=False)
        z = _mixer_latent(*acts, *caches, sink[l], ws_bf, bs, layer=l, seq=dec_seq, tq=256)
        xs = _back(xs, z, mod_lat, wg_bf, bg3, wpa_bf, wpb_bf, wpc_bf, wo_bf, lg3, lb3,
                   layer=l, tm=2 * tm, rows_per_mod=dec_seq, name="back_latent")

    def state(kind):
        t = jnp.stack([st[kind] for st in states], axis=1)
        return jnp.transpose(t.reshape(batch, DEPTH, N_KV, HEAD_DIM, seq), (0, 1, 4, 2, 3))

    return (xp.reshape(batch, seq, D_MODEL), xs.reshape(dec_batch, dec_seq, D_MODEL),
            state(0), state(1), state(2), state(3))
```

```python
import functools

import jax
import jax.numpy as jnp
from jax import lax
from jax.experimental import pallas as pl
from jax.experimental.pallas import tpu as pltpu

D_MODEL = 1024
DEPTH = 4
GRID_W = 64
HEAD_DIM = 64
HALF = HEAD_DIM // 2
N_HEADS = 8
N_KV = 2
GROUP = N_HEADS // N_KV
KV_WIDTH = N_KV * HEAD_DIM
WIDTH = 512
C_GROUPS = 4
CHUNK = 128
WINDOW = 128
IN_WIDTH = 4096
ROPE_BASE = 10000.0
ROPE_FREQS = HEAD_DIM // 4
EPS = 1e-6
NEG_INF = -1e30
DEEPNORM_ALPHA = (2 * DEPTH) ** 0.25
LOG2E = 1.4426950408889634
Q_SCALE = HEAD_DIM ** -0.5 * LOG2E

LANES = 128
BF16_SUBLANES = 16
MOD_ROWS = 8

_QA, _KA, _VA, _GA = 0, 512, 640, 768
_QB, _KB, _VB, _GB = 1280, 1792, 1920, 2048
_UC, _VC, _GC = 2560, 3072, 3584
_T_SEGMENTS = ((_QA, WIDTH), (_KA, KV_WIDTH), (_VA, KV_WIDTH), (_QB, WIDTH), (_KB, KV_WIDTH), (_VB, KV_WIDTH))
T_QA, T_KA, T_VA, T_QB, T_KB, T_VB = 0, 512, 640, 768, 1280, 1408
T_ROWS = 1536

(PZ_GA, PZ_GB, PZ_UC, PZ_VN, PZ_GC) = range(5)
PZ_WIDTH = 5 * WIDTH

BF16 = jnp.bfloat16
F32 = jnp.float32

VMEM_LIMIT = 56 * 1024 * 1024
BACK_SUB_TILES = 4
BACK_STAGE_LAG = 4


def _silu(t):
    return t * jax.nn.sigmoid(t)


def _dot(a, b):
    return jnp.dot(a, b, preferred_element_type=F32)


def _dot_nt(a, b):
    return lax.dot_general(a, b, (((1,), (1,)), ((), ())), preferred_element_type=F32)


def _layer_block(shape, layer):
    zeros = (0,) * len(shape)
    return pl.BlockSpec((None,) + tuple(shape), lambda *_: (layer,) + zeros,
                        pipeline_mode=pl.Buffered(1))


def _mod_kernel(c_ref, w_ref, b_ref, o_ref):
    s = _silu(c_ref[...]).astype(BF16)
    o_ref[0] = _dot(s, w_ref[0].astype(BF16)) + b_ref[0]


def _modulation(cvecs, w_mod, b_mod):
    tn = 1024
    return pl.pallas_call(
        _mod_kernel,
        grid=(DEPTH, 3 * D_MODEL // tn),
        in_specs=[
            pl.BlockSpec((MOD_ROWS, D_MODEL), lambda l, j: (0, 0)),
            pl.BlockSpec((1, D_MODEL, tn), lambda l, j: (l, 0, j)),
            pl.BlockSpec((1, 1, tn), lambda l, j: (l, 0, j)),
        ],
        out_specs=pl.BlockSpec((1, MOD_ROWS, tn), lambda l, j: (l, 0, j)),
        out_shape=jax.ShapeDtypeStruct((DEPTH, MOD_ROWS, 3 * D_MODEL), F32),
        compiler_params=pltpu.CompilerParams(
            dimension_semantics=("arbitrary", "arbitrary"), vmem_limit_bytes=VMEM_LIMIT),
        name="modulation",
    )(cvecs, w_mod, b_mod.reshape(DEPTH, 1, 3 * D_MODEL))


def _modulated_norm(x, mod):
    mu = jnp.mean(x, axis=-1, keepdims=True)
    xc = x - mu
    var = jnp.mean(xc * xc, axis=-1, keepdims=True)
    xn = xc * lax.rsqrt(var + EPS)
    return xn * (1.0 + mod[:, D_MODEL:2 * D_MODEL]) + mod[:, :D_MODEL]


def _write_token_blocks(ref, t):
    w = ref.shape[2]
    for j in range(ref.shape[0]):
        ref[j] = t[:, j * w:(j + 1) * w].astype(ref.dtype)


def _front_kernel(*refs, rope, with_state):
    refs = list(refs)
    x_ref, mod_ref, w_ref, wt_ref, qn_ref, kn_ref, sg_ref, sb_ref = refs[:8]
    refs = refs[8:]
    if rope:
        cos_ref, sin_ref = refs[:2]
        refs = refs[2:]
    qat_ref, qbt_ref, ka_ref, kb_ref, vat_ref, vbt_ref, pz_ref = refs[:7]
    st_refs = refs[7:] if with_state else None

    h = _modulated_norm(x_ref[...], mod_ref[0]).astype(BF16)

    def rotary_t(t):
        if not rope:
            return t
        x1, x2 = t[:HALF], t[HALF:]
        c, s = cos_ref[...], sin_ref[...]
        return jnp.concatenate([x1 * c - x2 * s, x2 * c + x1 * s], axis=0)

    def rms_t(t, gain):
        ms = jnp.mean(t * t, axis=0, keepdims=True)
        return t * lax.rsqrt(ms + EPS) * gain

    pt = _dot_nt(wt_ref[...], h)

    def heads_t(row0, n):
        return [pt[row0 + j * HEAD_DIM: row0 + (j + 1) * HEAD_DIM] for j in range(n)]

    qn, kn = qn_ref[...], kn_ref[...]
    qa = [rotary_t(rms_t(t, qn)) * Q_SCALE for t in heads_t(T_QA, N_HEADS)]
    qat_ref[...] = jnp.concatenate(qa, axis=0).astype(BF16)
    qb = [rotary_t(t) * Q_SCALE for t in heads_t(T_QB, N_HEADS)]
    qbt_ref[...] = jnp.concatenate(qb, axis=0).astype(BF16)

    ka = [rms_t(t, kn) for t in heads_t(T_KA, N_KV)]
    kb = heads_t(T_KB, N_KV)
    ka_ref[...] = jnp.concatenate([rotary_t(t) for t in ka], axis=0).T.astype(BF16)
    kb_ref[...] = jnp.concatenate([rotary_t(t) for t in kb], axis=0).T.astype(BF16)
    va = pt[T_VA:T_VA + KV_WIDTH]
    vb = pt[T_VB:T_VB + KV_WIDTH]
    vat_ref[...] = va.astype(BF16)
    if len(vbt_ref.shape) == 3:
        _write_token_blocks(vbt_ref, vb)
    else:
        vbt_ref[...] = vb.astype(BF16)

    if with_state:
        _write_token_blocks(st_refs[0], jnp.concatenate(ka, axis=0))
        _write_token_blocks(st_refs[1], va)
        _write_token_blocks(st_refs[2], jnp.concatenate(kb, axis=0))
        _write_token_blocks(st_refs[3], vb)

    def proj(lo):
        return _dot(h, w_ref[:, lo:lo + WIDTH])

    def store(block, val):
        pz_ref[:, block * WIDTH:(block + 1) * WIDTH] = val.astype(BF16)

    store(PZ_GA, _silu(proj(_GA)))
    store(PZ_GB, _silu(proj(_GB)))
    store(PZ_UC, proj(_UC))
    vc = proj(_VC)
    mu = jnp.mean(vc, axis=-1, keepdims=True)
    vcc = vc - mu
    var = jnp.mean(vcc * vcc, axis=-1, keepdims=True)
    store(PZ_VN, vcc * lax.rsqrt(var + EPS) * sg_ref[...] + sb_ref[...])
    store(PZ_GC, _silu(proj(_GC)))


def _front(x, mod, w_in, wt, qn, kn, sg, sb, rope_tables, *, layer, tm, seq, with_state):
    m = x.shape[0]
    n_seq = m // seq
    rope = rope_tables is not None
    const = lambda i: (0, 0)
    tps = max(seq // tm, 1)
    mod_idx = (lambda i: (0, 0, 0)) if tm >= seq else (lambda i: (i // tps, 0, 0))
    in_specs = [
        pl.BlockSpec((tm, D_MODEL), lambda i: (i, 0)),
        pl.BlockSpec((1, 1, 3 * D_MODEL), mod_idx),
        _layer_block((D_MODEL, IN_WIDTH), layer),
        _layer_block((T_ROWS, D_MODEL), layer),
        pl.BlockSpec((HEAD_DIM, tm), const),
        pl.BlockSpec((HEAD_DIM, tm), const),
        pl.BlockSpec((1, WIDTH), const),
        pl.BlockSpec((1, WIDTH), const),
    ]
    args = [x, mod, w_in, wt, qn, kn, sg, sb]
    if rope:
        in_specs += [pl.BlockSpec((HALF, tm), lambda i: (0, i % tps))] * 2
        args += list(rope_tables)
    qt_spec = pl.BlockSpec((WIDTH, tm), lambda i: (0, i))
    k_spec = pl.BlockSpec((tm, KV_WIDTH), lambda i: (i, 0))
    vt_shape = (KV_WIDTH, m)
    vt_spec = pl.BlockSpec((KV_WIDTH, tm), lambda i: (0, i))
    if rope:
        vbt_shape = (m // LANES, KV_WIDTH, LANES)
        vbt_spec = pl.BlockSpec((tm // LANES, KV_WIDTH, LANES), lambda i: (i, 0, 0))
    else:
        vbt_shape, vbt_spec = vt_shape, vt_spec
    out_specs = [qt_spec, qt_spec, k_spec, k_spec, vt_spec, vbt_spec,
                 pl.BlockSpec((tm, PZ_WIDTH), lambda i: (i, 0))]
    out_shape = [jax.ShapeDtypeStruct((WIDTH, m), BF16)] * 2
    out_shape += [jax.ShapeDtypeStruct((m, KV_WIDTH), BF16)] * 2
    out_shape += [jax.ShapeDtypeStruct(vt_shape, BF16), jax.ShapeDtypeStruct(vbt_shape, BF16)]
    out_shape += [jax.ShapeDtypeStruct((m, PZ_WIDTH), BF16)]
    if with_state:
        out_specs += [pl.BlockSpec((tm // seq, KV_WIDTH, seq), lambda i: (i, 0, 0))] * 4
        out_shape += [jax.ShapeDtypeStruct((n_seq, KV_WIDTH, seq), F32)] * 4
    return pl.pallas_call(
        functools.partial(_front_kernel, rope=rope, with_state=with_state),
        grid=(m // tm,),
        in_specs=in_specs,
        out_specs=out_specs,
        out_shape=out_shape,
        compiler_params=pltpu.CompilerParams(
            dimension_semantics=("arbitrary",), vmem_limit_bytes=VMEM_LIMIT),
        name="front_latent" if rope else "front_context",
    )(*args)


def _scores_t(q_heads_t, sources, kv_head):
    top = jnp.concatenate(q_heads_t, axis=1) if len(q_heads_t) > 1 else q_heads_t[0]
    zeros = jnp.zeros_like(top)
    q_aug = jnp.concatenate([top, zeros] if kv_head == 0 else [zeros, top], axis=0)
    scores = []
    for k, _, mask in sources:
        s = _dot(k, q_aug)
        if mask is not None:
            s = jnp.where(mask, s, NEG_INF)
        scores.append(s)
    return scores


def _softmax_pv_t(scores, sources, sinks, kv_head, tq):
    m = None
    for s in scores:
        ms = jnp.max(s, axis=0, keepdims=True)
        m = ms if m is None else jnp.maximum(m, ms)
    if sinks is not None:
        sink = jnp.concatenate([jnp.full((1, tq), sk * LOG2E, F32) for sk in sinks], axis=1)
        m = jnp.maximum(m, sink)
    rows = slice(kv_head * HEAD_DIM, (kv_head + 1) * HEAD_DIM)
    r = None
    for s, src in zip(scores, sources):
        v_t = src[1][rows, :]
        v_aug = jnp.concatenate([v_t, jnp.ones((BF16_SUBLANES, v_t.shape[1]), BF16)], axis=0)
        pv = _dot(v_aug, jnp.exp2(s - m).astype(BF16))
        r = pv if r is None else r + pv
    denom = r[HEAD_DIM:HEAD_DIM + 1]
    if sinks is not None:
        denom = denom + jnp.exp2(sink - m)
    return r[:HEAD_DIM] * (1.0 / denom)


def _attention_branches(branches, pz_ref, z_ref, heads_per_dot):
    tq = pz_ref.shape[0]
    items = [(br, kv_head, h0) for br in range(len(branches)) for kv_head in range(N_KV)
             for h0 in range(kv_head * GROUP, (kv_head + 1) * GROUP, heads_per_dot)]

    def scores_of(item):
        br, kv_head, h0 = item
        qt_ref, sources = branches[br][:2]
        q_heads = [qt_ref[h * HEAD_DIM:(h + 1) * HEAD_DIM, :] for h in range(h0, h0 + heads_per_dot)]
        return _scores_t(q_heads, sources, kv_head)

    pending = scores_of(items[0])
    for i, (br, kv_head, h0) in enumerate(items):
        scores = pending
        if i + 1 < len(items):
            pending = scores_of(items[i + 1])
        _, sources, sink_ref, gate_block, z_block = branches[br]
        sinks = None if sink_ref is None else [sink_ref[h] for h in range(h0, h0 + heads_per_dot)]
        o_t = _softmax_pv_t(scores, sources, sinks, kv_head, tq)
        for j in range(0, heads_per_dot, 2):
            pair = (h0 + j) // 2
            o = o_t[:, j * tq:(j + 2) * tq]
            o = jnp.concatenate([o[:, :tq], o[:, tq:]], axis=0).T
            gate = pz_ref[:, gate_block * WIDTH + pair * LANES: gate_block * WIDTH + (pair + 1) * LANES]
            z_ref[:, z_block * WIDTH + pair * LANES: z_block * WIDTH + (pair + 1) * LANES] = (
                o * gate.astype(F32)).astype(BF16)


def _spatial_gate(pz_ref, ws_ref, bs_ref, z_ref, n_rows):
    for ch in range(n_rows // CHUNK):
        rows = slice(ch * CHUNK, (ch + 1) * CHUNK)
        for g in range(C_GROUPS):
            cols = lambda blk: slice(blk * WIDTH + g * LANES, blk * WIDTH + (g + 1) * LANES)
            mixed = _dot(ws_ref[g], pz_ref[rows, cols(PZ_VN)]) + bs_ref[:, g * LANES:(g + 1) * LANES]
            zc = pz_ref[rows, cols(PZ_UC)].astype(F32) * mixed * pz_ref[rows, cols(PZ_GC)].astype(F32)
            z_ref[rows, 2 * WIDTH + g * LANES: 2 * WIDTH + (g + 1) * LANES] = zc.astype(BF16)


def _mixer_context_kernel(qat_ref, qbt_ref, ka_ref, kb_ref, vat_ref, vbt_ref, pz_ref,
                          sink_ref, ws_ref, bs_ref, z_ref):
    _attention_branches(
        [(qat_ref, [(ka_ref[...], vat_ref[...], None)], None, PZ_GA, 0),
         (qbt_ref, [(kb_ref[...], vbt_ref[...], None)], sink_ref, PZ_GB, 1)],
        pz_ref, z_ref, GROUP)
    _spatial_gate(pz_ref, ws_ref, bs_ref, z_ref, pz_ref.shape[0])


def _mixer_context(qat, qbt, ka, kb, vat, vbt, pz, sink, ws, bs, *, layer, seq):
    m = pz.shape[0]
    qt_spec = pl.BlockSpec((WIDTH, seq), lambda i: (0, i))
    k_spec = pl.BlockSpec((seq, KV_WIDTH), lambda i: (i, 0))
    vt_spec = pl.BlockSpec((KV_WIDTH, seq), lambda i: (0, i))
    return pl.pallas_call(
        _mixer_context_kernel,
        grid=(m // seq,),
        in_specs=[
            qt_spec, qt_spec, k_spec, k_spec, vt_spec, vt_spec,
            pl.BlockSpec((seq, PZ_WIDTH), lambda i: (i, 0)),
            pl.BlockSpec(memory_space=pltpu.SMEM),
            _layer_block((C_GROUPS, CHUNK, CHUNK), layer),
            pl.BlockSpec((CHUNK, WIDTH), lambda i: (0, 0)),
        ],
        out_specs=pl.BlockSpec((seq, 3 * WIDTH), lambda i: (i, 0)),
        out_shape=jax.ShapeDtypeStruct((m, 3 * WIDTH), BF16),
        compiler_params=pltpu.CompilerParams(
            dimension_semantics=("arbitrary",), vmem_limit_bytes=VMEM_LIMIT),
        name="mixer_context",
    )(qat, qbt, ka, kb, vat, vbt, pz, sink, ws, bs)


def _mixer_latent_kernel(qat_ref, qbt_ref, ka_ref, kb_ref, vat_ref, vbt_ref, pz_ref,
                         cak_ref, cav_ref, cbk_ref, cbv_ref, sink_ref, ws_ref, bs_ref,
                         z_ref, cak_s, cav_s, cbk_s, cbv_s, *, seq, heads_per_dot):
    tq = pz_ref.shape[0]
    band = tq + 2 * WINDOW
    qi = pl.program_id(1)

    @pl.when(qi == 0)
    def _():
        cak_s[...] = cak_ref[0, 0].T.astype(BF16)
        cbk_s[...] = cbk_ref[0, 0].T.astype(BF16)
        cav_s[...] = cav_ref[0, 0].astype(BF16)
        cbv_s[...] = cbv_ref[0, 0].astype(BF16)

    start_blk = jnp.clip(qi * (tq // LANES) - WINDOW // LANES, 0, (seq - band) // LANES)
    start = pl.multiple_of(start_blk * LANES, LANES)
    k_pos = start + lax.broadcasted_iota(jnp.int32, (band, heads_per_dot * tq), 0)
    q_pos = qi * tq + (lax.broadcasted_iota(jnp.int32, (band, heads_per_dot * tq), 1) & (tq - 1))
    in_window = jnp.abs(q_pos - k_pos) <= WINDOW
    vb_win = jnp.concatenate([vbt_ref[start_blk + j] for j in range(band // LANES)], axis=1)

    _attention_branches(
        [(qat_ref, [(ka_ref[...], vat_ref[...], None), (cak_s[...], cav_s[...], None)], None, PZ_GA, 0),
         (qbt_ref, [(kb_ref[pl.ds(start, band), :], vb_win, in_window), (cbk_s[...], cbv_s[...], None)],
          sink_ref, PZ_GB, 1)],
        pz_ref, z_ref, heads_per_dot)
    _spatial_gate(pz_ref, ws_ref, bs_ref, z_ref, tq)


def _mixer_latent(qat, qbt, ka, kb, vat, vbt, pz, cak, cav, cbk, cbv, sink, ws, bs, *, layer, seq, tq):
    m = pz.shape[0]
    n_seq = m // seq
    nq = seq // tq
    past = cak.shape[3]
    qt_spec = pl.BlockSpec((WIDTH, tq), lambda b, q: (0, b * nq + q))
    k_spec = pl.BlockSpec((seq, KV_WIDTH), lambda b, q: (b, 0))
    cache_spec = pl.BlockSpec((1, 1, KV_WIDTH, past), lambda b, q: (b, layer, 0, 0))
    return pl.pallas_call(
        functools.partial(_mixer_latent_kernel, seq=seq, heads_per_dot=2),
        grid=(n_seq, nq),
        in_specs=[
            qt_spec, qt_spec, k_spec, k_spec,
            pl.BlockSpec((KV_WIDTH, seq), lambda b, q: (0, b)),
            pl.BlockSpec((seq // LANES, KV_WIDTH, LANES), lambda b, q: (b, 0, 0)),
            pl.BlockSpec((tq, PZ_WIDTH), lambda b, q: (b * nq + q, 0)),
            cache_spec, cache_spec, cache_spec, cache_spec,
            pl.BlockSpec(memory_space=pltpu.SMEM),
            _layer_block((C_GROUPS, CHUNK, CHUNK), layer),
            pl.BlockSpec((CHUNK, WIDTH), lambda b, q: (0, 0)),
        ],
        out_specs=pl.BlockSpec((tq, 3 * WIDTH), lambda b, q: (b * nq + q, 0)),
        out_shape=jax.ShapeDtypeStruct((m, 3 * WIDTH), BF16),
        scratch_shapes=[pltpu.VMEM((past, KV_WIDTH), BF16), pltpu.VMEM((KV_WIDTH, past), BF16),
                        pltpu.VMEM((past, KV_WIDTH), BF16), pltpu.VMEM((KV_WIDTH, past), BF16)],
        compiler_params=pltpu.CompilerParams(
            dimension_semantics=("arbitrary", "arbitrary"), vmem_limit_bytes=VMEM_LIMIT),
        name="mixer_latent",
    )(qat, qbt, ka, kb, vat, vbt, pz, cak, cav, cbk, cbv, sink, ws, bs)


def _back_kernel(x_ref, z_ref, mod_ref, wg_ref, bg_ref, wpa_ref, wpb_ref, wpc_ref, wo_ref,
                 lg_ref, lb_ref, o_ref, *, n_sub):
    mod = mod_ref[0]
    sub = x_ref.shape[0] // n_sub
    wp_refs = (wpa_ref, wpb_ref, wpc_ref)

    def stages(s):
        rows = slice(s * sub, (s + 1) * sub)
        st = {}

        def norm():
            st["h"] = _modulated_norm(x_ref[rows, :], mod).astype(BF16)

        def branch(br):
            cols = slice(br * D_MODEL, (br + 1) * D_MODEL)
            g = jax.nn.sigmoid(_dot(st["h"], wg_ref[:, cols]) + bg_ref[:, cols])
            y = _dot(z_ref[rows, br * WIDTH:(br + 1) * WIDTH], wp_refs[br][...])
            st["mix"] = g * y if br == 0 else st["mix"] + g * y

        def out_proj():
            st["y"] = _dot(st["mix"].astype(BF16), wo_ref[...])

        def residual_norm():
            r = DEEPNORM_ALPHA * x_ref[rows, :] + mod[:, 2 * D_MODEL:] * st["y"]
            mu = jnp.mean(r, axis=-1, keepdims=True)
            rc = r - mu
            var = jnp.mean(rc * rc, axis=-1, keepdims=True)
            o_ref[rows, :] = rc * lax.rsqrt(var + EPS) * lg_ref[...] + lb_ref[...]

        return [norm] + [functools.partial(branch, br) for br in range(3)] + [out_proj, residual_norm]

    chains = [stages(s) for s in range(n_sub)]
    n_stage = len(chains[0])
    for t in range(n_stage + (n_sub - 1) * BACK_STAGE_LAG):
        for s in range(n_sub):
            k = t - s * BACK_STAGE_LAG
            if 0 <= k < n_stage:
                chains[s][k]()


def _back(x, z, mod, wg, bg, wpa, wpb, wpc, wo, lg, lb, *, layer, tm, rows_per_mod, name):
    m = x.shape[0]
    tiles_per_mod = rows_per_mod // tm
    return pl.pallas_call(
        functools.partial(_back_kernel, n_sub=BACK_SUB_TILES),
        grid=(m // tm,),
        in_specs=[
            pl.BlockSpec((tm, D_MODEL), lambda i: (i, 0)),
            pl.BlockSpec((tm, 3 * WIDTH), lambda i: (i, 0)),
            pl.BlockSpec((1, 1, 3 * D_MODEL), lambda i: (i // tiles_per_mod, 0, 0)),
            _layer_block((D_MODEL, 3 * D_MODEL), layer),
            _layer_block((1, 3 * D_MODEL), layer),
            _layer_block((WIDTH, D_MODEL), layer),
            _layer_block((WIDTH, D_MODEL), layer),
            _layer_block((WIDTH, D_MODEL), layer),
            _layer_block((D_MODEL, D_MODEL), layer),
            _layer_block((1, D_MODEL), layer),
            _layer_block((1, D_MODEL), layer),
        ],
        out_specs=pl.BlockSpec((tm, D_MODEL), lambda i: (i, 0)),
        out_shape=jax.ShapeDtypeStruct((m, D_MODEL), F32),
        compiler_params=pltpu.CompilerParams(
            dimension_semantics=("arbitrary",), vmem_limit_bytes=VMEM_LIMIT),
        name=name,
    )(x, z, mod, wg, bg, wpa, wpb, wpc, wo, lg, lb)


def _rope_tables(n_tokens):
    rows = n_tokens // GRID_W
    row = jnp.repeat(jnp.arange(rows, dtype=F32), GRID_W)
    col = jnp.tile(jnp.arange(GRID_W, dtype=F32), rows)
    inv_freq = jnp.power(ROPE_BASE, -jnp.arange(ROPE_FREQS, dtype=F32) / ROPE_FREQS)
    ang = jnp.concatenate([inv_freq[:, None] * row[None], inv_freq[:, None] * col[None]], axis=0)
    return jnp.cos(ang), jnp.sin(ang)


def _transposed_cache(t):
    b, depth, past = t.shape[:3]
    return jnp.transpose(t, (0, 1, 3, 4, 2)).reshape(b, depth, KV_WIDTH, past)


def kernel(x_prompt, x_sample, cache_a_k, cache_a_v, cache_b_k, cache_b_v, c, c_ctx, w_mod, b_mod,
           w_in, q_norm, k_norm, sink, sgu_g, sgu_b, w_spatial, b_spatial, w_proj_a, w_proj_b,
           w_proj_c, w_gate, b_gate, w_out, ln_g, ln_b):
    batch, seq, _ = x_prompt.shape
    dec_batch, dec_seq, _ = x_sample.shape
    tm = 512

    cvecs = jnp.concatenate(
        [c_ctx[None], c, jnp.zeros((MOD_ROWS - 1 - dec_batch, D_MODEL), F32)], axis=0)
    mod = _modulation(cvecs, w_mod, b_mod)

    rope_tables = _rope_tables(dec_seq)
    caches = [_transposed_cache(t) for t in (cache_a_k, cache_a_v, cache_b_k, cache_b_v)]

    w_in_bf = w_in.astype(BF16)
    wt_bf = jnp.swapaxes(lax.optimization_barrier(jnp.concatenate(
        [w_in[:, :, lo:lo + width] for lo, width in _T_SEGMENTS], axis=-1)), 1, 2).astype(BF16)
    wg_bf, wo_bf, ws_bf = w_gate.astype(BF16), w_out.astype(BF16), w_spatial.astype(BF16)
    wpa_bf, wpb_bf, wpc_bf = w_proj_a.astype(BF16), w_proj_b.astype(BF16), w_proj_c.astype(BF16)
    bg3, lg3, lb3 = b_gate[:, None], ln_g[:, None], ln_b[:, None]

    xp = x_prompt.reshape(batch * seq, D_MODEL)
    xs = x_sample.reshape(dec_batch * dec_seq, D_MODEL)
    states = []
    for l in range(DEPTH):
        bs = jnp.repeat(b_spatial[l].T, LANES, axis=1)
        qn = jnp.broadcast_to(q_norm[l][:, None], (HEAD_DIM, tm))
        kn = jnp.broadcast_to(k_norm[l][:, None], (HEAD_DIM, tm))
        sg, sb = sgu_g[l][None], sgu_b[l][None]
        mod_ctx = mod[l, 0:1][None]
        mod_lat = mod[l, 1:1 + dec_batch][:, None]

        *acts, st_ak, st_av, st_bk, st_bv = _front(
            xp, mod_ctx, w_in_bf, wt_bf, qn, kn, sg, sb, None, layer=l, tm=tm, seq=seq, with_state=True)
        states.append((st_ak, st_av, st_bk, st_bv))
        z = _mixer_context(*acts, sink[l], ws_bf, bs, layer=l, seq=seq)
        xp = _back(xp, z, mod_ctx, wg_bf, bg3, wpa_bf, wpb_bf, wpc_bf, wo_bf, lg3, lb3,
                   layer=l, tm=2 * tm, rows_per_mod=batch * seq, name="back_context")

        acts = _front(xs, mod_lat, w_in_bf, wt_bf, qn, kn, sg, sb, rope_tables,
                      layer=l, tm=tm, seq=dec_seq, with_state=False)
        z = _mixer_latent(*acts, *caches, sink[l], ws_bf, bs, layer=l, seq=dec_seq, tq=256)
        xs = _back(xs, z, mod_lat, wg_bf, bg3, wpa_bf, wpb_bf, wpc_bf, wo_bf, lg3, lb3,
                   layer=l, tm=2 * tm, rows_per_mod=dec_seq, name="back_latent")

    def state(kind):
        t = jnp.stack([st[kind] for st in states], axis=1)
        return jnp.transpose(t.reshape(batch, DEPTH, N_KV, HEAD_DIM, seq), (0, 1, 4, 2, 3))

    return (xp.reshape(batch, seq, D_MODEL), xs.reshape(dec_batch, dec_seq, D_MODEL),
            state(0), state(1), state(2), state(3))
```

```python
import functools

import jax
import jax.numpy as jnp
from jax import lax
from jax.experimental import pallas as pl
from jax.experimental.pallas import tpu as pltpu

D_MODEL = 1024
DEPTH = 4
GRID_W = 64
HEAD_DIM = 64
HALF = HEAD_DIM // 2
N_HEADS = 8
N_KV = 2
GROUP = N_HEADS // N_KV
KV_WIDTH = N_KV * HEAD_DIM
WIDTH = 512
C_GROUPS = 4
CHUNK = 128
WINDOW = 128
IN_WIDTH = 4096
ROPE_BASE = 10000.0
ROPE_FREQS = HEAD_DIM // 4
EPS = 1e-6
NEG_INF = -1e30
DEEPNORM_ALPHA = (2 * DEPTH) ** 0.25
LOG2E = 1.4426950408889634
Q_SCALE = HEAD_DIM ** -0.5 * LOG2E

LANES = 128
BF16_SUBLANES = 16
MOD_ROWS = 8

_QA, _KA, _VA, _GA = 0, 512, 640, 768
_QB, _KB, _VB, _GB = 1280, 1792, 1920, 2048
_UC, _VC, _GC = 2560, 3072, 3584
_T_SEGMENTS = ((_QA, WIDTH), (_KA, KV_WIDTH), (_VA, KV_WIDTH), (_QB, WIDTH), (_KB, KV_WIDTH), (_VB, KV_WIDTH))
T_QA, T_KA, T_VA, T_QB, T_KB, T_VB = 0, 512, 640, 768, 1280, 1408
T_ROWS = 1536

(PZ_GA, PZ_GB, PZ_UC, PZ_VN, PZ_GC) = range(5)
PZ_WIDTH = 5 * WIDTH

BF16 = jnp.bfloat16
F32 = jnp.float32

VMEM_LIMIT = 56 * 1024 * 1024
SHIFT_MARGIN = 1.02
MAX_SAFE_SHIFT = 32.0
STAT_ROWS = 8
STAT_Q = 2 * N_KV
BACK_SUB_TILES = 4
BACK_STAGE_LAG = 4


def _silu(t):
    return t * jax.nn.sigmoid(t)


def _dot(a, b):
    return jnp.dot(a, b, preferred_element_type=F32)


def _dot_nt(a, b):
    return lax.dot_general(a, b, (((1,), (1,)), ((), ())), preferred_element_type=F32)


def _layer_block(shape, layer):
    zeros = (0,) * len(shape)
    return pl.BlockSpec((None,) + tuple(shape), lambda *_: (layer,) + zeros,
                        pipeline_mode=pl.Buffered(1))


def _mod_kernel(c_ref, w_ref, b_ref, o_ref):
    s = _silu(c_ref[...]).astype(BF16)
    o_ref[0] = _dot(s, w_ref[0].astype(BF16)) + b_ref[0]


def _modulation(cvecs, w_mod, b_mod):
    tn = 1024
    return pl.pallas_call(
        _mod_kernel,
        grid=(DEPTH, 3 * D_MODEL // tn),
        in_specs=[
            pl.BlockSpec((MOD_ROWS, D_MODEL), lambda l, j: (0, 0)),
            pl.BlockSpec((1, D_MODEL, tn), lambda l, j: (l, 0, j)),
            pl.BlockSpec((1, 1, tn), lambda l, j: (l, 0, j)),
        ],
        out_specs=pl.BlockSpec((1, MOD_ROWS, tn), lambda l, j: (l, 0, j)),
        out_shape=jax.ShapeDtypeStruct((DEPTH, MOD_ROWS, 3 * D_MODEL), F32),
        compiler_params=pltpu.CompilerParams(
            dimension_semantics=("arbitrary", "arbitrary"), vmem_limit_bytes=VMEM_LIMIT),
        name="modulation",
    )(cvecs, w_mod, b_mod.reshape(DEPTH, 1, 3 * D_MODEL))


def _modulated_norm(x, mod):
    mu = jnp.mean(x, axis=-1, keepdims=True)
    xc = x - mu
    var = jnp.mean(xc * xc, axis=-1, keepdims=True)
    xn = xc * lax.rsqrt(var + EPS)
    return xn * (1.0 + mod[:, D_MODEL:2 * D_MODEL]) + mod[:, :D_MODEL]


def _write_token_blocks(ref, t):
    w = ref.shape[2]
    for j in range(ref.shape[0]):
        ref[j] = t[:, j * w:(j + 1) * w].astype(ref.dtype)


def _front_kernel(*refs, rope, with_state):
    refs = list(refs)
    x_ref, mod_ref, w_ref, wt_ref, qn_ref, kn_ref, sg_ref, sb_ref = refs[:8]
    refs = refs[8:]
    if rope:
        cos_ref, sin_ref = refs[:2]
        refs = refs[2:]
    qat_ref, qbt_ref, ka_ref, kb_ref, vat_ref, vbt_ref, pz_ref, stat_ref = refs[:8]
    st_refs = refs[8:] if with_state else None

    h = _modulated_norm(x_ref[...], mod_ref[0]).astype(BF16)

    def rotary_t(t):
        if not rope:
            return t
        x1, x2 = t[:HALF], t[HALF:]
        c, s = cos_ref[...], sin_ref[...]
        return jnp.concatenate([x1 * c - x2 * s, x2 * c + x1 * s], axis=0)

    def rms_t(t, gain):
        ms = jnp.mean(t * t, axis=0, keepdims=True)
        return t * lax.rsqrt(ms + EPS) * gain

    pt = _dot_nt(wt_ref[...], h)

    def heads_t(row0, n):
        return [pt[row0 + j * HEAD_DIM: row0 + (j + 1) * HEAD_DIM] for j in range(n)]

    qn, kn = qn_ref[...], kn_ref[...]
    qa = [rotary_t(rms_t(t, qn)) * Q_SCALE for t in heads_t(T_QA, N_HEADS)]
    qat_ref[...] = jnp.concatenate(qa, axis=0).astype(BF16)
    qb = [rotary_t(t) * Q_SCALE for t in heads_t(T_QB, N_HEADS)]
    qbt_ref[...] = jnp.concatenate(qb, axis=0).astype(BF16)

    ka = [rms_t(t, kn) for t in heads_t(T_KA, N_KV)]
    kb = heads_t(T_KB, N_KV)
    ka_ref[...] = jnp.concatenate([rotary_t(t) for t in ka], axis=0).T.astype(BF16)
    kb_ref[...] = jnp.concatenate([rotary_t(t) for t in kb], axis=0).T.astype(BF16)
    def max_sq_norm(heads):
        m = None
        for t in heads:
            r = jnp.sum(t * t, axis=0, keepdims=True)
            m = r if m is None else jnp.maximum(m, r)
        return jnp.max(m, axis=1, keepdims=True)

    stats = [max_sq_norm([t]) for t in ka + kb] + [max_sq_norm(qa), max_sq_norm(qb)]
    for r, v in enumerate(stats):
        stat_ref[r:r + 1, :] = jnp.broadcast_to(v, (1, LANES))
    stat_ref[len(stats):, :] = jnp.zeros((STAT_ROWS - len(stats), LANES), F32)
    va = pt[T_VA:T_VA + KV_WIDTH]
    vb = pt[T_VB:T_VB + KV_WIDTH]
    vat_ref[...] = va.astype(BF16)
    if len(vbt_ref.shape) == 3:
        _write_token_blocks(vbt_ref, vb)
    else:
        vbt_ref[...] = vb.astype(BF16)

    if with_state:
        _write_token_blocks(st_refs[0], jnp.concatenate(ka, axis=0))
        _write_token_blocks(st_refs[1], va)
        _write_token_blocks(st_refs[2], jnp.concatenate(kb, axis=0))
        _write_token_blocks(st_refs[3], vb)

    def proj(lo):
        return _dot(h, w_ref[:, lo:lo + WIDTH])

    def store(block, val):
        pz_ref[:, block * WIDTH:(block + 1) * WIDTH] = val.astype(BF16)

    store(PZ_GA, _silu(proj(_GA)))
    store(PZ_GB, _silu(proj(_GB)))
    store(PZ_UC, proj(_UC))
    vc = proj(_VC)
    mu = jnp.mean(vc, axis=-1, keepdims=True)
    vcc = vc - mu
    var = jnp.mean(vcc * vcc, axis=-1, keepdims=True)
    store(PZ_VN, vcc * lax.rsqrt(var + EPS) * sg_ref[...] + sb_ref[...])
    store(PZ_GC, _silu(proj(_GC)))


def _front(x, mod, w_in, wt, qn, kn, sg, sb, rope_tables, *, layer, tm, seq, with_state):
    m = x.shape[0]
    n_seq = m // seq
    rope = rope_tables is not None
    const = lambda i: (0, 0)
    tps = max(seq // tm, 1)
    mod_idx = (lambda i: (0, 0, 0)) if tm >= seq else (lambda i: (i // tps, 0, 0))
    in_specs = [
        pl.BlockSpec((tm, D_MODEL), lambda i: (i, 0)),
        pl.BlockSpec((1, 1, 3 * D_MODEL), mod_idx),
        _layer_block((D_MODEL, IN_WIDTH), layer),
        _layer_block((T_ROWS, D_MODEL), layer),
        pl.BlockSpec((HEAD_DIM, tm), const),
        pl.BlockSpec((HEAD_DIM, tm), const),
        pl.BlockSpec((1, WIDTH), const),
        pl.BlockSpec((1, WIDTH), const),
    ]
    args = [x, mod, w_in, wt, qn, kn, sg, sb]
    if rope:
        in_specs += [pl.BlockSpec((HALF, tm), lambda i: (0, i % tps))] * 2
        args += list(rope_tables)
    qt_spec = pl.BlockSpec((WIDTH, tm), lambda i: (0, i))
    k_spec = pl.BlockSpec((tm, KV_WIDTH), lambda i: (i, 0))
    vt_shape = (KV_WIDTH, m)
    vt_spec = pl.BlockSpec((KV_WIDTH, tm), lambda i: (0, i))
    if rope:
        vbt_shape = (m // LANES, KV_WIDTH, LANES)
        vbt_spec = pl.BlockSpec((tm // LANES, KV_WIDTH, LANES), lambda i: (i, 0, 0))
    else:
        vbt_shape, vbt_spec = vt_shape, vt_spec
    out_specs = [qt_spec, qt_spec, k_spec, k_spec, vt_spec, vbt_spec,
                 pl.BlockSpec((tm, PZ_WIDTH), lambda i: (i, 0)),
                 pl.BlockSpec((STAT_ROWS, LANES), lambda i: (i, 0))]
    out_shape = [jax.ShapeDtypeStruct((WIDTH, m), BF16)] * 2
    out_shape += [jax.ShapeDtypeStruct((m, KV_WIDTH), BF16)] * 2
    out_shape += [jax.ShapeDtypeStruct(vt_shape, BF16), jax.ShapeDtypeStruct(vbt_shape, BF16)]
    out_shape += [jax.ShapeDtypeStruct((m, PZ_WIDTH), BF16), jax.ShapeDtypeStruct((m // tm * STAT_ROWS, LANES), F32)]
    if with_state:
        out_specs += [pl.BlockSpec((tm // seq, KV_WIDTH, seq), lambda i: (i, 0, 0))] * 4
        out_shape += [jax.ShapeDtypeStruct((n_seq, KV_WIDTH, seq), F32)] * 4
    return pl.pallas_call(
        functools.partial(_front_kernel, rope=rope, with_state=with_state),
        grid=(m // tm,),
        in_specs=in_specs,
        out_specs=out_specs,
        out_shape=out_shape,
        compiler_params=pltpu.CompilerParams(
            dimension_semantics=("arbitrary",), vmem_limit_bytes=VMEM_LIMIT),
        name="front_latent" if rope else "front_context",
    )(*args)


def _scores_t(q_heads_t, sources, kv_head):
    top = jnp.concatenate(q_heads_t, axis=1) if len(q_heads_t) > 1 else q_heads_t[0]
    zeros = jnp.zeros_like(top)
    q_aug = jnp.concatenate([top, zeros] if kv_head == 0 else [zeros, top], axis=0)
    scores = []
    for k, _, mask in sources:
        s = _dot(k(), q_aug)
        if mask is not None:
            s = jnp.where(mask(), s, NEG_INF)
        scores.append(s)
    return scores


def _softmax_pv_t(scores, sources, sinks, kv_head, tq, shift=None):
    m = shift
    if m is None:
        for s in scores:
            ms = jnp.max(s, axis=0, keepdims=True)
            m = ms if m is None else jnp.maximum(m, ms)
    if sinks is not None:
        sink = jnp.concatenate([jnp.full((1, tq), sk * LOG2E, F32) for sk in sinks], axis=1)
        m = jnp.maximum(m, sink)
    rows = slice(kv_head * HEAD_DIM, (kv_head + 1) * HEAD_DIM)
    r = None
    for s, src in zip(scores, sources):
        v_t = src[1](rows)
        v_aug = jnp.concatenate([v_t, jnp.ones((BF16_SUBLANES, v_t.shape[1]), BF16)], axis=0)
        pv = _dot(v_aug, jnp.exp2(s - m).astype(BF16))
        r = pv if r is None else r + pv
    denom = r[HEAD_DIM:HEAD_DIM + 1]
    if sinks is not None:
        denom = denom + jnp.exp2(sink - m)
    return r[:HEAD_DIM] * (1.0 / denom)


def _attention_branches(branches, pz_ref, z_ref, heads_per_dot, tail):
    tq = pz_ref.shape[0]
    items = [(br, kv_head, h0) for br in range(len(branches)) for kv_head in range(N_KV)
             for h0 in range(kv_head * GROUP, (kv_head + 1) * GROUP, heads_per_dot)]

    def q_heads_of(item):
        br, _, h0 = item
        qt_ref = branches[br][0]
        return [qt_ref[h * HEAD_DIM:(h + 1) * HEAD_DIM, :] for h in range(h0, h0 + heads_per_dot)]

    def scores_of(item):
        return _scores_t(q_heads_of(item), branches[item[0]][1], item[1])

    shifts = [SHIFT_MARGIN * jnp.sqrt(jnp.full((1, heads_per_dot * tq), branches[br][5][kv_head], F32))
              for br, kv_head, _ in items]
    worst = None
    for branch in branches:
        for sq in branch[5]:
            worst = sq if worst is None else jnp.maximum(worst, sq)
    bounded = worst * SHIFT_MARGIN ** 2 <= MAX_SAFE_SHIFT ** 2

    def run(use_shift):
        pending = scores_of(items[0])
        for i, (br, kv_head, h0) in enumerate(items):
            scores = pending
            if i + 1 < len(items):
                pending = scores_of(items[i + 1])
            _, sources, sink_ref, gate_block, z_block, _ = branches[br]
            sinks = None if sink_ref is None else [sink_ref[h] for h in range(h0, h0 + heads_per_dot)]
            o_t = _softmax_pv_t(scores, sources, sinks, kv_head, tq, shifts[i] if use_shift else None)
            for j in range(0, heads_per_dot, 2):
                pair = (h0 + j) // 2
                o = o_t[:, j * tq:(j + 2) * tq]
                o = jnp.concatenate([o[:, :tq], o[:, tq:]], axis=0).T
                gate = pz_ref[:, gate_block * WIDTH + pair * LANES: gate_block * WIDTH + (pair + 1) * LANES]
                z_ref[:, z_block * WIDTH + pair * LANES: z_block * WIDTH + (pair + 1) * LANES] = (
                    o * gate.astype(F32)).astype(BF16)
        tail()

    pl.when(bounded)(functools.partial(run, True))
    pl.when(jnp.logical_not(bounded))(functools.partial(run, False))


def _spatial_gate(pz_ref, ws_ref, bs_ref, z_ref, n_rows):
    for ch in range(n_rows // CHUNK):
        rows = slice(ch * CHUNK, (ch + 1) * CHUNK)
        for g in range(C_GROUPS):
            cols = lambda blk: slice(blk * WIDTH + g * LANES, blk * WIDTH + (g + 1) * LANES)
            mixed = _dot(ws_ref[g], pz_ref[rows, cols(PZ_VN)]) + bs_ref[:, g * LANES:(g + 1) * LANES]
            zc = pz_ref[rows, cols(PZ_UC)].astype(F32) * mixed * pz_ref[rows, cols(PZ_GC)].astype(F32)
            z_ref[rows, 2 * WIDTH + g * LANES: 2 * WIDTH + (g + 1) * LANES] = zc.astype(BF16)


def _ref_source(k_ref, vt_ref):
    return (lambda: k_ref[...], lambda rows: vt_ref[rows, :], None)


def _mixer_context_kernel(qat_ref, qbt_ref, ka_ref, kb_ref, vat_ref, vbt_ref, pz_ref, stat_ref,
                          sink_ref, ws_ref, bs_ref, z_ref, *, tokens_per_stat):
    t = lax.div(pl.program_id(0) * pz_ref.shape[0], tokens_per_stat)
    sq = lambda br: [stat_ref[t, br * N_KV + g] * stat_ref[t, STAT_Q + br] for g in range(N_KV)]
    _attention_branches(
        [(qat_ref, [_ref_source(ka_ref, vat_ref)], None, PZ_GA, 0, sq(0)),
         (qbt_ref, [_ref_source(kb_ref, vbt_ref)], sink_ref, PZ_GB, 1, sq(1))],
        pz_ref, z_ref, GROUP,
        functools.partial(_spatial_gate, pz_ref, ws_ref, bs_ref, z_ref, pz_ref.shape[0]))


def _mixer_context(qat, qbt, ka, kb, vat, vbt, pz, stats, sink, ws, bs, *, layer, seq, tokens_per_stat):
    m = pz.shape[0]
    qt_spec = pl.BlockSpec((WIDTH, seq), lambda i: (0, i))
    k_spec = pl.BlockSpec((seq, KV_WIDTH), lambda i: (i, 0))
    vt_spec = pl.BlockSpec((KV_WIDTH, seq), lambda i: (0, i))
    return pl.pallas_call(
        functools.partial(_mixer_context_kernel, tokens_per_stat=tokens_per_stat),
        grid=(m // seq,),
        in_specs=[
            qt_spec, qt_spec, k_spec, k_spec, vt_spec, vt_spec,
            pl.BlockSpec((seq, PZ_WIDTH), lambda i: (i, 0)),
            pl.BlockSpec(memory_space=pltpu.SMEM),
            pl.BlockSpec(memory_space=pltpu.SMEM),
            _layer_block((C_GROUPS, CHUNK, CHUNK), layer),
            pl.BlockSpec((CHUNK, WIDTH), lambda i: (0, 0)),
        ],
        out_specs=pl.BlockSpec((seq, 3 * WIDTH), lambda i: (i, 0)),
        out_shape=jax.ShapeDtypeStruct((m, 3 * WIDTH), BF16),
        compiler_params=pltpu.CompilerParams(
            dimension_semantics=("arbitrary",), vmem_limit_bytes=VMEM_LIMIT),
        name="mixer_context",
    )(qat, qbt, ka, kb, vat, vbt, pz, stats, sink, ws, bs)


def _mixer_latent_kernel(qat_ref, qbt_ref, ka_ref, kb_ref, vat_ref, vbt_ref, pz_ref, stat_ref,
                         cak_ref, cav_ref, cbk_ref, cbv_ref, sink_ref, ws_ref, bs_ref,
                         z_ref, cak_s, cav_s, cbk_s, cbv_s, ck2_s, *, seq, heads_per_dot, tokens_per_stat):
    tq = pz_ref.shape[0]
    band = tq + 2 * WINDOW
    qi = pl.program_id(1)

    @pl.when(qi == 0)
    def _():
        cak_s[...] = cak_ref[0, 0].T.astype(BF16)
        cbk_s[...] = cbk_ref[0, 0].T.astype(BF16)
        cav_s[...] = cav_ref[0, 0].astype(BF16)
        cbv_s[...] = cbv_ref[0, 0].astype(BF16)
        for br, cache_ref in enumerate((cak_ref, cbk_ref)):
            ck = cache_ref[0, 0]
            for g in range(N_KV):
                c2 = jnp.sum(jnp.square(ck[g * HEAD_DIM:(g + 1) * HEAD_DIM]), axis=0, keepdims=True)
                ck2_s[br * N_KV + g] = jnp.max(c2)

    start_blk = jnp.clip(qi * (tq // LANES) - WINDOW // LANES, 0, (seq - band) // LANES)
    start = pl.multiple_of(start_blk * LANES, LANES)

    def in_window():
        k_pos = start + lax.broadcasted_iota(jnp.int32, (band, heads_per_dot * tq), 0)
        q_pos = qi * tq + (lax.broadcasted_iota(jnp.int32, (band, heads_per_dot * tq), 1) & (tq - 1))
        return jnp.abs(q_pos - k_pos) <= WINDOW

    window = (lambda: kb_ref[pl.ds(start, band), :],
              lambda rows: jnp.concatenate(
                  [vbt_ref[start_blk + j, rows, :] for j in range(band // LANES)], axis=1),
              in_window)

    def sq(br):
        b = pl.program_id(0)
        tps = seq // tokens_per_stat
        q2 = stat_ref[b * tps + lax.div(qi * tq, tokens_per_stat), STAT_Q + br]
        out = []
        for g in range(N_KV):
            k2 = ck2_s[br * N_KV + g]
            for j in range(tps):
                k2 = jnp.maximum(k2, stat_ref[b * tps + j, br * N_KV + g])
            out.append(q2 * k2)
        return out

    _attention_branches(
        [(qat_ref, [_ref_source(ka_ref, vat_ref), _ref_source(cak_s, cav_s)], None, PZ_GA, 0,
          sq(0)),
         (qbt_ref, [window, _ref_source(cbk_s, cbv_s)],
          sink_ref, PZ_GB, 1, sq(1))],
        pz_ref, z_ref, heads_per_dot,
        functools.partial(_spatial_gate, pz_ref, ws_ref, bs_ref, z_ref, tq))


def _mixer_latent(qat, qbt, ka, kb, vat, vbt, pz, stats, cak, cav, cbk, cbv, sink, ws, bs, *, layer, seq, tq,
                  tokens_per_stat):
    m = pz.shape[0]
    n_seq = m // seq
    nq = seq // tq
    past = cak.shape[3]
    qt_spec = pl.BlockSpec((WIDTH, tq), lambda b, q: (0, b * nq + q))
    k_spec = pl.BlockSpec((seq, KV_WIDTH), lambda b, q: (b, 0))
    cache_spec = pl.BlockSpec((1, 1, KV_WIDTH, past), lambda b, q: (b, layer, 0, 0))
    return pl.pallas_call(
        functools.partial(_mixer_latent_kernel, seq=seq, heads_per_dot=2, tokens_per_stat=tokens_per_stat),
        grid=(n_seq, nq),
        in_specs=[
            qt_spec, qt_spec, k_spec, k_spec,
            pl.BlockSpec((KV_WIDTH, seq), lambda b, q: (0, b)),
            pl.BlockSpec((seq // LANES, KV_WIDTH, LANES), lambda b, q: (b, 0, 0)),
            pl.BlockSpec((tq, PZ_WIDTH), lambda b, q: (b * nq + q, 0)),
            pl.BlockSpec(memory_space=pltpu.SMEM),
            cache_spec, cache_spec, cache_spec, cache_spec,
            pl.BlockSpec(memory_space=pltpu.SMEM),
            _layer_block((C_GROUPS, CHUNK, CHUNK), layer),
            pl.BlockSpec((CHUNK, WIDTH), lambda b, q: (0, 0)),
        ],
        out_specs=pl.BlockSpec((tq, 3 * WIDTH), lambda b, q: (b * nq + q, 0)),
        out_shape=jax.ShapeDtypeStruct((m, 3 * WIDTH), BF16),
        scratch_shapes=[pltpu.VMEM((past, KV_WIDTH), BF16), pltpu.VMEM((KV_WIDTH, past), BF16),
                        pltpu.VMEM((past, KV_WIDTH), BF16), pltpu.VMEM((KV_WIDTH, past), BF16),
                        pltpu.SMEM((2 * N_KV,), F32)],
        compiler_params=pltpu.CompilerParams(
            dimension_semantics=("arbitrary", "arbitrary"), vmem_limit_bytes=VMEM_LIMIT),
        name="mixer_latent",
    )(qat, qbt, ka, kb, vat, vbt, pz, stats, cak, cav, cbk, cbv, sink, ws, bs)


def _back_kernel(x_ref, z_ref, mod_ref, wg_ref, bg_ref, wpa_ref, wpb_ref, wpc_ref, wo_ref,
                 lg_ref, lb_ref, o_ref, *, n_sub):
    mod = mod_ref[0]
    sub = x_ref.shape[0] // n_sub
    wp_refs = (wpa_ref, wpb_ref, wpc_ref)

    def stages(s):
        rows = slice(s * sub, (s + 1) * sub)
        st = {}

        def norm():
            st["h"] = _modulated_norm(x_ref[rows, :], mod).astype(BF16)

        def branch(br):
            cols = slice(br * D_MODEL, (br + 1) * D_MODEL)
            g = jax.nn.sigmoid(_dot(st["h"], wg_ref[:, cols]) + bg_ref[:, cols])
            y = _dot(z_ref[rows, br * WIDTH:(br + 1) * WIDTH], wp_refs[br][...])
            st["mix"] = g * y if br == 0 else st["mix"] + g * y

        def out_proj():
            st["y"] = _dot(st["mix"].astype(BF16), wo_ref[...])

        def residual_norm():
            r = DEEPNORM_ALPHA * x_ref[rows, :] + mod[:, 2 * D_MODEL:] * st["y"]
            mu = jnp.mean(r, axis=-1, keepdims=True)
            rc = r - mu
            var = jnp.mean(rc * rc, axis=-1, keepdims=True)
            o_ref[rows, :] = rc * lax.rsqrt(var + EPS) * lg_ref[...] + lb_ref[...]

        return [norm] + [functools.partial(branch, br) for br in range(3)] + [out_proj, residual_norm]

    chains = [stages(s) for s in range(n_sub)]
    n_stage = len(chains[0])
    for t in range(n_stage + (n_sub - 1) * BACK_STAGE_LAG):
        for s in range(n_sub):
            k = t - s * BACK_STAGE_LAG
            if 0 <= k < n_stage:
                chains[s][k]()


def _back(x, z, mod, wg, bg, wpa, wpb, wpc, wo, lg, lb, *, layer, tm, rows_per_mod, name):
    m = x.shape[0]
    tiles_per_mod = rows_per_mod // tm
    return pl.pallas_call(
        functools.partial(_back_kernel, n_sub=BACK_SUB_TILES),
        grid=(m // tm,),
        in_specs=[
            pl.BlockSpec((tm, D_MODEL), lambda i: (i, 0)),
            pl.BlockSpec((tm, 3 * WIDTH), lambda i: (i, 0)),
            pl.BlockSpec((1, 1, 3 * D_MODEL), lambda i: (i // tiles_per_mod, 0, 0)),
            _layer_block((D_MODEL, 3 * D_MODEL), layer),
            _layer_block((1, 3 * D_MODEL), layer),
            _layer_block((WIDTH, D_MODEL), layer),
            _layer_block((WIDTH, D_MODEL), layer),
            _layer_block((WIDTH, D_MODEL), layer),
            _layer_block((D_MODEL, D_MODEL), layer),
            _layer_block((1, D_MODEL), layer),
            _layer_block((1, D_MODEL), layer),
        ],
        out_specs=pl.BlockSpec((tm, D_MODEL), lambda i: (i, 0)),
        out_shape=jax.ShapeDtypeStruct((m, D_MODEL), F32),
        compiler_params=pltpu.CompilerParams(
            dimension_semantics=("arbitrary",), vmem_limit_bytes=VMEM_LIMIT),
        name=name,
    )(x, z, mod, wg, bg, wpa, wpb, wpc, wo, lg, lb)


def _rope_tables(n_tokens):
    rows = n_tokens // GRID_W
    row = jnp.repeat(jnp.arange(rows, dtype=F32), GRID_W)
    col = jnp.tile(jnp.arange(GRID_W, dtype=F32), rows)
    inv_freq = jnp.power(ROPE_BASE, -jnp.arange(ROPE_FREQS, dtype=F32) / ROPE_FREQS)
    ang = jnp.concatenate([inv_freq[:, None] * row[None], inv_freq[:, None] * col[None]], axis=0)
    return jnp.cos(ang), jnp.sin(ang)


def _tile_stats(stats):
    return stats.reshape(-1, STAT_ROWS, LANES)[:, :, 0]


def _transposed_cache(t):
    b, depth, past = t.shape[:3]
    return jnp.transpose(t, (0, 1, 3, 4, 2)).reshape(b, depth, KV_WIDTH, past)


def kernel(x_prompt, x_sample, cache_a_k, cache_a_v, cache_b_k, cache_b_v, c, c_ctx, w_mod, b_mod,
           w_in, q_norm, k_norm, sink, sgu_g, sgu_b, w_spatial, b_spatial, w_proj_a, w_proj_b,
           w_proj_c, w_gate, b_gate, w_out, ln_g, ln_b):
    batch, seq, _ = x_prompt.shape
    dec_batch, dec_seq, _ = x_sample.shape
    tm = 512

    cvecs = jnp.concatenate(
        [c_ctx[None], c, jnp.zeros((MOD_ROWS - 1 - dec_batch, D_MODEL), F32)], axis=0)
    mod = _modulation(cvecs, w_mod, b_mod)

    rope_tables = _rope_tables(dec_seq)
    caches = [_transposed_cache(t) for t in (cache_a_k, cache_a_v, cache_b_k, cache_b_v)]

    w_in_bf = w_in.astype(BF16)
    wt_bf = jnp.swapaxes(lax.optimization_barrier(jnp.concatenate(
        [w_in[:, :, lo:lo + width] for lo, width in _T_SEGMENTS], axis=-1)), 1, 2).astype(BF16)
    wg_bf, wo_bf, ws_bf = w_gate.astype(BF16), w_out.astype(BF16), w_spatial.astype(BF16)
    wpa_bf, wpb_bf, wpc_bf = w_proj_a.astype(BF16), w_proj_b.astype(BF16), w_proj_c.astype(BF16)
    bg3, lg3, lb3 = b_gate[:, None], ln_g[:, None], ln_b[:, None]

    xp = x_prompt.reshape(batch * seq, D_MODEL)
    xs = x_sample.reshape(dec_batch * dec_seq, D_MODEL)
    states = []
    for l in range(DEPTH):
        bs = jnp.repeat(b_spatial[l].T, LANES, axis=1)
        qn = jnp.broadcast_to(q_norm[l][:, None], (HEAD_DIM, tm))
        kn = jnp.broadcast_to(k_norm[l][:, None], (HEAD_DIM, tm))
        sg, sb = sgu_g[l][None], sgu_b[l][None]
        mod_ctx = mod[l, 0:1][None]
        mod_lat = mod[l, 1:1 + dec_batch][:, None]

        *acts, st_ak, st_av, st_bk, st_bv = _front(
            xp, mod_ctx, w_in_bf, wt_bf, qn, kn, sg, sb, None, layer=l, tm=tm, seq=seq, with_state=True)
        states.append((st_ak, st_av, st_bk, st_bv))
        z = _mixer_context(*acts[:-1], _tile_stats(acts[-1]), sink[l], ws_bf, bs,
                           layer=l, seq=seq, tokens_per_stat=tm)
        xp = _back(xp, z, mod_ctx, wg_bf, bg3, wpa_bf, wpb_bf, wpc_bf, wo_bf, lg3, lb3,
                   layer=l, tm=2 * tm, rows_per_mod=batch * seq, name="back_context")

        acts = _front(xs, mod_lat, w_in_bf, wt_bf, qn, kn, sg, sb, rope_tables,
                      layer=l, tm=tm, seq=dec_seq, with_state=False)
        z = _mixer_latent(*acts[:-1], _tile_stats(acts[-1]), *caches, sink[l], ws_bf, bs,
                          layer=l, seq=dec_seq, tq=256, tokens_per_stat=tm)
        xs = _back(xs, z, mod_lat, wg_bf, bg3, wpa_bf, wpb_bf, wpc_bf, wo_bf, lg3, lb3,
                   layer=l, tm=2 * tm, rows_per_mod=dec_seq, name="back_latent")

    def state(kind):
        t = jnp.stack([st[kind] for st in states], axis=1)
        return jnp.transpose(t.reshape(batch, DEPTH, N_KV, HEAD_DIM, seq), (0, 1, 4, 2, 3))

    return (xp.reshape(batch, seq, D_MODEL), xs.reshape(dec_batch, dec_seq, D_MODEL),
            state(0), state(1), state(2), state(3))
```

```python
import functools

import jax
import jax.numpy as jnp
from jax import lax
from jax.experimental import pallas as pl
from jax.experimental.pallas import tpu as pltpu

D_MODEL = 1024
DEPTH = 4
GRID_W = 64
HEAD_DIM = 64
HALF = HEAD_DIM // 2
N_HEADS = 8
N_KV = 2
GROUP = N_HEADS // N_KV
KV_WIDTH = N_KV * HEAD_DIM
WIDTH = 512
C_GROUPS = 4
CHUNK = 128
WINDOW = 128
IN_WIDTH = 4096
ROPE_BASE = 10000.0
ROPE_FREQS = HEAD_DIM // 4
EPS = 1e-6
NEG_INF = -1e30
DEEPNORM_ALPHA = (2 * DEPTH) ** 0.25
LOG2E = 1.4426950408889634
Q_SCALE = HEAD_DIM ** -0.5 * LOG2E

LANES = 128
BF16_SUBLANES = 16
MOD_ROWS = 8

_QA, _KA, _VA, _GA = 0, 512, 640, 768
_QB, _KB, _VB, _GB = 1280, 1792, 1920, 2048
_UC, _VC, _GC = 2560, 3072, 3584
_T_SEGMENTS = ((_QA, WIDTH), (_KA, KV_WIDTH), (_VA, KV_WIDTH), (_QB, WIDTH), (_KB, KV_WIDTH), (_VB, KV_WIDTH))
T_QA, T_KA, T_VA, T_QB, T_KB, T_VB = 0, 512, 640, 768, 1280, 1408
T_ROWS = 1536

(PZ_GA, PZ_GB, PZ_UC, PZ_VN, PZ_GC) = range(5)
PZ_WIDTH = 5 * WIDTH

BF16 = jnp.bfloat16
F32 = jnp.float32

VMEM_LIMIT = 56 * 1024 * 1024
SHIFT_MARGIN = 1.02
MAX_SAFE_SHIFT = 48.0
STAT_ROWS = 8
STAT_Q = 2 * N_KV
BACK_SUB_TILES = 4
BACK_STAGE_LAG = 4


def _silu(t):
    return t * jax.nn.sigmoid(t)


def _dot(a, b):
    return jnp.dot(a, b, preferred_element_type=F32)


def _dot_nt(a, b):
    return lax.dot_general(a, b, (((1,), (1,)), ((), ())), preferred_element_type=F32)


def _layer_block(shape, layer):
    zeros = (0,) * len(shape)
    return pl.BlockSpec((None,) + tuple(shape), lambda *_: (layer,) + zeros,
                        pipeline_mode=pl.Buffered(1))


def _mod_kernel(c_ref, w_ref, b_ref, o_ref):
    s = _silu(c_ref[...]).astype(BF16)
    o_ref[0] = _dot(s, w_ref[0].astype(BF16)) + b_ref[0]


def _modulation(cvecs, w_mod, b_mod):
    tn = 1024
    return pl.pallas_call(
        _mod_kernel,
        grid=(DEPTH, 3 * D_MODEL // tn),
        in_specs=[
            pl.BlockSpec((MOD_ROWS, D_MODEL), lambda l, j: (0, 0)),
            pl.BlockSpec((1, D_MODEL, tn), lambda l, j: (l, 0, j)),
            pl.BlockSpec((1, 1, tn), lambda l, j: (l, 0, j)),
        ],
        out_specs=pl.BlockSpec((1, MOD_ROWS, tn), lambda l, j: (l, 0, j)),
        out_shape=jax.ShapeDtypeStruct((DEPTH, MOD_ROWS, 3 * D_MODEL), F32),
        compiler_params=pltpu.CompilerParams(
            dimension_semantics=("arbitrary", "arbitrary"), vmem_limit_bytes=VMEM_LIMIT),
        name="modulation",
    )(cvecs, w_mod, b_mod.reshape(DEPTH, 1, 3 * D_MODEL))


def _modulated_norm(x, mod):
    mu = jnp.mean(x, axis=-1, keepdims=True)
    xc = x - mu
    var = jnp.mean(xc * xc, axis=-1, keepdims=True)
    xn = xc * lax.rsqrt(var + EPS)
    return xn * (1.0 + mod[:, D_MODEL:2 * D_MODEL]) + mod[:, :D_MODEL]


def _write_token_blocks(ref, t):
    w = ref.shape[2]
    for j in range(ref.shape[0]):
        ref[j] = t[:, j * w:(j + 1) * w].astype(ref.dtype)


def _front_kernel(*refs, rope, with_state):
    refs = list(refs)
    x_ref, mod_ref, w_ref, wt_ref, qn_ref, kn_ref, sg_ref, sb_ref = refs[:8]
    refs = refs[8:]
    if rope:
        cos_ref, sin_ref = refs[:2]
        refs = refs[2:]
    qat_ref, qbt_ref, ka_ref, kb_ref, vat_ref, vbt_ref, pz_ref, stat_ref = refs[:8]
    st_refs = refs[8:] if with_state else None

    h = _modulated_norm(x_ref[...], mod_ref[0]).astype(BF16)

    def rotary_t(t):
        if not rope:
            return t
        x1, x2 = t[:HALF], t[HALF:]
        c, s = cos_ref[...], sin_ref[...]
        return jnp.concatenate([x1 * c - x2 * s, x2 * c + x1 * s], axis=0)

    def rms_t(t, gain):
        ms = jnp.mean(t * t, axis=0, keepdims=True)
        return t * lax.rsqrt(ms + EPS) * gain

    pt = _dot_nt(wt_ref[...], h)

    def heads_t(row0, n):
        return [pt[row0 + j * HEAD_DIM: row0 + (j + 1) * HEAD_DIM] for j in range(n)]

    qn, kn = qn_ref[...], kn_ref[...]
    qa = [rotary_t(rms_t(t, qn)) * Q_SCALE for t in heads_t(T_QA, N_HEADS)]
    qat_ref[...] = jnp.concatenate(qa, axis=0).astype(BF16)
    qb = [rotary_t(t) * Q_SCALE for t in heads_t(T_QB, N_HEADS)]
    qbt_ref[...] = jnp.concatenate(qb, axis=0).astype(BF16)

    ka = [rms_t(t, kn) for t in heads_t(T_KA, N_KV)]
    kb = heads_t(T_KB, N_KV)
    ka_ref[...] = jnp.concatenate([rotary_t(t) for t in ka], axis=0).T.astype(BF16)
    kb_ref[...] = jnp.concatenate([rotary_t(t) for t in kb], axis=0).T.astype(BF16)
    def max_sq_norm(heads):
        m = None
        for t in heads:
            r = jnp.sum(t * t, axis=0, keepdims=True)
            m = r if m is None else jnp.maximum(m, r)
        return jnp.max(m, axis=1, keepdims=True)

    stats = [max_sq_norm([t]) for t in ka + kb] + [max_sq_norm(qa), max_sq_norm(qb)]
    for r, v in enumerate(stats):
        stat_ref[r:r + 1, :] = jnp.broadcast_to(v, (1, LANES))
    stat_ref[len(stats):, :] = jnp.zeros((STAT_ROWS - len(stats), LANES), F32)
    va = pt[T_VA:T_VA + KV_WIDTH]
    vb = pt[T_VB:T_VB + KV_WIDTH]
    vat_ref[...] = va.astype(BF16)
    if len(vbt_ref.shape) == 3:
        _write_token_blocks(vbt_ref, vb)
    else:
        vbt_ref[...] = vb.astype(BF16)

    if with_state:
        _write_token_blocks(st_refs[0], jnp.concatenate(ka, axis=0))
        _write_token_blocks(st_refs[1], va)
        _write_token_blocks(st_refs[2], jnp.concatenate(kb, axis=0))
        _write_token_blocks(st_refs[3], vb)

    def proj(lo):
        return _dot(h, w_ref[:, lo:lo + WIDTH])

    def store(block, val):
        pz_ref[:, block * WIDTH:(block + 1) * WIDTH] = val.astype(BF16)

    store(PZ_GA, _silu(proj(_GA)))
    store(PZ_GB, _silu(proj(_GB)))
    store(PZ_UC, proj(_UC))
    vc = proj(_VC)
    mu = jnp.mean(vc, axis=-1, keepdims=True)
    vcc = vc - mu
    var = jnp.mean(vcc * vcc, axis=-1, keepdims=True)
    store(PZ_VN, vcc * lax.rsqrt(var + EPS) * sg_ref[...] + sb_ref[...])
    store(PZ_GC, _silu(proj(_GC)))


def _front(x, mod, w_in, wt, qn, kn, sg, sb, rope_tables, *, layer, tm, seq, with_state):
    m = x.shape[0]
    n_seq = m // seq
    rope = rope_tables is not None
    const = lambda i: (0, 0)
    tps = max(seq // tm, 1)
    mod_idx = (lambda i: (0, 0, 0)) if tm >= seq else (lambda i: (i // tps, 0, 0))
    in_specs = [
        pl.BlockSpec((tm, D_MODEL), lambda i: (i, 0)),
        pl.BlockSpec((1, 1, 3 * D_MODEL), mod_idx),
        _layer_block((D_MODEL, IN_WIDTH), layer),
        _layer_block((T_ROWS, D_MODEL), layer),
        pl.BlockSpec((HEAD_DIM, tm), const),
        pl.BlockSpec((HEAD_DIM, tm), const),
        pl.BlockSpec((1, WIDTH), const),
        pl.BlockSpec((1, WIDTH), const),
    ]
    args = [x, mod, w_in, wt, qn, kn, sg, sb]
    if rope:
        in_specs += [pl.BlockSpec((HALF, tm), lambda i: (0, i % tps))] * 2
        args += list(rope_tables)
    qt_spec = pl.BlockSpec((WIDTH, tm), lambda i: (0, i))
    k_spec = pl.BlockSpec((tm, KV_WIDTH), lambda i: (i, 0))
    vt_shape = (KV_WIDTH, m)
    vt_spec = pl.BlockSpec((KV_WIDTH, tm), lambda i: (0, i))
    if rope:
        vbt_shape = (m // LANES, KV_WIDTH, LANES)
        vbt_spec = pl.BlockSpec((tm // LANES, KV_WIDTH, LANES), lambda i: (i, 0, 0))
    else:
        vbt_shape, vbt_spec = vt_shape, vt_spec
    out_specs = [qt_spec, qt_spec, k_spec, k_spec, vt_spec, vbt_spec,
                 pl.BlockSpec((tm, PZ_WIDTH), lambda i: (i, 0)),
                 pl.BlockSpec((STAT_ROWS, LANES), lambda i: (i, 0))]
    out_shape = [jax.ShapeDtypeStruct((WIDTH, m), BF16)] * 2
    out_shape += [jax.ShapeDtypeStruct((m, KV_WIDTH), BF16)] * 2
    out_shape += [jax.ShapeDtypeStruct(vt_shape, BF16), jax.ShapeDtypeStruct(vbt_shape, BF16)]
    out_shape += [jax.ShapeDtypeStruct((m, PZ_WIDTH), BF16), jax.ShapeDtypeStruct((m // tm * STAT_ROWS, LANES), F32)]
    if with_state:
        out_specs += [pl.BlockSpec((tm // seq, KV_WIDTH, seq), lambda i: (i, 0, 0))] * 4
        out_shape += [jax.ShapeDtypeStruct((n_seq, KV_WIDTH, seq), F32)] * 4
    return pl.pallas_call(
        functools.partial(_front_kernel, rope=rope, with_state=with_state),
        grid=(m // tm,),
        in_specs=in_specs,
        out_specs=out_specs,
        out_shape=out_shape,
        compiler_params=pltpu.CompilerParams(
            dimension_semantics=("arbitrary",), vmem_limit_bytes=VMEM_LIMIT),
        name="front_latent" if rope else "front_context",
    )(*args)


def _scores_t(q_heads_t, sources, kv_head):
    top = jnp.concatenate(q_heads_t, axis=1) if len(q_heads_t) > 1 else q_heads_t[0]
    zeros = jnp.zeros_like(top)
    q_aug = jnp.concatenate([top, zeros] if kv_head == 0 else [zeros, top], axis=0)
    scores = []
    for k, _, mask in sources:
        s = _dot(k(), q_aug)
        if mask is not None:
            s = jnp.where(mask(), s, NEG_INF)
        scores.append(s)
    return scores


def _softmax_pv_t(scores, sources, sinks, kv_head, tq, shift=None):
    m = shift
    if m is None:
        for s in scores:
            ms = jnp.max(s, axis=0, keepdims=True)
            m = ms if m is None else jnp.maximum(m, ms)
    if sinks is not None:
        sink = jnp.concatenate([jnp.full((1, tq), sk * LOG2E, F32) for sk in sinks], axis=1)
        m = jnp.maximum(m, sink)
    rows = slice(kv_head * HEAD_DIM, (kv_head + 1) * HEAD_DIM)
    r = None
    for s, src in zip(scores, sources):
        v_t = src[1](rows)
        v_aug = jnp.concatenate([v_t, jnp.ones((BF16_SUBLANES, v_t.shape[1]), BF16)], axis=0)
        pv = _dot(v_aug, jnp.exp2(s - m).astype(BF16))
        r = pv if r is None else r + pv
    denom = r[HEAD_DIM:HEAD_DIM + 1]
    if sinks is not None:
        denom = denom + jnp.exp2(sink - m)
    return r[:HEAD_DIM] * (1.0 / denom)


def _attention_branches(branches, pz_ref, z_ref, heads_per_dot, tail):
    tq = pz_ref.shape[0]
    items = [(br, kv_head, h0) for br in range(len(branches)) for kv_head in range(N_KV)
             for h0 in range(kv_head * GROUP, (kv_head + 1) * GROUP, heads_per_dot)]

    def q_heads_of(item):
        br, _, h0 = item
        qt_ref = branches[br][0]
        return [qt_ref[h * HEAD_DIM:(h + 1) * HEAD_DIM, :] for h in range(h0, h0 + heads_per_dot)]

    def scores_of(item):
        return _scores_t(q_heads_of(item), branches[item[0]][1], item[1])

    shifts = [SHIFT_MARGIN * jnp.sqrt(jnp.full((1, heads_per_dot * tq), branches[br][5][kv_head], F32))
              for br, kv_head, _ in items]
    worst = None
    for branch in branches:
        for sq in branch[5]:
            worst = sq if worst is None else jnp.maximum(worst, sq)
    bounded = worst * SHIFT_MARGIN ** 2 <= MAX_SAFE_SHIFT ** 2

    def run(use_shift):
        pending = scores_of(items[0])
        for i, (br, kv_head, h0) in enumerate(items):
            scores = pending
            if i + 1 < len(items):
                pending = scores_of(items[i + 1])
            _, sources, sink_ref, gate_block, z_block, _ = branches[br]
            sinks = None if sink_ref is None else [sink_ref[h] for h in range(h0, h0 + heads_per_dot)]
            o_t = _softmax_pv_t(scores, sources, sinks, kv_head, tq, shifts[i] if use_shift else None)
            for j in range(0, heads_per_dot, 2):
                pair = (h0 + j) // 2
                o = o_t[:, j * tq:(j + 2) * tq]
                o = jnp.concatenate([o[:, :tq], o[:, tq:]], axis=0).T
                gate = pz_ref[:, gate_block * WIDTH + pair * LANES: gate_block * WIDTH + (pair + 1) * LANES]
                z_ref[:, z_block * WIDTH + pair * LANES: z_block * WIDTH + (pair + 1) * LANES] = (
                    o * gate.astype(F32)).astype(BF16)
        tail()

    pl.when(bounded)(functools.partial(run, True))
    pl.when(jnp.logical_not(bounded))(functools.partial(run, False))


def _spatial_gate(pz_ref, ws_ref, bs_ref, z_ref, n_rows):
    for ch in range(n_rows // CHUNK):
        rows = slice(ch * CHUNK, (ch + 1) * CHUNK)
        for g in range(C_GROUPS):
            cols = lambda blk: slice(blk * WIDTH + g * LANES, blk * WIDTH + (g + 1) * LANES)
            mixed = _dot(ws_ref[g], pz_ref[rows, cols(PZ_VN)]) + bs_ref[:, g * LANES:(g + 1) * LANES]
            zc = pz_ref[rows, cols(PZ_UC)].astype(F32) * mixed * pz_ref[rows, cols(PZ_GC)].astype(F32)
            z_ref[rows, 2 * WIDTH + g * LANES: 2 * WIDTH + (g + 1) * LANES] = zc.astype(BF16)


def _ref_source(k_ref, vt_ref):
    return (lambda: k_ref[...], lambda rows: vt_ref[rows, :], None)


def _mixer_context_kernel(qat_ref, qbt_ref, ka_ref, kb_ref, vat_ref, vbt_ref, pz_ref, stat_ref,
                          sink_ref, ws_ref, bs_ref, z_ref, *, tokens_per_stat):
    t = lax.div(pl.program_id(0) * pz_ref.shape[0], tokens_per_stat)
    sq = lambda br: [stat_ref[t, br * N_KV + g] * stat_ref[t, STAT_Q + br] for g in range(N_KV)]
    _attention_branches(
        [(qat_ref, [_ref_source(ka_ref, vat_ref)], None, PZ_GA, 0, sq(0)),
         (qbt_ref, [_ref_source(kb_ref, vbt_ref)], sink_ref, PZ_GB, 1, sq(1))],
        pz_ref, z_ref, GROUP,
        functools.partial(_spatial_gate, pz_ref, ws_ref, bs_ref, z_ref, pz_ref.shape[0]))


def _mixer_context(qat, qbt, ka, kb, vat, vbt, pz, stats, sink, ws, bs, *, layer, seq, tokens_per_stat):
    m = pz.shape[0]
    qt_spec = pl.BlockSpec((WIDTH, seq), lambda i: (0, i))
    k_spec = pl.BlockSpec((seq, KV_WIDTH), lambda i: (i, 0))
    vt_spec = pl.BlockSpec((KV_WIDTH, seq), lambda i: (0, i))
    return pl.pallas_call(
        functools.partial(_mixer_context_kernel, tokens_per_stat=tokens_per_stat),
        grid=(m // seq,),
        in_specs=[
            qt_spec, qt_spec, k_spec, k_spec, vt_spec, vt_spec,
            pl.BlockSpec((seq, PZ_WIDTH), lambda i: (i, 0)),
            pl.BlockSpec(memory_space=pltpu.SMEM),
            pl.BlockSpec(memory_space=pltpu.SMEM),
            _layer_block((C_GROUPS, CHUNK, CHUNK), layer),
            pl.BlockSpec((CHUNK, WIDTH), lambda i: (0, 0)),
        ],
        out_specs=pl.BlockSpec((seq, 3 * WIDTH), lambda i: (i, 0)),
        out_shape=jax.ShapeDtypeStruct((m, 3 * WIDTH), BF16),
        compiler_params=pltpu.CompilerParams(
            dimension_semantics=("arbitrary",), vmem_limit_bytes=VMEM_LIMIT),
        name="mixer_context",
    )(qat, qbt, ka, kb, vat, vbt, pz, stats, sink, ws, bs)


def _mixer_latent_kernel(qat_ref, qbt_ref, ka_ref, kb_ref, vat_ref, vbt_ref, pz_ref, stat_ref,
                         cak_ref, cav_ref, cbk_ref, cbv_ref, sink_ref, ws_ref, bs_ref,
                         z_ref, cak_s, cav_s, cbk_s, cbv_s, ck2_s, *, seq, heads_per_dot, tokens_per_stat):
    tq = pz_ref.shape[0]
    band = tq + 2 * WINDOW
    qi = pl.program_id(1)

    @pl.when(qi == 0)
    def _():
        cak_s[...] = cak_ref[0, 0].T.astype(BF16)
        cbk_s[...] = cbk_ref[0, 0].T.astype(BF16)
        cav_s[...] = cav_ref[0, 0].astype(BF16)
        cbv_s[...] = cbv_ref[0, 0].astype(BF16)
        for br, cache_ref in enumerate((cak_ref, cbk_ref)):
            ck = cache_ref[0, 0]
            for g in range(N_KV):
                c2 = jnp.sum(jnp.square(ck[g * HEAD_DIM:(g + 1) * HEAD_DIM]), axis=0, keepdims=True)
                ck2_s[br * N_KV + g] = jnp.max(c2)

    start_blk = jnp.clip(qi * (tq // LANES) - WINDOW // LANES, 0, (seq - band) // LANES)
    start = pl.multiple_of(start_blk * LANES, LANES)

    def in_window():
        k_pos = start + lax.broadcasted_iota(jnp.int32, (band, heads_per_dot * tq), 0)
        q_pos = qi * tq + (lax.broadcasted_iota(jnp.int32, (band, heads_per_dot * tq), 1) & (tq - 1))
        return jnp.abs(q_pos - k_pos) <= WINDOW

    window = (lambda: kb_ref[pl.ds(start, band), :],
              lambda rows: jnp.concatenate(
                  [vbt_ref[start_blk + j, rows, :] for j in range(band // LANES)], axis=1),
              in_window)

    def sq(br):
        b = pl.program_id(0)
        tps = seq // tokens_per_stat
        q2 = stat_ref[b * tps + lax.div(qi * tq, tokens_per_stat), STAT_Q + br]
        out = []
        for g in range(N_KV):
            k2 = ck2_s[br * N_KV + g]
            for j in range(tps):
                k2 = jnp.maximum(k2, stat_ref[b * tps + j, br * N_KV + g])
            out.append(q2 * k2)
        return out

    _attention_branches(
        [(qat_ref, [_ref_source(ka_ref, vat_ref), _ref_source(cak_s, cav_s)], None, PZ_GA, 0,
          sq(0)),
         (qbt_ref, [window, _ref_source(cbk_s, cbv_s)],
          sink_ref, PZ_GB, 1, sq(1))],
        pz_ref, z_ref, heads_per_dot,
        functools.partial(_spatial_gate, pz_ref, ws_ref, bs_ref, z_ref, tq))


def _mixer_latent(qat, qbt, ka, kb, vat, vbt, pz, stats, cak, cav, cbk, cbv, sink, ws, bs, *, layer, seq, tq,
                  tokens_per_stat):
    m = pz.shape[0]
    n_seq = m // seq
    nq = seq // tq
    past = cak.shape[3]
    qt_spec = pl.BlockSpec((WIDTH, tq), lambda b, q: (0, b * nq + q))
    k_spec = pl.BlockSpec((seq, KV_WIDTH), lambda b, q: (b, 0))
    cache_spec = pl.BlockSpec((1, 1, KV_WIDTH, past), lambda b, q: (b, layer, 0, 0))
    return pl.pallas_call(
        functools.partial(_mixer_latent_kernel, seq=seq, heads_per_dot=2, tokens_per_stat=tokens_per_stat),
        grid=(n_seq, nq),
        in_specs=[
            qt_spec, qt_spec, k_spec, k_spec,
            pl.BlockSpec((KV_WIDTH, seq), lambda b, q: (0, b)),
            pl.BlockSpec((seq // LANES, KV_WIDTH, LANES), lambda b, q: (b, 0, 0)),
            pl.BlockSpec((tq, PZ_WIDTH), lambda b, q: (b * nq + q, 0)),
            pl.BlockSpec(memory_space=pltpu.SMEM),
            cache_spec, cache_spec, cache_spec, cache_spec,
            pl.BlockSpec(memory_space=pltpu.SMEM),
            _layer_block((C_GROUPS, CHUNK, CHUNK), layer),
            pl.BlockSpec((CHUNK, WIDTH), lambda b, q: (0, 0)),
        ],
        out_specs=pl.BlockSpec((tq, 3 * WIDTH), lambda b, q: (b * nq + q, 0)),
        out_shape=jax.ShapeDtypeStruct((m, 3 * WIDTH), BF16),
        scratch_shapes=[pltpu.VMEM((past, KV_WIDTH), BF16), pltpu.VMEM((KV_WIDTH, past), BF16),
                        pltpu.VMEM((past, KV_WIDTH), BF16), pltpu.VMEM((KV_WIDTH, past), BF16),
                        pltpu.SMEM((2 * N_KV,), F32)],
        compiler_params=pltpu.CompilerParams(
            dimension_semantics=("arbitrary", "arbitrary"), vmem_limit_bytes=VMEM_LIMIT),
        name="mixer_latent",
    )(qat, qbt, ka, kb, vat, vbt, pz, stats, cak, cav, cbk, cbv, sink, ws, bs)


def _back_kernel(x_ref, z_ref, mod_ref, wg_ref, bg_ref, wpa_ref, wpb_ref, wpc_ref, wo_ref,
                 lg_ref, lb_ref, o_ref, *, n_sub):
    mod = mod_ref[0]
    sub = x_ref.shape[0] // n_sub
    wp_refs = (wpa_ref, wpb_ref, wpc_ref)

    def stages(s):
        rows = slice(s * sub, (s + 1) * sub)
        st = {}

        def norm():
            st["h"] = _modulated_norm(x_ref[rows, :], mod).astype(BF16)

        def branch(br):
            cols = slice(br * D_MODEL, (br + 1) * D_MODEL)
            g = jax.nn.sigmoid(_dot(st["h"], wg_ref[:, cols]) + bg_ref[:, cols])
            y = _dot(z_ref[rows, br * WIDTH:(br + 1) * WIDTH], wp_refs[br][...])
            st["mix"] = g * y if br == 0 else st["mix"] + g * y

        def out_proj():
            st["y"] = _dot(st["mix"].astype(BF16), wo_ref[...])

        def residual_norm():
            r = DEEPNORM_ALPHA * x_ref[rows, :] + mod[:, 2 * D_MODEL:] * st["y"]
            mu = jnp.mean(r, axis=-1, keepdims=True)
            rc = r - mu
            var = jnp.mean(rc * rc, axis=-1, keepdims=True)
            o_ref[rows, :] = rc * lax.rsqrt(var + EPS) * lg_ref[...] + lb_ref[...]

        return [norm] + [functools.partial(branch, br) for br in range(3)] + [out_proj, residual_norm]

    chains = [stages(s) for s in range(n_sub)]
    n_stage = len(chains[0])
    for t in range(n_stage + (n_sub - 1) * BACK_STAGE_LAG):
        for s in range(n_sub):
            k = t - s * BACK_STAGE_LAG
            if 0 <= k < n_stage:
                chains[s][k]()


def _back(x, z, mod, wg, bg, wpa, wpb, wpc, wo, lg, lb, *, layer, tm, rows_per_mod, name):
    m = x.shape[0]
    tiles_per_mod = rows_per_mod // tm
    return pl.pallas_call(
        functools.partial(_back_kernel, n_sub=BACK_SUB_TILES),
        grid=(m // tm,),
        in_specs=[
            pl.BlockSpec((tm, D_MODEL), lambda i: (i, 0)),
            pl.BlockSpec((tm, 3 * WIDTH), lambda i: (i, 0)),
            pl.BlockSpec((1, 1, 3 * D_MODEL), lambda i: (i // tiles_per_mod, 0, 0)),
            _layer_block((D_MODEL, 3 * D_MODEL), layer),
            _layer_block((1, 3 * D_MODEL), layer),
            _layer_block((WIDTH, D_MODEL), layer),
            _layer_block((WIDTH, D_MODEL), layer),
            _layer_block((WIDTH, D_MODEL), layer),
            _layer_block((D_MODEL, D_MODEL), layer),
            _layer_block((1, D_MODEL), layer),
            _layer_block((1, D_MODEL), layer),
        ],
        out_specs=pl.BlockSpec((tm, D_MODEL), lambda i: (i, 0)),
        out_shape=jax.ShapeDtypeStruct((m, D_MODEL), F32),
        compiler_params=pltpu.CompilerParams(
            dimension_semantics=("arbitrary",), vmem_limit_bytes=VMEM_LIMIT),
        name=name,
    )(x, z, mod, wg, bg, wpa, wpb, wpc, wo, lg, lb)


def _rope_tables(n_tokens):
    rows = n_tokens // GRID_W
    row = jnp.repeat(jnp.arange(rows, dtype=F32), GRID_W)
    col = jnp.tile(jnp.arange(GRID_W, dtype=F32), rows)
    inv_freq = jnp.power(ROPE_BASE, -jnp.arange(ROPE_FREQS, dtype=F32) / ROPE_FREQS)
    ang = jnp.concatenate([inv_freq[:, None] * row[None], inv_freq[:, None] * col[None]], axis=0)
    return jnp.cos(ang), jnp.sin(ang)


def _tile_stats(stats):
    return stats.reshape(-1, STAT_ROWS, LANES)[:, :, 0]


def _transposed_cache(t):
    b, depth, past = t.shape[:3]
    return jnp.transpose(t, (0, 1, 3, 4, 2)).reshape(b, depth, KV_WIDTH, past)


def kernel(x_prompt, x_sample, cache_a_k, cache_a_v, cache_b_k, cache_b_v, c, c_ctx, w_mod, b_mod,
           w_in, q_norm, k_norm, sink, sgu_g, sgu_b, w_spatial, b_spatial, w_proj_a, w_proj_b,
           w_proj_c, w_gate, b_gate, w_out, ln_g, ln_b):
    batch, seq, _ = x_prompt.shape
    dec_batch, dec_seq, _ = x_sample.shape
    tm = 512

    cvecs = jnp.concatenate(
        [c_ctx[None], c, jnp.zeros((MOD_ROWS - 1 - dec_batch, D_MODEL), F32)], axis=0)
    mod = _modulation(cvecs, w_mod, b_mod)

    rope_tables = _rope_tables(dec_seq)
    caches = [_transposed_cache(t) for t in (cache_a_k, cache_a_v, cache_b_k, cache_b_v)]

    w_in_bf = w_in.astype(BF16)
    wt_bf = jnp.swapaxes(lax.optimization_barrier(jnp.concatenate(
        [w_in[:, :, lo:lo + width] for lo, width in _T_SEGMENTS], axis=-1)), 1, 2).astype(BF16)
    wg_bf, wo_bf, ws_bf = w_gate.astype(BF16), w_out.astype(BF16), w_spatial.astype(BF16)
    wpa_bf, wpb_bf, wpc_bf = w_proj_a.astype(BF16), w_proj_b.astype(BF16), w_proj_c.astype(BF16)
    bg3, lg3, lb3 = b_gate[:, None], ln_g[:, None], ln_b[:, None]

    xp = x_prompt.reshape(batch * seq, D_MODEL)
    xs = x_sample.reshape(dec_batch * dec_seq, D_MODEL)
    states = []
    for l in range(DEPTH):
        bs = jnp.repeat(b_spatial[l].T, LANES, axis=1)
        qn = jnp.broadcast_to(q_norm[l][:, None], (HEAD_DIM, tm))
        kn = jnp.broadcast_to(k_norm[l][:, None], (HEAD_DIM, tm))
        sg, sb = sgu_g[l][None], sgu_b[l][None]
        mod_ctx = mod[l, 0:1][None]
        mod_lat = mod[l, 1:1 + dec_batch][:, None]

        *acts, st_ak, st_av, st_bk, st_bv = _front(
            xp, mod_ctx, w_in_bf, wt_bf, qn, kn, sg, sb, None, layer=l, tm=tm, seq=seq, with_state=True)
        states.append((st_ak, st_av, st_bk, st_bv))
        z = _mixer_context(*acts[:-1], _tile_stats(acts[-1]), sink[l], ws_bf, bs,
                           layer=l, seq=seq, tokens_per_stat=tm)
        xp = _back(xp, z, mod_ctx, wg_bf, bg3, wpa_bf, wpb_bf, wpc_bf, wo_bf, lg3, lb3,
                   layer=l, tm=2 * tm, rows_per_mod=batch * seq, name="back_context")

        acts = _front(xs, mod_lat, w_in_bf, wt_bf, qn, kn, sg, sb, rope_tables,
                      layer=l, tm=tm, seq=dec_seq, with_state=False)
        z = _mixer_latent(*acts[:-1], _tile_stats(acts[-1]), *caches, sink[l], ws_bf, bs,
                          layer=l, seq=dec_seq, tq=256, tokens_per_stat=tm)
        xs = _back(xs, z, mod_lat, wg_bf, bg3, wpa_bf, wpb_bf, wpc_bf, wo_bf, lg3, lb3,
                   layer=l, tm=2 * tm, rows_per_mod=dec_seq, name="back_latent")

    def state(kind):
        t = jnp.stack([st[kind] for st in states], axis=1)
        return jnp.transpose(t.reshape(batch, DEPTH, N_KV, HEAD_DIM, seq), (0, 1, 4, 2, 3))

    return (xp.reshape(batch, seq, D_MODEL), xs.reshape(dec_batch, dec_seq, D_MODEL),
            state(0), state(1), state(2), state(3))
```

```python
import functools

import jax
import jax.numpy as jnp
from jax import lax
from jax.experimental import pallas as pl
from jax.experimental.pallas import tpu as pltpu

D_MODEL = 1024
DEPTH = 4
GRID_W = 64
HEAD_DIM = 64
HALF = HEAD_DIM // 2
N_HEADS = 8
N_KV = 2
GROUP = N_HEADS // N_KV
KV_WIDTH = N_KV * HEAD_DIM
WIDTH = 512
C_GROUPS = 4
CHUNK = 128
WINDOW = 128
IN_WIDTH = 4096
ROPE_BASE = 10000.0
ROPE_FREQS = HEAD_DIM // 4
EPS = 1e-6
NEG_INF = -1e30
DEEPNORM_ALPHA = (2 * DEPTH) ** 0.25
LOG2E = 1.4426950408889634
Q_SCALE = HEAD_DIM ** -0.5 * LOG2E

LANES = 128
BF16_SUBLANES = 16
MOD_ROWS = 8

_QA, _KA, _VA, _GA = 0, 512, 640, 768
_QB, _KB, _VB, _GB = 1280, 1792, 1920, 2048
_UC, _VC, _GC = 2560, 3072, 3584
_T_SEGMENTS = ((_QA, WIDTH), (_KA, KV_WIDTH), (_VA, KV_WIDTH), (_QB, WIDTH), (_KB, KV_WIDTH), (_VB, KV_WIDTH))
T_QA, T_KA, T_VA, T_QB, T_KB, T_VB = 0, 512, 640, 768, 1280, 1408
T_ROWS = 1536

(PZ_GA, PZ_GB, PZ_UC, PZ_VN, PZ_GC) = range(5)
PZ_WIDTH = 5 * WIDTH

BF16 = jnp.bfloat16
F32 = jnp.float32

VMEM_LIMIT = 56 * 1024 * 1024
SHIFT_MARGIN = 1.02
MAX_SAFE_SHIFT = 48.0
STAT_ROWS = 8
STAT_Q = 2 * N_KV
BACK_SUB_TILES = 4
BACK_STAGE_LAG = 4


def _silu(t):
    return t * jax.nn.sigmoid(t)


def _dot(a, b):
    return jnp.dot(a, b, preferred_element_type=F32)


def _dot_nt(a, b):
    return lax.dot_general(a, b, (((1,), (1,)), ((), ())), preferred_element_type=F32)


def _layer_block(shape, layer):
    zeros = (0,) * len(shape)
    return pl.BlockSpec((None,) + tuple(shape), lambda *_: (layer,) + zeros,
                        pipeline_mode=pl.Buffered(1))


def _mod_kernel(c_ref, w_ref, b_ref, o_ref):
    s = _silu(c_ref[...]).astype(BF16)
    o_ref[0] = _dot(s, w_ref[0].astype(BF16)) + b_ref[0]


def _modulation(cvecs, w_mod, b_mod):
    tn = 1024
    return pl.pallas_call(
        _mod_kernel,
        grid=(DEPTH, 3 * D_MODEL // tn),
        in_specs=[
            pl.BlockSpec((MOD_ROWS, D_MODEL), lambda l, j: (0, 0)),
            pl.BlockSpec((1, D_MODEL, tn), lambda l, j: (l, 0, j)),
            pl.BlockSpec((1, 1, tn), lambda l, j: (l, 0, j)),
        ],
        out_specs=pl.BlockSpec((1, MOD_ROWS, tn), lambda l, j: (l, 0, j)),
        out_shape=jax.ShapeDtypeStruct((DEPTH, MOD_ROWS, 3 * D_MODEL), F32),
        compiler_params=pltpu.CompilerParams(
            dimension_semantics=("arbitrary", "arbitrary"), vmem_limit_bytes=VMEM_LIMIT),
        name="modulation",
    )(cvecs, w_mod, b_mod.reshape(DEPTH, 1, 3 * D_MODEL))


def _modulated_norm(x, mod):
    mu = jnp.mean(x, axis=-1, keepdims=True)
    xc = x - mu
    var = jnp.mean(xc * xc, axis=-1, keepdims=True)
    xn = xc * lax.rsqrt(var + EPS)
    return xn * (1.0 + mod[:, D_MODEL:2 * D_MODEL]) + mod[:, :D_MODEL]


def _write_token_blocks(ref, t):
    w = ref.shape[2]
    for j in range(ref.shape[0]):
        ref[j] = t[:, j * w:(j + 1) * w].astype(ref.dtype)


def _front_kernel(*refs, rope, with_state):
    refs = list(refs)
    x_ref, mod_ref, w_ref, wt_ref, qn_ref, kn_ref, sg_ref, sb_ref = refs[:8]
    refs = refs[8:]
    if rope:
        cos_ref, sin_ref = refs[:2]
        refs = refs[2:]
    if with_state:
        refs = refs[4:]
    qat_ref, qbt_ref, ka_ref, kb_ref, vat_ref, vbt_ref, pz_ref, stat_ref = refs[:8]
    st_refs = refs[8:] if with_state else None

    h = _modulated_norm(x_ref[...], mod_ref[0]).astype(BF16)

    def rotary_t(t):
        if not rope:
            return t
        x1, x2 = t[:HALF], t[HALF:]
        c, s = cos_ref[...], sin_ref[...]
        return jnp.concatenate([x1 * c - x2 * s, x2 * c + x1 * s], axis=0)

    def rms_t(t, gain):
        ms = jnp.mean(t * t, axis=0, keepdims=True)
        return t * lax.rsqrt(ms + EPS) * gain

    def proj_t(row0, rows):
        return _dot_nt(wt_ref[row0:row0 + rows, :], h)

    def proj(lo):
        return _dot(h, w_ref[:, lo:lo + WIDTH])

    def heads_of(t):
        return [t[j * HEAD_DIM:(j + 1) * HEAD_DIM] for j in range(t.shape[0] // HEAD_DIM)]

    def store(block, val):
        pz_ref[:, block * WIDTH:(block + 1) * WIDTH] = val.astype(BF16)

    def max_sq_norm(heads):
        m = None
        for t in heads:
            r = jnp.sum(t * t, axis=0, keepdims=True)
            m = r if m is None else jnp.maximum(m, r)
        return jnp.max(m, axis=1, keepdims=True)

    qn, kn = qn_ref[...], kn_ref[...]
    qa_t = proj_t(T_QA, WIDTH)
    ga = proj(_GA)
    qa = [rotary_t(rms_t(t, qn)) * Q_SCALE for t in heads_of(qa_t)]
    qat_ref[...] = jnp.concatenate(qa, axis=0).astype(BF16)
    qb_t = proj_t(T_QB, WIDTH)
    store(PZ_GA, _silu(ga))
    gb = proj(_GB)
    qb = [rotary_t(t) * Q_SCALE for t in heads_of(qb_t)]
    qbt_ref[...] = jnp.concatenate(qb, axis=0).astype(BF16)
    kva_t = proj_t(T_KA, 2 * KV_WIDTH)
    kvb_t = proj_t(T_KB, 2 * KV_WIDTH)
    store(PZ_GB, _silu(gb))
    uc = proj(_UC)

    ka = [rms_t(t, kn) for t in heads_of(kva_t[:KV_WIDTH])]
    kb = heads_of(kvb_t[:KV_WIDTH])
    va, vb = kva_t[KV_WIDTH:], kvb_t[KV_WIDTH:]
    ka_ref[...] = jnp.concatenate([rotary_t(t) for t in ka], axis=0).T.astype(BF16)
    kb_ref[...] = jnp.concatenate([rotary_t(t) for t in kb], axis=0).T.astype(BF16)
    vat_ref[...] = va.astype(BF16)
    if len(vbt_ref.shape) == 3:
        _write_token_blocks(vbt_ref, vb)
    else:
        vbt_ref[...] = vb.astype(BF16)
    stats = [max_sq_norm([t]) for t in ka + kb] + [max_sq_norm(qa), max_sq_norm(qb)]
    for r, v in enumerate(stats):
        stat_ref[r:r + 1, :] = jnp.broadcast_to(v, (1, LANES))
    stat_ref[len(stats):, :] = jnp.zeros((STAT_ROWS - len(stats), LANES), F32)
    if with_state:
        _write_token_blocks(st_refs[0], jnp.concatenate(ka, axis=0))
        _write_token_blocks(st_refs[1], va)
        _write_token_blocks(st_refs[2], jnp.concatenate(kb, axis=0))
        _write_token_blocks(st_refs[3], vb)

    store(PZ_UC, uc)
    vc = proj(_VC)
    gc = proj(_GC)
    mu = jnp.mean(vc, axis=-1, keepdims=True)
    vcc = vc - mu
    var = jnp.mean(vcc * vcc, axis=-1, keepdims=True)
    store(PZ_VN, vcc * lax.rsqrt(var + EPS) * sg_ref[...] + sb_ref[...])
    store(PZ_GC, _silu(gc))


def _front(x, mod, w_in, wt, qn, kn, sg, sb, rope_tables, states, *, layer, tm, seq):
    with_state = states is not None
    m = x.shape[0]
    n_seq = m // seq
    rope = rope_tables is not None
    const = lambda i: (0, 0)
    tps = max(seq // tm, 1)
    mod_idx = (lambda i: (0, 0, 0)) if tm >= seq else (lambda i: (i // tps, 0, 0))
    in_specs = [
        pl.BlockSpec((tm, D_MODEL), lambda i: (i, 0)),
        pl.BlockSpec((1, 1, 3 * D_MODEL), mod_idx),
        _layer_block((D_MODEL, IN_WIDTH), layer),
        _layer_block((T_ROWS, D_MODEL), layer),
        pl.BlockSpec((HEAD_DIM, tm), const),
        pl.BlockSpec((HEAD_DIM, tm), const),
        pl.BlockSpec((1, WIDTH), const),
        pl.BlockSpec((1, WIDTH), const),
    ]
    args = [x, mod, w_in, wt, qn, kn, sg, sb]
    if rope:
        in_specs += [pl.BlockSpec((HALF, tm), lambda i: (0, i % tps))] * 2
        args += list(rope_tables)
    qt_spec = pl.BlockSpec((WIDTH, tm), lambda i: (0, i))
    k_spec = pl.BlockSpec((tm, KV_WIDTH), lambda i: (i, 0))
    vt_shape = (KV_WIDTH, m)
    vt_spec = pl.BlockSpec((KV_WIDTH, tm), lambda i: (0, i))
    if rope:
        vbt_shape = (m // LANES, KV_WIDTH, LANES)
        vbt_spec = pl.BlockSpec((tm // LANES, KV_WIDTH, LANES), lambda i: (i, 0, 0))
    else:
        vbt_shape, vbt_spec = vt_shape, vt_spec
    out_specs = [qt_spec, qt_spec, k_spec, k_spec, vt_spec, vbt_spec,
                 pl.BlockSpec((tm, PZ_WIDTH), lambda i: (i, 0)),
                 pl.BlockSpec((STAT_ROWS, LANES), lambda i: (i, 0))]
    out_shape = [jax.ShapeDtypeStruct((WIDTH, m), BF16)] * 2
    out_shape += [jax.ShapeDtypeStruct((m, KV_WIDTH), BF16)] * 2
    out_shape += [jax.ShapeDtypeStruct(vt_shape, BF16), jax.ShapeDtypeStruct(vbt_shape, BF16)]
    out_shape += [jax.ShapeDtypeStruct((m, PZ_WIDTH), BF16), jax.ShapeDtypeStruct((m // tm * STAT_ROWS, LANES), F32)]
    aliases = {}
    if with_state:
        aliases = {len(args) + j: len(out_shape) + j for j in range(4)}
        in_specs += [pl.BlockSpec(memory_space=pl.ANY)] * 4
        args += list(states)
        out_specs += [pl.BlockSpec((tm // seq, None, KV_WIDTH, seq), lambda i: (i, layer, 0, 0))] * 4
        out_shape += [jax.ShapeDtypeStruct((n_seq, DEPTH, KV_WIDTH, seq), F32)] * 4
    return pl.pallas_call(
        functools.partial(_front_kernel, rope=rope, with_state=with_state),
        grid=(m // tm,),
        in_specs=in_specs,
        out_specs=out_specs,
        out_shape=out_shape,
        input_output_aliases=aliases,
        compiler_params=pltpu.CompilerParams(
            dimension_semantics=("arbitrary",), vmem_limit_bytes=VMEM_LIMIT),
        name="front_latent" if rope else "front_context",
    )(*args)


def _scores_t(q_heads_t, sources, kv_head):
    top = jnp.concatenate(q_heads_t, axis=1) if len(q_heads_t) > 1 else q_heads_t[0]
    zeros = jnp.zeros_like(top)
    q_aug = jnp.concatenate([top, zeros] if kv_head == 0 else [zeros, top], axis=0)
    scores = []
    for k, _, mask in sources:
        s = _dot(k(), q_aug)
        if mask is not None:
            s = jnp.where(mask(), s, NEG_INF)
        scores.append(s)
    return scores


def _softmax_pv_t(scores, sources, sinks, kv_head, tq, shift=None):
    m = shift
    if m is None:
        for s in scores:
            ms = jnp.max(s, axis=0, keepdims=True)
            m = ms if m is None else jnp.maximum(m, ms)
    if sinks is not None:
        sink = jnp.concatenate([jnp.full((1, tq), sk * LOG2E, F32) for sk in sinks], axis=1)
        m = jnp.maximum(m, sink)
    rows = slice(kv_head * HEAD_DIM, (kv_head + 1) * HEAD_DIM)
    r = None
    for s, src in zip(scores, sources):
        v_t = src[1](rows)
        v_aug = jnp.concatenate([v_t, jnp.ones((BF16_SUBLANES, v_t.shape[1]), BF16)], axis=0)
        pv = _dot(v_aug, jnp.exp2(s - m).astype(BF16))
        r = pv if r is None else r + pv
    denom = r[HEAD_DIM:HEAD_DIM + 1]
    if sinks is not None:
        denom = denom + jnp.exp2(sink - m)
    return r[:HEAD_DIM] * (1.0 / denom)


def _attention_branches(branches, pz_ref, z_ref, heads_per_dot, tail):
    tq = pz_ref.shape[0]
    items = [(br, kv_head, h0) for br in range(len(branches)) for kv_head in range(N_KV)
             for h0 in range(kv_head * GROUP, (kv_head + 1) * GROUP, heads_per_dot)]

    def q_heads_of(item):
        br, _, h0 = item
        qt_ref = branches[br][0]
        return [qt_ref[h * HEAD_DIM:(h + 1) * HEAD_DIM, :] for h in range(h0, h0 + heads_per_dot)]

    def scores_of(item):
        return _scores_t(q_heads_of(item), branches[item[0]][1], item[1])

    shifts = [SHIFT_MARGIN * jnp.sqrt(jnp.full((1, heads_per_dot * tq), branches[br][5][kv_head], F32))
              for br, kv_head, _ in items]
    worst = None
    for branch in branches:
        for sq in branch[5]:
            worst = sq if worst is None else jnp.maximum(worst, sq)
    bounded = worst * SHIFT_MARGIN ** 2 <= MAX_SAFE_SHIFT ** 2

    def run(use_shift):
        pending = scores_of(items[0])
        for i, (br, kv_head, h0) in enumerate(items):
            scores = pending
            if i + 1 < len(items):
                pending = scores_of(items[i + 1])
            _, sources, sink_ref, gate_block, z_block, _ = branches[br]
            sinks = None if sink_ref is None else [sink_ref[h] for h in range(h0, h0 + heads_per_dot)]
            o_t = _softmax_pv_t(scores, sources, sinks, kv_head, tq, shifts[i] if use_shift else None)
            for j in range(0, heads_per_dot, 2):
                pair = (h0 + j) // 2
                o = o_t[:, j * tq:(j + 2) * tq]
                o = jnp.concatenate([o[:, :tq], o[:, tq:]], axis=0).T
                gate = pz_ref[:, gate_block * WIDTH + pair * LANES: gate_block * WIDTH + (pair + 1) * LANES]
                z_ref[:, z_block * WIDTH + pair * LANES: z_block * WIDTH + (pair + 1) * LANES] = (
                    o * gate.astype(F32)).astype(BF16)
        tail()

    pl.when(bounded)(functools.partial(run, True))
    pl.when(jnp.logical_not(bounded))(functools.partial(run, False))


def _spatial_gate(pz_ref, ws_ref, bs_ref, z_ref, n_rows):
    for ch in range(n_rows // CHUNK):
        rows = slice(ch * CHUNK, (ch + 1) * CHUNK)
        for g in range(C_GROUPS):
            cols = lambda blk: slice(blk * WIDTH + g * LANES, blk * WIDTH + (g + 1) * LANES)
            mixed = _dot(ws_ref[g], pz_ref[rows, cols(PZ_VN)]) + bs_ref[:, g * LANES:(g + 1) * LANES]
            zc = pz_ref[rows, cols(PZ_UC)].astype(F32) * mixed * pz_ref[rows, cols(PZ_GC)].astype(F32)
            z_ref[rows, 2 * WIDTH + g * LANES: 2 * WIDTH + (g + 1) * LANES] = zc.astype(BF16)


def _ref_source(k_ref, vt_ref):
    return (lambda: k_ref[...], lambda rows: vt_ref[rows, :], None)


def _mixer_context_kernel(qat_ref, qbt_ref, ka_ref, kb_ref, vat_ref, vbt_ref, pz_ref, stat_ref,
                          sink_ref, ws_ref, bs_ref, z_ref, *, tokens_per_stat):
    t = lax.div(pl.program_id(0) * pz_ref.shape[0], tokens_per_stat)
    sq = lambda br: [stat_ref[t, br * N_KV + g] * stat_ref[t, STAT_Q + br] for g in range(N_KV)]
    _attention_branches(
        [(qat_ref, [_ref_source(ka_ref, vat_ref)], None, PZ_GA, 0, sq(0)),
         (qbt_ref, [_ref_source(kb_ref, vbt_ref)], sink_ref, PZ_GB, 1, sq(1))],
        pz_ref, z_ref, GROUP,
        functools.partial(_spatial_gate, pz_ref, ws_ref, bs_ref, z_ref, pz_ref.shape[0]))


def _mixer_context(qat, qbt, ka, kb, vat, vbt, pz, stats, sink, ws, bs, *, layer, seq, tokens_per_stat):
    m = pz.shape[0]
    qt_spec = pl.BlockSpec((WIDTH, seq), lambda i: (0, i))
    k_spec = pl.BlockSpec((seq, KV_WIDTH), lambda i: (i, 0))
    vt_spec = pl.BlockSpec((KV_WIDTH, seq), lambda i: (0, i))
    return pl.pallas_call(
        functools.partial(_mixer_context_kernel, tokens_per_stat=tokens_per_stat),
        grid=(m // seq,),
        in_specs=[
            qt_spec, qt_spec, k_spec, k_spec, vt_spec, vt_spec,
            pl.BlockSpec((seq, PZ_WIDTH), lambda i: (i, 0)),
            pl.BlockSpec(memory_space=pltpu.SMEM),
            pl.BlockSpec(memory_space=pltpu.SMEM),
            _layer_block((C_GROUPS, CHUNK, CHUNK), layer),
            pl.BlockSpec((CHUNK, WIDTH), lambda i: (0, 0)),
        ],
        out_specs=pl.BlockSpec((seq, 3 * WIDTH), lambda i: (i, 0)),
        out_shape=jax.ShapeDtypeStruct((m, 3 * WIDTH), BF16),
        compiler_params=pltpu.CompilerParams(
            dimension_semantics=("arbitrary",), vmem_limit_bytes=VMEM_LIMIT),
        name="mixer_context",
    )(qat, qbt, ka, kb, vat, vbt, pz, stats, sink, ws, bs)


def _mixer_latent_kernel(qat_ref, qbt_ref, ka_ref, kb_ref, vat_ref, vbt_ref, pz_ref, stat_ref,
                         cak_ref, cav_ref, cbk_ref, cbv_ref, sink_ref, ws_ref, bs_ref,
                         z_ref, cak_s, cav_s, cbk_s, cbv_s, ck2_s, *, seq, heads_per_dot, tokens_per_stat):
    tq = pz_ref.shape[0]
    band = tq + 2 * WINDOW
    qi = pl.program_id(1)

    @pl.when(qi == 0)
    def _():
        cak_s[...] = cak_ref[0, 0].T.astype(BF16)
        cbk_s[...] = cbk_ref[0, 0].T.astype(BF16)
        cav_s[...] = cav_ref[0, 0].astype(BF16)
        cbv_s[...] = cbv_ref[0, 0].astype(BF16)
        for br, cache_ref in enumerate((cak_ref, cbk_ref)):
            ck = cache_ref[0, 0]
            for g in range(N_KV):
                c2 = jnp.sum(jnp.square(ck[g * HEAD_DIM:(g + 1) * HEAD_DIM]), axis=0, keepdims=True)
                ck2_s[br * N_KV + g] = jnp.max(c2)

    start_blk = jnp.clip(qi * (tq // LANES) - WINDOW // LANES, 0, (seq - band) // LANES)
    start = pl.multiple_of(start_blk * LANES, LANES)

    def in_window():
        k_pos = start + lax.broadcasted_iota(jnp.int32, (band, heads_per_dot * tq), 0)
        q_pos = qi * tq + (lax.broadcasted_iota(jnp.int32, (band, heads_per_dot * tq), 1) & (tq - 1))
        return jnp.abs(q_pos - k_pos) <= WINDOW

    window = (lambda: kb_ref[pl.ds(start, band), :],
              lambda rows: jnp.concatenate(
                  [vbt_ref[start_blk + j, rows, :] for j in range(band // LANES)], axis=1),
              in_window)

    def sq(br):
        b = pl.program_id(0)
        tps = seq // tokens_per_stat
        q2 = stat_ref[b * tps + lax.div(qi * tq, tokens_per_stat), STAT_Q + br]
        out = []
        for g in range(N_KV):
            k2 = ck2_s[br * N_KV + g]
            for j in range(tps):
                k2 = jnp.maximum(k2, stat_ref[b * tps + j, br * N_KV + g])
            out.append(q2 * k2)
        return out

    _attention_branches(
        [(qat_ref, [_ref_source(ka_ref, vat_ref), _ref_source(cak_s, cav_s)], None, PZ_GA, 0,
          sq(0)),
         (qbt_ref, [window, _ref_source(cbk_s, cbv_s)],
          sink_ref, PZ_GB, 1, sq(1))],
        pz_ref, z_ref, heads_per_dot,
        functools.partial(_spatial_gate, pz_ref, ws_ref, bs_ref, z_ref, tq))


def _mixer_latent(qat, qbt, ka, kb, vat, vbt, pz, stats, cak, cav, cbk, cbv, sink, ws, bs, *, layer, seq, tq,
                  tokens_per_stat):
    m = pz.shape[0]
    n_seq = m // seq
    nq = seq // tq
    past = cak.shape[3]
    qt_spec = pl.BlockSpec((WIDTH, tq), lambda b, q: (0, b * nq + q))
    k_spec = pl.BlockSpec((seq, KV_WIDTH), lambda b, q: (b, 0))
    cache_spec = pl.BlockSpec((1, 1, KV_WIDTH, past), lambda b, q: (b, layer, 0, 0))
    return pl.pallas_call(
        functools.partial(_mixer_latent_kernel, seq=seq, heads_per_dot=2, tokens_per_stat=tokens_per_stat),
        grid=(n_seq, nq),
        in_specs=[
            qt_spec, qt_spec, k_spec, k_spec,
            pl.BlockSpec((KV_WIDTH, seq), lambda b, q: (0, b)),
            pl.BlockSpec((seq // LANES, KV_WIDTH, LANES), lambda b, q: (b, 0, 0)),
            pl.BlockSpec((tq, PZ_WIDTH), lambda b, q: (b * nq + q, 0)),
            pl.BlockSpec(memory_space=pltpu.SMEM),
            cache_spec, cache_spec, cache_spec, cache_spec,
            pl.BlockSpec(memory_space=pltpu.SMEM),
            _layer_block((C_GROUPS, CHUNK, CHUNK), layer),
            pl.BlockSpec((CHUNK, WIDTH), lambda b, q: (0, 0)),
        ],
        out_specs=pl.BlockSpec((tq, 3 * WIDTH), lambda b, q: (b * nq + q, 0)),
        out_shape=jax.ShapeDtypeStruct((m, 3 * WIDTH), BF16),
        scratch_shapes=[pltpu.VMEM((past, KV_WIDTH), BF16), pltpu.VMEM((KV_WIDTH, past), BF16),
                        pltpu.VMEM((past, KV_WIDTH), BF16), pltpu.VMEM((KV_WIDTH, past), BF16),
                        pltpu.SMEM((2 * N_KV,), F32)],
        compiler_params=pltpu.CompilerParams(
            dimension_semantics=("arbitrary", "arbitrary"), vmem_limit_bytes=VMEM_LIMIT),
        name="mixer_latent",
    )(qat, qbt, ka, kb, vat, vbt, pz, stats, cak, cav, cbk, cbv, sink, ws, bs)


def _back_kernel(x_ref, z_ref, mod_ref, wg_ref, bg_ref, wpa_ref, wpb_ref, wpc_ref, wo_ref,
                 lg_ref, lb_ref, o_ref, *, n_sub):
    mod = mod_ref[0]
    sub = x_ref.shape[0] // n_sub
    wp_refs = (wpa_ref, wpb_ref, wpc_ref)

    def stages(s):
        rows = slice(s * sub, (s + 1) * sub)
        st = {}

        def norm():
            st["h"] = _modulated_norm(x_ref[rows, :], mod).astype(BF16)

        def branch(br):
            cols = slice(br * D_MODEL, (br + 1) * D_MODEL)
            g = jax.nn.sigmoid(_dot(st["h"], wg_ref[:, cols]) + bg_ref[:, cols])
            y = _dot(z_ref[rows, br * WIDTH:(br + 1) * WIDTH], wp_refs[br][...])
            st["mix"] = g * y if br == 0 else st["mix"] + g * y

        def out_proj():
            st["y"] = _dot(st["mix"].astype(BF16), wo_ref[...])

        def residual_norm():
            r = DEEPNORM_ALPHA * x_ref[rows, :] + mod[:, 2 * D_MODEL:] * st["y"]
            mu = jnp.mean(r, axis=-1, keepdims=True)
            rc = r - mu
            var = jnp.mean(rc * rc, axis=-1, keepdims=True)
            o_ref[rows, :] = rc * lax.rsqrt(var + EPS) * lg_ref[...] + lb_ref[...]

        return [norm] + [functools.partial(branch, br) for br in range(3)] + [out_proj, residual_norm]

    chains = [stages(s) for s in range(n_sub)]
    n_stage = len(chains[0])
    for t in range(n_stage + (n_sub - 1) * BACK_STAGE_LAG):
        for s in range(n_sub):
            k = t - s * BACK_STAGE_LAG
            if 0 <= k < n_stage:
                chains[s][k]()


def _back(x, z, mod, wg, bg, wpa, wpb, wpc, wo, lg, lb, *, layer, tm, rows_per_mod, name):
    m = x.shape[0]
    tiles_per_mod = rows_per_mod // tm
    return pl.pallas_call(
        functools.partial(_back_kernel, n_sub=BACK_SUB_TILES),
        grid=(m // tm,),
        in_specs=[
            pl.BlockSpec((tm, D_MODEL), lambda i: (i, 0)),
            pl.BlockSpec((tm, 3 * WIDTH), lambda i: (i, 0)),
            pl.BlockSpec((1, 1, 3 * D_MODEL), lambda i: (i // tiles_per_mod, 0, 0)),
            _layer_block((D_MODEL, 3 * D_MODEL), layer),
            _layer_block((1, 3 * D_MODEL), layer),
            _layer_block((WIDTH, D_MODEL), layer),
            _layer_block((WIDTH, D_MODEL), layer),
            _layer_block((WIDTH, D_MODEL), layer),
            _layer_block((D_MODEL, D_MODEL), layer),
            _layer_block((1, D_MODEL), layer),
            _layer_block((1, D_MODEL), layer),
        ],
        out_specs=pl.BlockSpec((tm, D_MODEL), lambda i: (i, 0)),
        out_shape=jax.ShapeDtypeStruct((m, D_MODEL), F32),
        compiler_params=pltpu.CompilerParams(
            dimension_semantics=("arbitrary",), vmem_limit_bytes=VMEM_LIMIT),
        name=name,
    )(x, z, mod, wg, bg, wpa, wpb, wpc, wo, lg, lb)


def _rope_tables(n_tokens):
    rows = n_tokens // GRID_W
    row = jnp.repeat(jnp.arange(rows, dtype=F32), GRID_W)
    col = jnp.tile(jnp.arange(GRID_W, dtype=F32), rows)
    inv_freq = jnp.power(ROPE_BASE, -jnp.arange(ROPE_FREQS, dtype=F32) / ROPE_FREQS)
    ang = jnp.concatenate([inv_freq[:, None] * row[None], inv_freq[:, None] * col[None]], axis=0)
    return jnp.cos(ang), jnp.sin(ang)


def _tile_stats(stats):
    return stats.reshape(-1, STAT_ROWS, LANES)[:, :, 0]


def _transposed_cache(t):
    b, depth, past = t.shape[:3]
    return jnp.transpose(t, (0, 1, 3, 4, 2)).reshape(b, depth, KV_WIDTH, past)


def kernel(x_prompt, x_sample, cache_a_k, cache_a_v, cache_b_k, cache_b_v, c, c_ctx, w_mod, b_mod,
           w_in, q_norm, k_norm, sink, sgu_g, sgu_b, w_spatial, b_spatial, w_proj_a, w_proj_b,
           w_proj_c, w_gate, b_gate, w_out, ln_g, ln_b):
    batch, seq, _ = x_prompt.shape
    dec_batch, dec_seq, _ = x_sample.shape
    tm = 512

    cvecs = jnp.concatenate(
        [c_ctx[None], c, jnp.zeros((MOD_ROWS - 1 - dec_batch, D_MODEL), F32)], axis=0)
    mod = _modulation(cvecs, w_mod, b_mod)

    rope_tables = _rope_tables(dec_seq)
    caches = [_transposed_cache(t) for t in (cache_a_k, cache_a_v, cache_b_k, cache_b_v)]

    w_in_bf = w_in.astype(BF16)
    wt_bf = jnp.swapaxes(lax.optimization_barrier(jnp.concatenate(
        [w_in[:, :, lo:lo + width] for lo, width in _T_SEGMENTS], axis=-1)), 1, 2).astype(BF16)
    wg_bf, wo_bf, ws_bf = w_gate.astype(BF16), w_out.astype(BF16), w_spatial.astype(BF16)
    wpa_bf, wpb_bf, wpc_bf = w_proj_a.astype(BF16), w_proj_b.astype(BF16), w_proj_c.astype(BF16)
    bg3, lg3, lb3 = b_gate[:, None], ln_g[:, None], ln_b[:, None]

    xp = x_prompt.reshape(batch * seq, D_MODEL)
    xs = x_sample.reshape(dec_batch * dec_seq, D_MODEL)
    states = [jnp.zeros((batch, DEPTH, KV_WIDTH, seq), F32)] * 4
    for l in range(DEPTH):
        bs = jnp.repeat(b_spatial[l].T, LANES, axis=1)
        qn = jnp.broadcast_to(q_norm[l][:, None], (HEAD_DIM, tm))
        kn = jnp.broadcast_to(k_norm[l][:, None], (HEAD_DIM, tm))
        sg, sb = sgu_g[l][None], sgu_b[l][None]
        mod_ctx = mod[l, 0:1][None]
        mod_lat = mod[l, 1:1 + dec_batch][:, None]

        *acts, st_ak, st_av, st_bk, st_bv = _front(
            xp, mod_ctx, w_in_bf, wt_bf, qn, kn, sg, sb, None, states, layer=l, tm=tm, seq=seq)
        states = [st_ak, st_av, st_bk, st_bv]
        z = _mixer_context(*acts[:-1], _tile_stats(acts[-1]), sink[l], ws_bf, bs,
                           layer=l, seq=seq, tokens_per_stat=tm)
        xp = _back(xp, z, mod_ctx, wg_bf, bg3, wpa_bf, wpb_bf, wpc_bf, wo_bf, lg3, lb3,
                   layer=l, tm=2 * tm, rows_per_mod=batch * seq, name="back_context")

        acts = _front(xs, mod_lat, w_in_bf, wt_bf, qn, kn, sg, sb, rope_tables, None,
                      layer=l, tm=tm, seq=dec_seq)
        z = _mixer_latent(*acts[:-1], _tile_stats(acts[-1]), *caches, sink[l], ws_bf, bs,
                          layer=l, seq=dec_seq, tq=256, tokens_per_stat=tm)
        xs = _back(xs, z, mod_lat, wg_bf, bg3, wpa_bf, wpb_bf, wpc_bf, wo_bf, lg3, lb3,
                   layer=l, tm=2 * tm, rows_per_mod=dec_seq, name="back_latent")

    def state(t):
        return jnp.transpose(t.reshape(batch, DEPTH, N_KV, HEAD_DIM, seq), (0, 1, 4, 2, 3))

    return (xp.reshape(batch, seq, D_MODEL), xs.reshape(dec_batch, dec_seq, D_MODEL),
            *[state(t) for t in states])
```

```python
import functools

import jax
import jax.numpy as jnp
from jax import lax
from jax.experimental import pallas as pl
from jax.experimental.pallas import tpu as pltpu

D_MODEL = 1024
DEPTH = 4
GRID_W = 64
HEAD_DIM = 64
HALF = HEAD_DIM // 2
N_HEADS = 8
N_KV = 2
GROUP = N_HEADS // N_KV
KV_WIDTH = N_KV * HEAD_DIM
WIDTH = 512
C_GROUPS = 4
CHUNK = 128
WINDOW = 128
IN_WIDTH = 4096
ROPE_BASE = 10000.0
ROPE_FREQS = HEAD_DIM // 4
EPS = 1e-6
NEG_INF = -1e30
DEEPNORM_ALPHA = (2 * DEPTH) ** 0.25
LOG2E = 1.4426950408889634
Q_SCALE = HEAD_DIM ** -0.5 * LOG2E

LANES = 128
BF16_SUBLANES = 16
MOD_ROWS = 8

_QA, _KA, _VA, _GA = 0, 512, 640, 768
_QB, _KB, _VB, _GB = 1280, 1792, 1920, 2048
_UC, _VC, _GC = 2560, 3072, 3584
_T_SEGMENTS = ((_QA, WIDTH), (_KA, KV_WIDTH), (_VA, KV_WIDTH), (_QB, WIDTH), (_KB, KV_WIDTH), (_VB, KV_WIDTH))
T_QA, T_KA, T_VA, T_QB, T_KB, T_VB = 0, 512, 640, 768, 1280, 1408
T_ROWS = 1536

(PZ_GA, PZ_GB, PZ_UC, PZ_VN, PZ_GC) = range(5)
PZ_WIDTH = 5 * WIDTH

BF16 = jnp.bfloat16
F32 = jnp.float32

VMEM_LIMIT = 56 * 1024 * 1024
SHIFT_MARGIN = 1.02
MAX_SAFE_SHIFT = 48.0
STAT_ROWS = 8
STAT_Q = 2 * N_KV
BACK_SUB_TILES = 4
BACK_STAGE_LAG = 4


def _silu(t):
    return t * jax.nn.sigmoid(t)


def _dot(a, b):
    return jnp.dot(a, b, preferred_element_type=F32)


def _dot_nt(a, b):
    return lax.dot_general(a, b, (((1,), (1,)), ((), ())), preferred_element_type=F32)


def _layer_block(shape, layer):
    zeros = (0,) * len(shape)
    return pl.BlockSpec((None,) + tuple(shape), lambda *_: (layer,) + zeros,
                        pipeline_mode=pl.Buffered(1))


def _mod_kernel(c_ref, w_ref, b_ref, o_ref):
    s = _silu(c_ref[...]).astype(BF16)
    o_ref[0] = _dot(s, w_ref[0].astype(BF16)) + b_ref[0]


def _modulation(cvecs, w_mod, b_mod):
    tn = 1024
    return pl.pallas_call(
        _mod_kernel,
        grid=(DEPTH, 3 * D_MODEL // tn),
        in_specs=[
            pl.BlockSpec((MOD_ROWS, D_MODEL), lambda l, j: (0, 0)),
            pl.BlockSpec((1, D_MODEL, tn), lambda l, j: (l, 0, j)),
            pl.BlockSpec((1, 1, tn), lambda l, j: (l, 0, j)),
        ],
        out_specs=pl.BlockSpec((1, MOD_ROWS, tn), lambda l, j: (l, 0, j)),
        out_shape=jax.ShapeDtypeStruct((DEPTH, MOD_ROWS, 3 * D_MODEL), F32),
        compiler_params=pltpu.CompilerParams(
            dimension_semantics=("arbitrary", "arbitrary"), vmem_limit_bytes=VMEM_LIMIT),
        name="modulation",
    )(cvecs, w_mod, b_mod.reshape(DEPTH, 1, 3 * D_MODEL))


def _modulated_norm(x, mod):
    mu = jnp.mean(x, axis=-1, keepdims=True)
    xc = x - mu
    var = jnp.mean(xc * xc, axis=-1, keepdims=True)
    xn = xc * lax.rsqrt(var + EPS)
    return xn * (1.0 + mod[:, D_MODEL:2 * D_MODEL]) + mod[:, :D_MODEL]


def _write_token_blocks(ref, t):
    w = ref.shape[2]
    for j in range(ref.shape[0]):
        ref[j] = t[:, j * w:(j + 1) * w].astype(ref.dtype)


def _front_kernel(*refs, rope, state_layer, state_start):
    refs = list(refs)
    x_ref, mod_ref, w_ref, wt_ref, qn_ref, kn_ref, sg_ref, sb_ref = refs[:8]
    refs = refs[8:]
    if rope:
        cos_ref, sin_ref = refs[:2]
        refs = refs[2:]
    with_state = state_layer is not None
    if with_state and not state_start:
        refs = refs[4:]
    qat_ref, qbt_ref, ka_ref, kb_ref, vat_ref, vbt_ref, pz_ref, stat_ref = refs[:8]
    st_refs = refs[8:] if with_state else None

    h = _modulated_norm(x_ref[...], mod_ref[0]).astype(BF16)

    def rotary_t(t):
        if not rope:
            return t
        x1, x2 = t[:HALF], t[HALF:]
        c, s = cos_ref[...], sin_ref[...]
        return jnp.concatenate([x1 * c - x2 * s, x2 * c + x1 * s], axis=0)

    def rms_t(t, gain):
        ms = jnp.mean(t * t, axis=0, keepdims=True)
        return t * lax.rsqrt(ms + EPS) * gain

    def proj_t(row0, rows):
        return _dot_nt(wt_ref[row0:row0 + rows, :], h)

    def proj(lo):
        return _dot(h, w_ref[:, lo:lo + WIDTH])

    def heads_of(t):
        return [t[j * HEAD_DIM:(j + 1) * HEAD_DIM] for j in range(t.shape[0] // HEAD_DIM)]

    def store(block, val):
        pz_ref[:, block * WIDTH:(block + 1) * WIDTH] = val.astype(BF16)

    def max_sq_norm(heads):
        m = None
        for t in heads:
            r = jnp.sum(t * t, axis=0, keepdims=True)
            m = r if m is None else jnp.maximum(m, r)
        return jnp.max(m, axis=1, keepdims=True)

    qn, kn = qn_ref[...], kn_ref[...]
    qa_t = proj_t(T_QA, WIDTH)
    ga = proj(_GA)
    qa = [rotary_t(rms_t(t, qn)) * Q_SCALE for t in heads_of(qa_t)]
    qat_ref[...] = jnp.concatenate(qa, axis=0).astype(BF16)
    qb_t = proj_t(T_QB, WIDTH)
    store(PZ_GA, _silu(ga))
    gb = proj(_GB)
    qb = [rotary_t(t) * Q_SCALE for t in heads_of(qb_t)]
    qbt_ref[...] = jnp.concatenate(qb, axis=0).astype(BF16)
    kva_t = proj_t(T_KA, 2 * KV_WIDTH)
    kvb_t = proj_t(T_KB, 2 * KV_WIDTH)
    store(PZ_GB, _silu(gb))
    uc = proj(_UC)

    ka = [rms_t(t, kn) for t in heads_of(kva_t[:KV_WIDTH])]
    kb = heads_of(kvb_t[:KV_WIDTH])
    va, vb = kva_t[KV_WIDTH:], kvb_t[KV_WIDTH:]
    ka_ref[...] = jnp.concatenate([rotary_t(t) for t in ka], axis=0).T.astype(BF16)
    kb_ref[...] = jnp.concatenate([rotary_t(t) for t in kb], axis=0).T.astype(BF16)
    vat_ref[...] = va.astype(BF16)
    if len(vbt_ref.shape) == 3:
        _write_token_blocks(vbt_ref, vb)
    else:
        vbt_ref[...] = vb.astype(BF16)
    stats = [max_sq_norm([t]) for t in ka + kb] + [max_sq_norm(qa), max_sq_norm(qb)]
    for r, v in enumerate(stats):
        stat_ref[r:r + 1, :] = jnp.broadcast_to(v, (1, LANES))
    stat_ref[len(stats):, :] = jnp.zeros((STAT_ROWS - len(stats), LANES), F32)
    if with_state:
        for st_ref, t in zip(st_refs, (jnp.concatenate(ka, axis=0), va, jnp.concatenate(kb, axis=0), vb)):
            if state_start:
                for d in range(st_ref.shape[1]):
                    if d != state_layer:
                        st_ref[:, d] = jnp.zeros(st_ref.shape[:1] + st_ref.shape[2:], F32)
                st_ref = st_ref.at[:, state_layer]
            _write_token_blocks(st_ref, t)

    store(PZ_UC, uc)
    vc = proj(_VC)
    gc = proj(_GC)
    mu = jnp.mean(vc, axis=-1, keepdims=True)
    vcc = vc - mu
    var = jnp.mean(vcc * vcc, axis=-1, keepdims=True)
    store(PZ_VN, vcc * lax.rsqrt(var + EPS) * sg_ref[...] + sb_ref[...])
    store(PZ_GC, _silu(gc))


def _front(x, mod, w_in, wt, qn, kn, sg, sb, rope_tables, states, *, layer, tm, seq):
    with_state = states is not None
    state_start = with_state and not states
    m = x.shape[0]
    n_seq = m // seq
    rope = rope_tables is not None
    const = lambda i: (0, 0)
    tps = max(seq // tm, 1)
    mod_idx = (lambda i: (0, 0, 0)) if tm >= seq else (lambda i: (i // tps, 0, 0))
    in_specs = [
        pl.BlockSpec((tm, D_MODEL), lambda i: (i, 0)),
        pl.BlockSpec((1, 1, 3 * D_MODEL), mod_idx),
        _layer_block((D_MODEL, IN_WIDTH), layer),
        _layer_block((T_ROWS, D_MODEL), layer),
        pl.BlockSpec((HEAD_DIM, tm), const),
        pl.BlockSpec((HEAD_DIM, tm), const),
        pl.BlockSpec((1, WIDTH), const),
        pl.BlockSpec((1, WIDTH), const),
    ]
    args = [x, mod, w_in, wt, qn, kn, sg, sb]
    if rope:
        in_specs += [pl.BlockSpec((HALF, tm), lambda i: (0, i % tps))] * 2
        args += list(rope_tables)
    qt_spec = pl.BlockSpec((WIDTH, tm), lambda i: (0, i))
    k_spec = pl.BlockSpec((tm, KV_WIDTH), lambda i: (i, 0))
    vt_shape = (KV_WIDTH, m)
    vt_spec = pl.BlockSpec((KV_WIDTH, tm), lambda i: (0, i))
    if rope:
        vbt_shape = (m // LANES, KV_WIDTH, LANES)
        vbt_spec = pl.BlockSpec((tm // LANES, KV_WIDTH, LANES), lambda i: (i, 0, 0))
    else:
        vbt_shape, vbt_spec = vt_shape, vt_spec
    out_specs = [qt_spec, qt_spec, k_spec, k_spec, vt_spec, vbt_spec,
                 pl.BlockSpec((tm, PZ_WIDTH), lambda i: (i, 0)),
                 pl.BlockSpec((STAT_ROWS, LANES), lambda i: (i, 0))]
    out_shape = [jax.ShapeDtypeStruct((WIDTH, m), BF16)] * 2
    out_shape += [jax.ShapeDtypeStruct((m, KV_WIDTH), BF16)] * 2
    out_shape += [jax.ShapeDtypeStruct(vt_shape, BF16), jax.ShapeDtypeStruct(vbt_shape, BF16)]
    out_shape += [jax.ShapeDtypeStruct((m, PZ_WIDTH), BF16), jax.ShapeDtypeStruct((m // tm * STAT_ROWS, LANES), F32)]
    aliases = {}
    if state_start:
        out_specs += [pl.BlockSpec((tm // seq, DEPTH, KV_WIDTH, seq), lambda i: (i, 0, 0, 0))] * 4
    elif with_state:
        aliases = {len(args) + j: len(out_shape) + j for j in range(4)}
        in_specs += [pl.BlockSpec(memory_space=pl.ANY)] * 4
        args += list(states)
        out_specs += [pl.BlockSpec((tm // seq, None, KV_WIDTH, seq), lambda i: (i, layer, 0, 0))] * 4
    if with_state:
        out_shape += [jax.ShapeDtypeStruct((n_seq, DEPTH, KV_WIDTH, seq), F32)] * 4
    return pl.pallas_call(
        functools.partial(_front_kernel, rope=rope, state_layer=layer if with_state else None,
                          state_start=state_start),
        grid=(m // tm,),
        in_specs=in_specs,
        out_specs=out_specs,
        out_shape=out_shape,
        input_output_aliases=aliases,
        compiler_params=pltpu.CompilerParams(
            dimension_semantics=("arbitrary",), vmem_limit_bytes=VMEM_LIMIT),
        name="front_latent" if rope else "front_context",
    )(*args)


def _scores_t(q_heads_t, sources, kv_head):
    top = jnp.concatenate(q_heads_t, axis=1) if len(q_heads_t) > 1 else q_heads_t[0]
    zeros = jnp.zeros_like(top)
    q_aug = jnp.concatenate([top, zeros] if kv_head == 0 else [zeros, top], axis=0)
    scores = []
    for k, _, mask in sources:
        s = _dot(k(), q_aug)
        if mask is not None:
            s = jnp.where(mask(), s, NEG_INF)
        scores.append(s)
    return scores


def _softmax_pv_t(scores, sources, sinks, kv_head, tq, shift=None):
    m = shift
    if m is None:
        for s in scores:
            ms = jnp.max(s, axis=0, keepdims=True)
            m = ms if m is None else jnp.maximum(m, ms)
    if sinks is not None:
        sink = jnp.concatenate([jnp.full((1, tq), sk * LOG2E, F32) for sk in sinks], axis=1)
        m = jnp.maximum(m, sink)
    rows = slice(kv_head * HEAD_DIM, (kv_head + 1) * HEAD_DIM)
    r = None
    for s, src in zip(scores, sources):
        v_t = src[1](rows)
        v_aug = jnp.concatenate([v_t, jnp.ones((BF16_SUBLANES, v_t.shape[1]), BF16)], axis=0)
        pv = _dot(v_aug, jnp.exp2(s - m).astype(BF16))
        r = pv if r is None else r + pv
    denom = r[HEAD_DIM:HEAD_DIM + 1]
    if sinks is not None:
        denom = denom + jnp.exp2(sink - m)
    return r[:HEAD_DIM] * (1.0 / denom)


def _attention_branches(branches, pz_ref, z_ref, heads_per_dot, tail):
    tq = pz_ref.shape[0]
    items = [(br, kv_head, h0) for br in range(len(branches)) for kv_head in range(N_KV)
             for h0 in range(kv_head * GROUP, (kv_head + 1) * GROUP, heads_per_dot)]

    def q_heads_of(item):
        br, _, h0 = item
        qt_ref = branches[br][0]
        return [qt_ref[h * HEAD_DIM:(h + 1) * HEAD_DIM, :] for h in range(h0, h0 + heads_per_dot)]

    def scores_of(item):
        return _scores_t(q_heads_of(item), branches[item[0]][1], item[1])

    shifts = [SHIFT_MARGIN * jnp.sqrt(jnp.full((1, heads_per_dot * tq), branches[br][5][kv_head], F32))
              for br, kv_head, _ in items]
    worst = None
    for branch in branches:
        for sq in branch[5]:
            worst = sq if worst is None else jnp.maximum(worst, sq)
    bounded = worst * SHIFT_MARGIN ** 2 <= MAX_SAFE_SHIFT ** 2

    def run(use_shift):
        pending = scores_of(items[0])
        for i, (br, kv_head, h0) in enumerate(items):
            scores = pending
            if i + 1 < len(items):
                pending = scores_of(items[i + 1])
            _, sources, sink_ref, gate_block, z_block, _ = branches[br]
            sinks = None if sink_ref is None else [sink_ref[h] for h in range(h0, h0 + heads_per_dot)]
            o_t = _softmax_pv_t(scores, sources, sinks, kv_head, tq, shifts[i] if use_shift else None)
            for j in range(0, heads_per_dot, 2):
                pair = (h0 + j) // 2
                o = o_t[:, j * tq:(j + 2) * tq]
                o = jnp.concatenate([o[:, :tq], o[:, tq:]], axis=0).T
                gate = pz_ref[:, gate_block * WIDTH + pair * LANES: gate_block * WIDTH + (pair + 1) * LANES]
                z_ref[:, z_block * WIDTH + pair * LANES: z_block * WIDTH + (pair + 1) * LANES] = (
                    o * gate.astype(F32)).astype(BF16)
        tail()

    pl.when(bounded)(functools.partial(run, True))
    pl.when(jnp.logical_not(bounded))(functools.partial(run, False))


def _spatial_gate(pz_ref, ws_ref, bs_ref, z_ref, n_rows):
    for ch0 in range(0, n_rows // CHUNK, 2):
        chunks = [slice(ch * CHUNK, (ch + 1) * CHUNK) for ch in (ch0, ch0 + 1)]
        for g in range(C_GROUPS):
            cols = lambda blk: slice(blk * WIDTH + g * LANES, blk * WIDTH + (g + 1) * LANES)
            vn = jnp.concatenate([pz_ref[rows, cols(PZ_VN)] for rows in chunks], axis=1)
            mixed2 = _dot(ws_ref[g], vn)
            for j, rows in enumerate(chunks):
                mixed = mixed2[:, j * LANES:(j + 1) * LANES] + bs_ref[:, g * LANES:(g + 1) * LANES]
                zc = pz_ref[rows, cols(PZ_UC)].astype(F32) * mixed * pz_ref[rows, cols(PZ_GC)].astype(F32)
                z_ref[rows, 2 * WIDTH + g * LANES: 2 * WIDTH + (g + 1) * LANES] = zc.astype(BF16)


def _ref_source(k_ref, vt_ref):
    return (lambda: k_ref[...], lambda rows: vt_ref[rows, :], None)


def _mixer_context_kernel(qat_ref, qbt_ref, ka_ref, kb_ref, vat_ref, vbt_ref, pz_ref, stat_ref,
                          sink_ref, ws_ref, bs_ref, z_ref, *, tokens_per_stat):
    t = lax.div(pl.program_id(0) * pz_ref.shape[0], tokens_per_stat)
    sq = lambda br: [stat_ref[t, br * N_KV + g] * stat_ref[t, STAT_Q + br] for g in range(N_KV)]
    _attention_branches(
        [(qat_ref, [_ref_source(ka_ref, vat_ref)], None, PZ_GA, 0, sq(0)),
         (qbt_ref, [_ref_source(kb_ref, vbt_ref)], sink_ref, PZ_GB, 1, sq(1))],
        pz_ref, z_ref, GROUP,
        functools.partial(_spatial_gate, pz_ref, ws_ref, bs_ref, z_ref, pz_ref.shape[0]))


def _mixer_context(qat, qbt, ka, kb, vat, vbt, pz, stats, sink, ws, bs, *, layer, seq, tokens_per_stat):
    m = pz.shape[0]
    qt_spec = pl.BlockSpec((WIDTH, seq), lambda i: (0, i))
    k_spec = pl.BlockSpec((seq, KV_WIDTH), lambda i: (i, 0))
    vt_spec = pl.BlockSpec((KV_WIDTH, seq), lambda i: (0, i))
    return pl.pallas_call(
        functools.partial(_mixer_context_kernel, tokens_per_stat=tokens_per_stat),
        grid=(m // seq,),
        in_specs=[
            qt_spec, qt_spec, k_spec, k_spec, vt_spec, vt_spec,
            pl.BlockSpec((seq, PZ_WIDTH), lambda i: (i, 0)),
            pl.BlockSpec(memory_space=pltpu.SMEM),
            pl.BlockSpec(memory_space=pltpu.SMEM),
            _layer_block((C_GROUPS, CHUNK, CHUNK), layer),
            pl.BlockSpec((CHUNK, WIDTH), lambda i: (0, 0)),
        ],
        out_specs=pl.BlockSpec((seq, 3 * WIDTH), lambda i: (i, 0)),
        out_shape=jax.ShapeDtypeStruct((m, 3 * WIDTH), BF16),
        compiler_params=pltpu.CompilerParams(
            dimension_semantics=("arbitrary",), vmem_limit_bytes=VMEM_LIMIT),
        name="mixer_context",
    )(qat, qbt, ka, kb, vat, vbt, pz, stats, sink, ws, bs)


def _mixer_latent_kernel(qat_ref, qbt_ref, ka_ref, kb_ref, vat_ref, vbt_ref, pz_ref, stat_ref,
                         cak_ref, cav_ref, cbk_ref, cbv_ref, sink_ref, ws_ref, bs_ref,
                         z_ref, cak_s, cav_s, cbk_s, cbv_s, ck2_s, *, seq, heads_per_dot, tokens_per_stat):
    tq = pz_ref.shape[0]
    band = tq + 2 * WINDOW
    qi = pl.program_id(1)

    @pl.when(qi == 0)
    def _():
        cak_s[...] = cak_ref[0, 0].T.astype(BF16)
        cbk_s[...] = cbk_ref[0, 0].T.astype(BF16)
        cav_s[...] = cav_ref[0, 0].astype(BF16)
        cbv_s[...] = cbv_ref[0, 0].astype(BF16)
        for br, cache_ref in enumerate((cak_ref, cbk_ref)):
            ck = cache_ref[0, 0]
            for g in range(N_KV):
                c2 = jnp.sum(jnp.square(ck[g * HEAD_DIM:(g + 1) * HEAD_DIM]), axis=0, keepdims=True)
                ck2_s[br * N_KV + g] = jnp.max(c2)

    start_blk = jnp.clip(qi * (tq // LANES) - WINDOW // LANES, 0, (seq - band) // LANES)
    start = pl.multiple_of(start_blk * LANES, LANES)

    def in_window():
        k_pos = start + lax.broadcasted_iota(jnp.int32, (band, heads_per_dot * tq), 0)
        q_pos = qi * tq + (lax.broadcasted_iota(jnp.int32, (band, heads_per_dot * tq), 1) & (tq - 1))
        return jnp.abs(q_pos - k_pos) <= WINDOW

    window = (lambda: kb_ref[pl.ds(start, band), :],
              lambda rows: jnp.concatenate(
                  [vbt_ref[start_blk + j, rows, :] for j in range(band // LANES)], axis=1),
              in_window)

    def sq(br):
        b = pl.program_id(0)
        tps = seq // tokens_per_stat
        q2 = stat_ref[b * tps + lax.div(qi * tq, tokens_per_stat), STAT_Q + br]
        out = []
        for g in range(N_KV):
            k2 = ck2_s[br * N_KV + g]
            for j in range(tps):
                k2 = jnp.maximum(k2, stat_ref[b * tps + j, br * N_KV + g])
            out.append(q2 * k2)
        return out

    _attention_branches(
        [(qat_ref, [_ref_source(ka_ref, vat_ref), _ref_source(cak_s, cav_s)], None, PZ_GA, 0,
          sq(0)),
         (qbt_ref, [window, _ref_source(cbk_s, cbv_s)],
          sink_ref, PZ_GB, 1, sq(1))],
        pz_ref, z_ref, heads_per_dot,
        functools.partial(_spatial_gate, pz_ref, ws_ref, bs_ref, z_ref, tq))


def _mixer_latent(qat, qbt, ka, kb, vat, vbt, pz, stats, cak, cav, cbk, cbv, sink, ws, bs, *, layer, seq, tq,
                  tokens_per_stat):
    m = pz.shape[0]
    n_seq = m // seq
    nq = seq // tq
    past = cak.shape[3]
    qt_spec = pl.BlockSpec((WIDTH, tq), lambda b, q: (0, b * nq + q))
    k_spec = pl.BlockSpec((seq, KV_WIDTH), lambda b, q: (b, 0))
    cache_spec = pl.BlockSpec((1, 1, KV_WIDTH, past), lambda b, q: (b, layer, 0, 0))
    return pl.pallas_call(
        functools.partial(_mixer_latent_kernel, seq=seq, heads_per_dot=2, tokens_per_stat=tokens_per_stat),
        grid=(n_seq, nq),
        in_specs=[
            qt_spec, qt_spec, k_spec, k_spec,
            pl.BlockSpec((KV_WIDTH, seq), lambda b, q: (0, b)),
            pl.BlockSpec((seq // LANES, KV_WIDTH, LANES), lambda b, q: (b, 0, 0)),
            pl.BlockSpec((tq, PZ_WIDTH), lambda b, q: (b * nq + q, 0)),
            pl.BlockSpec(memory_space=pltpu.SMEM),
            cache_spec, cache_spec, cache_spec, cache_spec,
            pl.BlockSpec(memory_space=pltpu.SMEM),
            _layer_block((C_GROUPS, CHUNK, CHUNK), layer),
            pl.BlockSpec((CHUNK, WIDTH), lambda b, q: (0, 0)),
        ],
        out_specs=pl.BlockSpec((tq, 3 * WIDTH), lambda b, q: (b * nq + q, 0)),
        out_shape=jax.ShapeDtypeStruct((m, 3 * WIDTH), BF16),
        scratch_shapes=[pltpu.VMEM((past, KV_WIDTH), BF16), pltpu.VMEM((KV_WIDTH, past), BF16),
                        pltpu.VMEM((past, KV_WIDTH), BF16), pltpu.VMEM((KV_WIDTH, past), BF16),
                        pltpu.SMEM((2 * N_KV,), F32)],
        compiler_params=pltpu.CompilerParams(
            dimension_semantics=("arbitrary", "arbitrary"), vmem_limit_bytes=VMEM_LIMIT),
        name="mixer_latent",
    )(qat, qbt, ka, kb, vat, vbt, pz, stats, cak, cav, cbk, cbv, sink, ws, bs)


def _back_kernel(x_ref, z_ref, mod_ref, wg_ref, bg_ref, wpa_ref, wpb_ref, wpc_ref, wo_ref,
                 lg_ref, lb_ref, o_ref, *, n_sub):
    mod = mod_ref[0]
    sub = x_ref.shape[0] // n_sub
    wp_refs = (wpa_ref, wpb_ref, wpc_ref)

    def stages(s):
        rows = slice(s * sub, (s + 1) * sub)
        st = {}

        def norm():
            st["h"] = _modulated_norm(x_ref[rows, :], mod).astype(BF16)

        def branch(br):
            cols = slice(br * D_MODEL, (br + 1) * D_MODEL)
            g = jax.nn.sigmoid(_dot(st["h"], wg_ref[:, cols]) + bg_ref[:, cols])
            y = _dot(z_ref[rows, br * WIDTH:(br + 1) * WIDTH], wp_refs[br][...])
            st["mix"] = g * y if br == 0 else st["mix"] + g * y

        def out_proj():
            st["y"] = _dot(st["mix"].astype(BF16), wo_ref[...])

        def residual_norm():
            r = DEEPNORM_ALPHA * x_ref[rows, :] + mod[:, 2 * D_MODEL:] * st["y"]
            mu = jnp.mean(r, axis=-1, keepdims=True)
            rc = r - mu
            var = jnp.mean(rc * rc, axis=-1, keepdims=True)
            o_ref[rows, :] = rc * lax.rsqrt(var + EPS) * lg_ref[...] + lb_ref[...]

        return [norm] + [functools.partial(branch, br) for br in range(3)] + [out_proj, residual_norm]

    chains = [stages(s) for s in range(n_sub)]
    n_stage = len(chains[0])
    for t in range(n_stage + (n_sub - 1) * BACK_STAGE_LAG):
        for s in range(n_sub):
            k = t - s * BACK_STAGE_LAG
            if 0 <= k < n_stage:
                chains[s][k]()


def _back(x, z, mod, wg, bg, wpa, wpb, wpc, wo, lg, lb, *, layer, tm, rows_per_mod, name):
    m = x.shape[0]
    tiles_per_mod = rows_per_mod // tm
    return pl.pallas_call(
        functools.partial(_back_kernel, n_sub=BACK_SUB_TILES),
        grid=(m // tm,),
        in_specs=[
            pl.BlockSpec((tm, D_MODEL), lambda i: (i, 0)),
            pl.BlockSpec((tm, 3 * WIDTH), lambda i: (i, 0)),
            pl.BlockSpec((1, 1, 3 * D_MODEL), lambda i: (i // tiles_per_mod, 0, 0)),
            _layer_block((D_MODEL, 3 * D_MODEL), layer),
            _layer_block((1, 3 * D_MODEL), layer),
            _layer_block((WIDTH, D_MODEL), layer),
            _layer_block((WIDTH, D_MODEL), layer),
            _layer_block((WIDTH, D_MODEL), layer),
            _layer_block((D_MODEL, D_MODEL), layer),
            _layer_block((1, D_MODEL), layer),
            _layer_block((1, D_MODEL), layer),
        ],
        out_specs=pl.BlockSpec((tm, D_MODEL), lambda i: (i, 0)),
        out_shape=jax.ShapeDtypeStruct((m, D_MODEL), F32),
        compiler_params=pltpu.CompilerParams(
            dimension_semantics=("arbitrary",), vmem_limit_bytes=VMEM_LIMIT),
        name=name,
    )(x, z, mod, wg, bg, wpa, wpb, wpc, wo, lg, lb)


def _rope_tables(n_tokens):
    rows = n_tokens // GRID_W
    row = jnp.repeat(jnp.arange(rows, dtype=F32), GRID_W)
    col = jnp.tile(jnp.arange(GRID_W, dtype=F32), rows)
    inv_freq = jnp.power(ROPE_BASE, -jnp.arange(ROPE_FREQS, dtype=F32) / ROPE_FREQS)
    ang = jnp.concatenate([inv_freq[:, None] * row[None], inv_freq[:, None] * col[None]], axis=0)
    return jnp.cos(ang), jnp.sin(ang)


def _tile_stats(stats):
    return stats.reshape(-1, STAT_ROWS, LANES)[:, :, 0]


def _transposed_cache(t):
    b, depth, past = t.shape[:3]
    return jnp.transpose(t, (0, 1, 3, 4, 2)).reshape(b, depth, KV_WIDTH, past)


def kernel(x_prompt, x_sample, cache_a_k, cache_a_v, cache_b_k, cache_b_v, c, c_ctx, w_mod, b_mod,
           w_in, q_norm, k_norm, sink, sgu_g, sgu_b, w_spatial, b_spatial, w_proj_a, w_proj_b,
           w_proj_c, w_gate, b_gate, w_out, ln_g, ln_b):
    batch, seq, _ = x_prompt.shape
    dec_batch, dec_seq, _ = x_sample.shape
    tm = 512

    cvecs = jnp.concatenate(
        [c_ctx[None], c, jnp.zeros((MOD_ROWS - 1 - dec_batch, D_MODEL), F32)], axis=0)
    mod = _modulation(cvecs, w_mod, b_mod)

    rope_tables = _rope_tables(dec_seq)
    caches = [_transposed_cache(t) for t in (cache_a_k, cache_a_v, cache_b_k, cache_b_v)]

    w_in_bf = lax.optimization_barrier(w_in.astype(BF16))
    wt_bf = jnp.swapaxes(jnp.concatenate(
        [w_in_bf[:, :, lo:lo + width] for lo, width in _T_SEGMENTS], axis=-1), 1, 2)
    wg_bf, wo_bf, ws_bf = w_gate.astype(BF16), w_out.astype(BF16), w_spatial.astype(BF16)
    wpa_bf, wpb_bf, wpc_bf = w_proj_a.astype(BF16), w_proj_b.astype(BF16), w_proj_c.astype(BF16)
    bg3, lg3, lb3 = b_gate[:, None], ln_g[:, None], ln_b[:, None]

    xp = x_prompt.reshape(batch * seq, D_MODEL)
    xs = x_sample.reshape(dec_batch * dec_seq, D_MODEL)
    states = ()
    for l in range(DEPTH):
        bs = jnp.repeat(b_spatial[l].T, LANES, axis=1)
        qn = jnp.broadcast_to(q_norm[l][:, None], (HEAD_DIM, tm))
        kn = jnp.broadcast_to(k_norm[l][:, None], (HEAD_DIM, tm))
        sg, sb = sgu_g[l][None], sgu_b[l][None]
        mod_ctx = mod[l, 0:1][None]
        mod_lat = mod[l, 1:1 + dec_batch][:, None]

        *acts, st_ak, st_av, st_bk, st_bv = _front(
            xp, mod_ctx, w_in_bf, wt_bf, qn, kn, sg, sb, None, states, layer=l, tm=tm, seq=seq)
        states = [st_ak, st_av, st_bk, st_bv]
        z = _mixer_context(*acts[:-1], _tile_stats(acts[-1]), sink[l], ws_bf, bs,
                           layer=l, seq=seq, tokens_per_stat=tm)
        xp = _back(xp, z, mod_ctx, wg_bf, bg3, wpa_bf, wpb_bf, wpc_bf, wo_bf, lg3, lb3,
                   layer=l, tm=2 * tm, rows_per_mod=batch * seq, name="back_context")

        acts = _front(xs, mod_lat, w_in_bf, wt_bf, qn, kn, sg, sb, rope_tables, None,
                      layer=l, tm=tm, seq=dec_seq)
        z = _mixer_latent(*acts[:-1], _tile_stats(acts[-1]), *caches, sink[l], ws_bf, bs,
                          layer=l, seq=dec_seq, tq=256, tokens_per_stat=tm)
        xs = _back(xs, z, mod_lat, wg_bf, bg3, wpa_bf, wpb_bf, wpc_bf, wo_bf, lg3, lb3,
                   layer=l, tm=2 * tm, rows_per_mod=dec_seq, name="back_latent")

    def state(t):
        return jnp.transpose(t.reshape(batch, DEPTH, N_KV, HEAD_DIM, seq), (0, 1, 4, 2, 3))

    return (xp.reshape(batch, seq, D_MODEL), xs.reshape(dec_batch, dec_seq, D_MODEL),
            *[state(t) for t in states])
```

```python
import functools

import jax
import jax.numpy as jnp
from jax import lax
from jax.experimental import pallas as pl
from jax.experimental.pallas import tpu as pltpu

D_MODEL = 1024
DEPTH = 4
GRID_W = 64
HEAD_DIM = 64
HALF = HEAD_DIM // 2
N_HEADS = 8
N_KV = 2
GROUP = N_HEADS // N_KV
KV_WIDTH = N_KV * HEAD_DIM
WIDTH = 512
C_GROUPS = 4
CHUNK = 128
WINDOW = 128
IN_WIDTH = 4096
ROPE_BASE = 10000.0
ROPE_FREQS = HEAD_DIM // 4
EPS = 1e-6
NEG_INF = -1e30
DEEPNORM_ALPHA = (2 * DEPTH) ** 0.25
LOG2E = 1.4426950408889634
Q_SCALE = HEAD_DIM ** -0.5 * LOG2E

LANES = 128
BF16_SUBLANES = 16
MOD_ROWS = 8

_QA, _KA, _VA, _GA = 0, 512, 640, 768
_QB, _KB, _VB, _GB = 1280, 1792, 1920, 2048
_UC, _VC, _GC = 2560, 3072, 3584
_T_SEGMENTS = ((_QA, WIDTH), (_KA, KV_WIDTH), (_VA, KV_WIDTH), (_QB, WIDTH), (_KB, KV_WIDTH), (_VB, KV_WIDTH))
T_QA, T_KA, T_VA, T_QB, T_KB, T_VB = 0, 512, 640, 768, 1280, 1408
T_ROWS = 1536

(PZ_GA, PZ_GB, PZ_UC, PZ_VN, PZ_GC) = range(5)
PZ_WIDTH = 5 * WIDTH

BF16 = jnp.bfloat16
F32 = jnp.float32

VMEM_LIMIT = 56 * 1024 * 1024
SHIFT_MARGIN = 1.02
MAX_SAFE_SHIFT = 48.0
STAT_ROWS = 8
STAT_Q = 2 * N_KV
CTX_SEQS_PER_STEP = 2
BACK_SUB_TILES = 4
BACK_STAGE_LAG = 4


def _silu(t):
    return t * jax.nn.sigmoid(t)


def _dot(a, b):
    return jnp.dot(a, b, preferred_element_type=F32)


def _dot_nt(a, b):
    return lax.dot_general(a, b, (((1,), (1,)), ((), ())), preferred_element_type=F32)


def _layer_block(shape, layer):
    zeros = (0,) * len(shape)
    return pl.BlockSpec((None,) + tuple(shape), lambda *_: (layer,) + zeros,
                        pipeline_mode=pl.Buffered(1))


def _mod_kernel(c_ref, w_ref, b_ref, o_ref):
    s = _silu(c_ref[...]).astype(BF16)
    o_ref[0] = _dot(s, w_ref[0].astype(BF16)) + b_ref[0]


def _modulation(cvecs, w_mod, b_mod):
    tn = 1024
    return pl.pallas_call(
        _mod_kernel,
        grid=(DEPTH, 3 * D_MODEL // tn),
        in_specs=[
            pl.BlockSpec((MOD_ROWS, D_MODEL), lambda l, j: (0, 0)),
            pl.BlockSpec((1, D_MODEL, tn), lambda l, j: (l, 0, j)),
            pl.BlockSpec((1, 1, tn), lambda l, j: (l, 0, j)),
        ],
        out_specs=pl.BlockSpec((1, MOD_ROWS, tn), lambda l, j: (l, 0, j)),
        out_shape=jax.ShapeDtypeStruct((DEPTH, MOD_ROWS, 3 * D_MODEL), F32),
        compiler_params=pltpu.CompilerParams(
            dimension_semantics=("arbitrary", "arbitrary"), vmem_limit_bytes=VMEM_LIMIT),
        name="modulation",
    )(cvecs, w_mod, b_mod.reshape(DEPTH, 1, 3 * D_MODEL))


def _modulated_norm(x, mod):
    mu = jnp.mean(x, axis=-1, keepdims=True)
    xc = x - mu
    var = jnp.mean(xc * xc, axis=-1, keepdims=True)
    xn = xc * lax.rsqrt(var + EPS)
    return xn * (1.0 + mod[:, D_MODEL:2 * D_MODEL]) + mod[:, :D_MODEL]


def _write_token_blocks(ref, t):
    w = ref.shape[2]
    for j in range(ref.shape[0]):
        ref[j] = t[:, j * w:(j + 1) * w].astype(ref.dtype)


def _front_kernel(*refs, rope, state_layer, state_start):
    refs = list(refs)
    x_ref, mod_ref, w_ref, wt_ref, qn_ref, kn_ref, sg_ref, sb_ref = refs[:8]
    refs = refs[8:]
    if rope:
        cos_ref, sin_ref = refs[:2]
        refs = refs[2:]
    with_state = state_layer is not None
    if with_state and not state_start:
        refs = refs[4:]
    qat_ref, qbt_ref, ka_ref, kb_ref, vat_ref, vbt_ref, pz_ref, stat_ref = refs[:8]
    st_refs = refs[8:] if with_state else None

    h = _modulated_norm(x_ref[...], mod_ref[0]).astype(BF16)

    def rotary_t(t):
        if not rope:
            return t
        x1, x2 = t[:HALF], t[HALF:]
        c, s = cos_ref[...], sin_ref[...]
        return jnp.concatenate([x1 * c - x2 * s, x2 * c + x1 * s], axis=0)

    def rms_t(t, gain):
        ms = jnp.mean(t * t, axis=0, keepdims=True)
        return t * lax.rsqrt(ms + EPS) * gain

    def proj_t(row0, rows):
        return _dot_nt(wt_ref[row0:row0 + rows, :], h)

    def proj(lo):
        return _dot(h, w_ref[:, lo:lo + WIDTH])

    def heads_of(t):
        return [t[j * HEAD_DIM:(j + 1) * HEAD_DIM] for j in range(t.shape[0] // HEAD_DIM)]

    def store(block, val):
        pz_ref[:, block * WIDTH:(block + 1) * WIDTH] = val.astype(BF16)

    def max_sq_norm(heads):
        m = None
        for t in heads:
            r = jnp.sum(t * t, axis=0, keepdims=True)
            m = r if m is None else jnp.maximum(m, r)
        return jnp.max(m, axis=1, keepdims=True)

    qn, kn = qn_ref[...], kn_ref[...]
    qa_t = proj_t(T_QA, WIDTH)
    ga = proj(_GA)
    qa = [rotary_t(rms_t(t, qn)) * Q_SCALE for t in heads_of(qa_t)]
    qat_ref[...] = jnp.concatenate(qa, axis=0).astype(BF16)
    qb_t = proj_t(T_QB, WIDTH)
    store(PZ_GA, _silu(ga))
    gb = proj(_GB)
    qb = [rotary_t(t) * Q_SCALE for t in heads_of(qb_t)]
    qbt_ref[...] = jnp.concatenate(qb, axis=0).astype(BF16)
    kva_t = proj_t(T_KA, 2 * KV_WIDTH)
    kvb_t = proj_t(T_KB, 2 * KV_WIDTH)
    store(PZ_GB, _silu(gb))
    uc = proj(_UC)

    ka = [rms_t(t, kn) for t in heads_of(kva_t[:KV_WIDTH])]
    kb = heads_of(kvb_t[:KV_WIDTH])
    va, vb = kva_t[KV_WIDTH:], kvb_t[KV_WIDTH:]
    ka_ref[...] = jnp.concatenate([rotary_t(t) for t in ka], axis=0).T.astype(BF16)
    kb_ref[...] = jnp.concatenate([rotary_t(t) for t in kb], axis=0).T.astype(BF16)
    vat_ref[...] = va.astype(BF16)
    if len(vbt_ref.shape) == 3:
        _write_token_blocks(vbt_ref, vb)
    else:
        vbt_ref[...] = vb.astype(BF16)
    stats = [max_sq_norm([t]) for t in ka + kb] + [max_sq_norm(qa), max_sq_norm(qb)]
    for r, v in enumerate(stats):
        stat_ref[r:r + 1, :] = jnp.broadcast_to(v, (1, LANES))
    stat_ref[len(stats):, :] = jnp.zeros((STAT_ROWS - len(stats), LANES), F32)
    if with_state:
        for st_ref, t in zip(st_refs, (jnp.concatenate(ka, axis=0), va, jnp.concatenate(kb, axis=0), vb)):
            if state_start:
                for d in range(st_ref.shape[1]):
                    if d != state_layer:
                        st_ref[:, d] = jnp.zeros(st_ref.shape[:1] + st_ref.shape[2:], F32)
                st_ref = st_ref.at[:, state_layer]
            _write_token_blocks(st_ref, t)

    store(PZ_UC, uc)
    vc = proj(_VC)
    gc = proj(_GC)
    mu = jnp.mean(vc, axis=-1, keepdims=True)
    vcc = vc - mu
    var = jnp.mean(vcc * vcc, axis=-1, keepdims=True)
    store(PZ_VN, vcc * lax.rsqrt(var + EPS) * sg_ref[...] + sb_ref[...])
    store(PZ_GC, _silu(gc))


def _front(x, mod, w_in, wt, qn, kn, sg, sb, rope_tables, states, *, layer, tm, seq):
    with_state = states is not None
    state_start = with_state and not states
    m = x.shape[0]
    n_seq = m // seq
    rope = rope_tables is not None
    const = lambda i: (0, 0)
    tps = max(seq // tm, 1)
    mod_idx = (lambda i: (0, 0, 0)) if tm >= seq else (lambda i: (i // tps, 0, 0))
    in_specs = [
        pl.BlockSpec((tm, D_MODEL), lambda i: (i, 0)),
        pl.BlockSpec((1, 1, 3 * D_MODEL), mod_idx),
        _layer_block((D_MODEL, IN_WIDTH), layer),
        _layer_block((T_ROWS, D_MODEL), layer),
        pl.BlockSpec((HEAD_DIM, tm), const),
        pl.BlockSpec((HEAD_DIM, tm), const),
        pl.BlockSpec((1, WIDTH), const),
        pl.BlockSpec((1, WIDTH), const),
    ]
    args = [x, mod, w_in, wt, qn, kn, sg, sb]
    if rope:
        in_specs += [pl.BlockSpec((HALF, tm), lambda i: (0, i % tps))] * 2
        args += list(rope_tables)
    qt_spec = pl.BlockSpec((WIDTH, tm), lambda i: (0, i))
    k_spec = pl.BlockSpec((tm, KV_WIDTH), lambda i: (i, 0))
    vt_shape = (KV_WIDTH, m)
    vt_spec = pl.BlockSpec((KV_WIDTH, tm), lambda i: (0, i))
    if rope:
        vbt_shape = (m // LANES, KV_WIDTH, LANES)
        vbt_spec = pl.BlockSpec((tm // LANES, KV_WIDTH, LANES), lambda i: (i, 0, 0))
    else:
        vbt_shape, vbt_spec = vt_shape, vt_spec
    out_specs = [qt_spec, qt_spec, k_spec, k_spec, vt_spec, vbt_spec,
                 pl.BlockSpec((tm, PZ_WIDTH), lambda i: (i, 0)),
                 pl.BlockSpec((STAT_ROWS, LANES), lambda i: (i, 0))]
    out_shape = [jax.ShapeDtypeStruct((WIDTH, m), BF16)] * 2
    out_shape += [jax.ShapeDtypeStruct((m, KV_WIDTH), BF16)] * 2
    out_shape += [jax.ShapeDtypeStruct(vt_shape, BF16), jax.ShapeDtypeStruct(vbt_shape, BF16)]
    out_shape += [jax.ShapeDtypeStruct((m, PZ_WIDTH), BF16), jax.ShapeDtypeStruct((m // tm * STAT_ROWS, LANES), F32)]
    aliases = {}
    if state_start:
        out_specs += [pl.BlockSpec((tm // seq, DEPTH, KV_WIDTH, seq), lambda i: (i, 0, 0, 0))] * 4
    elif with_state:
        aliases = {len(args) + j: len(out_shape) + j for j in range(4)}
        in_specs += [pl.BlockSpec(memory_space=pl.ANY)] * 4
        args += list(states)
        out_specs += [pl.BlockSpec((tm // seq, None, KV_WIDTH, seq), lambda i: (i, layer, 0, 0))] * 4
    if with_state:
        out_shape += [jax.ShapeDtypeStruct((n_seq, DEPTH, KV_WIDTH, seq), F32)] * 4
    return pl.pallas_call(
        functools.partial(_front_kernel, rope=rope, state_layer=layer if with_state else None,
                          state_start=state_start),
        grid=(m // tm,),
        in_specs=in_specs,
        out_specs=out_specs,
        out_shape=out_shape,
        input_output_aliases=aliases,
        compiler_params=pltpu.CompilerParams(
            dimension_semantics=("arbitrary",), vmem_limit_bytes=VMEM_LIMIT),
        name="front_latent" if rope else "front_context",
    )(*args)


def _scores_t(q_heads_t, sources, kv_head):
    top = jnp.concatenate(q_heads_t, axis=1) if len(q_heads_t) > 1 else q_heads_t[0]
    zeros = jnp.zeros_like(top)
    q_aug = jnp.concatenate([top, zeros] if kv_head == 0 else [zeros, top], axis=0)
    scores = []
    for k, _, mask in sources:
        s = _dot(k(), q_aug)
        if mask is not None:
            s = jnp.where(mask(), s, NEG_INF)
        scores.append(s)
    return scores


def _softmax_pv_t(scores, sources, sinks, kv_head, tq, shift=None):
    m = shift
    if m is None:
        for s in scores:
            ms = jnp.max(s, axis=0, keepdims=True)
            m = ms if m is None else jnp.maximum(m, ms)
    if sinks is not None:
        sink = jnp.concatenate([jnp.full((1, tq), sk * LOG2E, F32) for sk in sinks], axis=1)
        m = jnp.maximum(m, sink)
    rows = slice(kv_head * HEAD_DIM, (kv_head + 1) * HEAD_DIM)
    r = None
    for s, src in zip(scores, sources):
        v_t = src[1](rows)
        v_aug = jnp.concatenate([v_t, jnp.ones((BF16_SUBLANES, v_t.shape[1]), BF16)], axis=0)
        pv = _dot(v_aug, jnp.exp2(s - m).astype(BF16))
        r = pv if r is None else r + pv
    denom = r[HEAD_DIM:HEAD_DIM + 1]
    if sinks is not None:
        denom = denom + jnp.exp2(sink - m)
    return r[:HEAD_DIM] * (1.0 / denom)


def _attention_branches(branches, heads_per_dot, tail):
    tq = branches[0][6].shape[0]
    items = [(br, kv_head, h0) for br in range(len(branches)) for kv_head in range(N_KV)
             for h0 in range(kv_head * GROUP, (kv_head + 1) * GROUP, heads_per_dot)]

    def q_heads_of(item):
        br, _, h0 = item
        qt_ref = branches[br][0]
        return [qt_ref[h * HEAD_DIM:(h + 1) * HEAD_DIM, :] for h in range(h0, h0 + heads_per_dot)]

    def scores_of(item):
        return _scores_t(q_heads_of(item), branches[item[0]][1], item[1])

    shifts = [SHIFT_MARGIN * jnp.sqrt(jnp.full((1, heads_per_dot * tq), branches[br][5][kv_head], F32))
              for br, kv_head, _ in items]
    worst = None
    for branch in branches:
        for sq in branch[5]:
            worst = sq if worst is None else jnp.maximum(worst, sq)
    bounded = worst * SHIFT_MARGIN ** 2 <= MAX_SAFE_SHIFT ** 2

    def run(use_shift):
        pending = scores_of(items[0])
        for i, (br, kv_head, h0) in enumerate(items):
            scores = pending
            if i + 1 < len(items):
                pending = scores_of(items[i + 1])
            _, sources, sink_ref, gate_block, z_block, _, pz_ref, z_ref = branches[br]
            sinks = None if sink_ref is None else [sink_ref[h] for h in range(h0, h0 + heads_per_dot)]
            o_t = _softmax_pv_t(scores, sources, sinks, kv_head, tq, shifts[i] if use_shift else None)
            for j in range(0, heads_per_dot, 2):
                pair = (h0 + j) // 2
                o = o_t[:, j * tq:(j + 2) * tq]
                o = jnp.concatenate([o[:, :tq], o[:, tq:]], axis=0).T
                gate = pz_ref[:, gate_block * WIDTH + pair * LANES: gate_block * WIDTH + (pair + 1) * LANES]
                z_ref[:, z_block * WIDTH + pair * LANES: z_block * WIDTH + (pair + 1) * LANES] = (
                    o * gate.astype(F32)).astype(BF16)
        tail()

    pl.when(bounded)(functools.partial(run, True))
    pl.when(jnp.logical_not(bounded))(functools.partial(run, False))


def _spatial_gate(pz_ref, ws_ref, bs_ref, z_ref, n_rows):
    for ch0 in range(0, n_rows // CHUNK, 2):
        chunks = [slice(ch * CHUNK, (ch + 1) * CHUNK) for ch in (ch0, ch0 + 1)]
        for g in range(C_GROUPS):
            cols = lambda blk: slice(blk * WIDTH + g * LANES, blk * WIDTH + (g + 1) * LANES)
            vn = jnp.concatenate([pz_ref[rows, cols(PZ_VN)] for rows in chunks], axis=1)
            mixed2 = _dot(ws_ref[g], vn)
            for j, rows in enumerate(chunks):
                mixed = mixed2[:, j * LANES:(j + 1) * LANES] + bs_ref[:, g * LANES:(g + 1) * LANES]
                zc = pz_ref[rows, cols(PZ_UC)].astype(F32) * mixed * pz_ref[rows, cols(PZ_GC)].astype(F32)
                z_ref[rows, 2 * WIDTH + g * LANES: 2 * WIDTH + (g + 1) * LANES] = zc.astype(BF16)


def _ref_source(k_ref, vt_ref):
    return (lambda: k_ref[...], lambda rows: vt_ref[rows, :], None)


def _mixer_context_kernel(qat_ref, qbt_ref, ka_ref, kb_ref, vat_ref, vbt_ref, pz_ref, stat_ref,
                          sink_ref, ws_ref, bs_ref, z_ref, *, seq, tokens_per_stat):
    t = lax.div(pl.program_id(0) * pz_ref.shape[0], tokens_per_stat)
    sq = lambda br: [stat_ref[t, br * N_KV + g] * stat_ref[t, STAT_Q + br] for g in range(N_KV)]
    branches = []
    for s in range(pz_ref.shape[0] // seq):
        tok = slice(s * seq, (s + 1) * seq)
        views = (pz_ref.at[tok, :], z_ref.at[tok, :])
        branches += [
            (qat_ref.at[:, tok], [_ref_source(ka_ref.at[tok, :], vat_ref.at[:, tok])], None, PZ_GA, 0, sq(0)) + views,
            (qbt_ref.at[:, tok], [_ref_source(kb_ref.at[tok, :], vbt_ref.at[:, tok])], sink_ref, PZ_GB, 1, sq(1)) + views]
    _attention_branches(
        branches, GROUP, functools.partial(_spatial_gate, pz_ref, ws_ref, bs_ref, z_ref, pz_ref.shape[0]))


def _mixer_context(qat, qbt, ka, kb, vat, vbt, pz, stats, sink, ws, bs, *, layer, seq, tokens_per_stat):
    m = pz.shape[0]
    tok = CTX_SEQS_PER_STEP * seq
    qt_spec = pl.BlockSpec((WIDTH, tok), lambda i: (0, i))
    k_spec = pl.BlockSpec((tok, KV_WIDTH), lambda i: (i, 0))
    vt_spec = pl.BlockSpec((KV_WIDTH, tok), lambda i: (0, i))
    return pl.pallas_call(
        functools.partial(_mixer_context_kernel, seq=seq, tokens_per_stat=tokens_per_stat),
        grid=(m // tok,),
        in_specs=[
            qt_spec, qt_spec, k_spec, k_spec, vt_spec, vt_spec,
            pl.BlockSpec((tok, PZ_WIDTH), lambda i: (i, 0)),
            pl.BlockSpec(memory_space=pltpu.SMEM),
            pl.BlockSpec(memory_space=pltpu.SMEM),
            _layer_block((C_GROUPS, CHUNK, CHUNK), layer),
            pl.BlockSpec((CHUNK, WIDTH), lambda i: (0, 0)),
        ],
        out_specs=pl.BlockSpec((tok, 3 * WIDTH), lambda i: (i, 0)),
        out_shape=jax.ShapeDtypeStruct((m, 3 * WIDTH), BF16),
        compiler_params=pltpu.CompilerParams(
            dimension_semantics=("arbitrary",), vmem_limit_bytes=VMEM_LIMIT),
        name="mixer_context",
    )(qat, qbt, ka, kb, vat, vbt, pz, stats, sink, ws, bs)


def _mixer_latent_kernel(qat_ref, qbt_ref, ka_ref, kb_ref, vat_ref, vbt_ref, pz_ref, stat_ref,
                         cak_ref, cav_ref, cbk_ref, cbv_ref, sink_ref, ws_ref, bs_ref,
                         z_ref, cak_s, cav_s, cbk_s, cbv_s, ck2_s, *, seq, heads_per_dot, tokens_per_stat):
    tq = pz_ref.shape[0]
    band = tq + 2 * WINDOW
    qi = pl.program_id(1)

    @pl.when(qi == 0)
    def _():
        cak_s[...] = cak_ref[0, 0].T.astype(BF16)
        cbk_s[...] = cbk_ref[0, 0].T.astype(BF16)
        cav_s[...] = cav_ref[0, 0].astype(BF16)
        cbv_s[...] = cbv_ref[0, 0].astype(BF16)
        for br, cache_ref in enumerate((cak_ref, cbk_ref)):
            ck = cache_ref[0, 0]
            for g in range(N_KV):
                c2 = jnp.sum(jnp.square(ck[g * HEAD_DIM:(g + 1) * HEAD_DIM]), axis=0, keepdims=True)
                ck2_s[br * N_KV + g] = jnp.max(c2)

    start_blk = jnp.clip(qi * (tq // LANES) - WINDOW // LANES, 0, (seq - band) // LANES)
    start = pl.multiple_of(start_blk * LANES, LANES)

    def in_window():
        k_pos = start + lax.broadcasted_iota(jnp.int32, (band, heads_per_dot * tq), 0)
        q_pos = qi * tq + (lax.broadcasted_iota(jnp.int32, (band, heads_per_dot * tq), 1) & (tq - 1))
        return jnp.abs(q_pos - k_pos) <= WINDOW

    window = (lambda: kb_ref[pl.ds(start, band), :],
              lambda rows: jnp.concatenate(
                  [vbt_ref[start_blk + j, rows, :] for j in range(band // LANES)], axis=1),
              in_window)

    def sq(br):
        b = pl.program_id(0)
        tps = seq // tokens_per_stat
        q2 = stat_ref[b * tps + lax.div(qi * tq, tokens_per_stat), STAT_Q + br]
        out = []
        for g in range(N_KV):
            k2 = ck2_s[br * N_KV + g]
            for j in range(tps):
                k2 = jnp.maximum(k2, stat_ref[b * tps + j, br * N_KV + g])
            out.append(q2 * k2)
        return out

    _attention_branches(
        [(qat_ref, [_ref_source(ka_ref, vat_ref), _ref_source(cak_s, cav_s)], None, PZ_GA, 0,
          sq(0), pz_ref, z_ref),
         (qbt_ref, [window, _ref_source(cbk_s, cbv_s)],
          sink_ref, PZ_GB, 1, sq(1), pz_ref, z_ref)],
        heads_per_dot,
        functools.partial(_spatial_gate, pz_ref, ws_ref, bs_ref, z_ref, tq))


def _mixer_latent(qat, qbt, ka, kb, vat, vbt, pz, stats, cak, cav, cbk, cbv, sink, ws, bs, *, layer, seq, tq,
                  tokens_per_stat):
    m = pz.shape[0]
    n_seq = m // seq
    nq = seq // tq
    past = cak.shape[3]
    qt_spec = pl.BlockSpec((WIDTH, tq), lambda b, q: (0, b * nq + q))
    k_spec = pl.BlockSpec((seq, KV_WIDTH), lambda b, q: (b, 0))
    cache_spec = pl.BlockSpec((1, 1, KV_WIDTH, past), lambda b, q: (b, layer, 0, 0))
    return pl.pallas_call(
        functools.partial(_mixer_latent_kernel, seq=seq, heads_per_dot=2, tokens_per_stat=tokens_per_stat),
        grid=(n_seq, nq),
        in_specs=[
            qt_spec, qt_spec, k_spec, k_spec,
            pl.BlockSpec((KV_WIDTH, seq), lambda b, q: (0, b)),
            pl.BlockSpec((seq // LANES, KV_WIDTH, LANES), lambda b, q: (b, 0, 0)),
            pl.BlockSpec((tq, PZ_WIDTH), lambda b, q: (b * nq + q, 0)),
            pl.BlockSpec(memory_space=pltpu.SMEM),
            cache_spec, cache_spec, cache_spec, cache_spec,
            pl.BlockSpec(memory_space=pltpu.SMEM),
            _layer_block((C_GROUPS, CHUNK, CHUNK), layer),
            pl.BlockSpec((CHUNK, WIDTH), lambda b, q: (0, 0)),
        ],
        out_specs=pl.BlockSpec((tq, 3 * WIDTH), lambda b, q: (b * nq + q, 0)),
        out_shape=jax.ShapeDtypeStruct((m, 3 * WIDTH), BF16),
        scratch_shapes=[pltpu.VMEM((past, KV_WIDTH), BF16), pltpu.VMEM((KV_WIDTH, past), BF16),
                        pltpu.VMEM((past, KV_WIDTH), BF16), pltpu.VMEM((KV_WIDTH, past), BF16),
                        pltpu.SMEM((2 * N_KV,), F32)],
        compiler_params=pltpu.CompilerParams(
            dimension_semantics=("arbitrary", "arbitrary"), vmem_limit_bytes=VMEM_LIMIT),
        name="mixer_latent",
    )(qat, qbt, ka, kb, vat, vbt, pz, stats, cak, cav, cbk, cbv, sink, ws, bs)


def _back_kernel(x_ref, z_ref, mod_ref, wg_ref, bg_ref, wpa_ref, wpb_ref, wpc_ref, wo_ref,
                 lg_ref, lb_ref, o_ref, *, n_sub):
    mod = mod_ref[0]
    sub = x_ref.shape[0] // n_sub
    wp_refs = (wpa_ref, wpb_ref, wpc_ref)

    def stages(s):
        rows = slice(s * sub, (s + 1) * sub)
        st = {}

        def norm():
            st["h"] = _modulated_norm(x_ref[rows, :], mod).astype(BF16)

        def branch(br):
            cols = slice(br * D_MODEL, (br + 1) * D_MODEL)
            g = jax.nn.sigmoid(_dot(st["h"], wg_ref[:, cols]) + bg_ref[:, cols])
            y = _dot(z_ref[rows, br * WIDTH:(br + 1) * WIDTH], wp_refs[br][...])
            st["mix"] = g * y if br == 0 else st["mix"] + g * y

        def out_proj():
            st["y"] = _dot(st["mix"].astype(BF16), wo_ref[...])

        def residual_norm():
            r = DEEPNORM_ALPHA * x_ref[rows, :] + mod[:, 2 * D_MODEL:] * st["y"]
            mu = jnp.mean(r, axis=-1, keepdims=True)
            rc = r - mu
            var = jnp.mean(rc * rc, axis=-1, keepdims=True)
            o_ref[rows, :] = rc * lax.rsqrt(var + EPS) * lg_ref[...] + lb_ref[...]

        return [norm] + [functools.partial(branch, br) for br in range(3)] + [out_proj, residual_norm]

    chains = [stages(s) for s in range(n_sub)]
    n_stage = len(chains[0])
    for t in range(n_stage + (n_sub - 1) * BACK_STAGE_LAG):
        for s in range(n_sub):
            k = t - s * BACK_STAGE_LAG
            if 0 <= k < n_stage:
                chains[s][k]()


def _back(x, z, mod, wg, bg, wpa, wpb, wpc, wo, lg, lb, *, layer, tm, rows_per_mod, name):
    m = x.shape[0]
    tiles_per_mod = rows_per_mod // tm
    return pl.pallas_call(
        functools.partial(_back_kernel, n_sub=BACK_SUB_TILES),
        grid=(m // tm,),
        in_specs=[
            pl.BlockSpec((tm, D_MODEL), lambda i: (i, 0)),
            pl.BlockSpec((tm, 3 * WIDTH), lambda i: (i, 0)),
            pl.BlockSpec((1, 1, 3 * D_MODEL), lambda i: (i // tiles_per_mod, 0, 0)),
            _layer_block((D_MODEL, 3 * D_MODEL), layer),
            _layer_block((1, 3 * D_MODEL), layer),
            _layer_block((WIDTH, D_MODEL), layer),
            _layer_block((WIDTH, D_MODEL), layer),
            _layer_block((WIDTH, D_MODEL), layer),
            _layer_block((D_MODEL, D_MODEL), layer),
            _layer_block((1, D_MODEL), layer),
            _layer_block((1, D_MODEL), layer),
        ],
        out_specs=pl.BlockSpec((tm, D_MODEL), lambda i: (i, 0)),
        out_shape=jax.ShapeDtypeStruct((m, D_MODEL), F32),
        compiler_params=pltpu.CompilerParams(
            dimension_semantics=("arbitrary",), vmem_limit_bytes=VMEM_LIMIT),
        name=name,
    )(x, z, mod, wg, bg, wpa, wpb, wpc, wo, lg, lb)


def _rope_tables(n_tokens):
    rows = n_tokens // GRID_W
    row = jnp.repeat(jnp.arange(rows, dtype=F32), GRID_W)
    col = jnp.tile(jnp.arange(GRID_W, dtype=F32), rows)
    inv_freq = jnp.power(ROPE_BASE, -jnp.arange(ROPE_FREQS, dtype=F32) / ROPE_FREQS)
    ang = jnp.concatenate([inv_freq[:, None] * row[None], inv_freq[:, None] * col[None]], axis=0)
    return jnp.cos(ang), jnp.sin(ang)


def _tile_stats(stats):
    return stats.reshape(-1, STAT_ROWS, LANES)[:, :, 0]


def _transposed_cache(t):
    b, depth, past = t.shape[:3]
    return jnp.transpose(t, (0, 1, 3, 4, 2)).reshape(b, depth, KV_WIDTH, past)


def kernel(x_prompt, x_sample, cache_a_k, cache_a_v, cache_b_k, cache_b_v, c, c_ctx, w_mod, b_mod,
           w_in, q_norm, k_norm, sink, sgu_g, sgu_b, w_spatial, b_spatial, w_proj_a, w_proj_b,
           w_proj_c, w_gate, b_gate, w_out, ln_g, ln_b):
    batch, seq, _ = x_prompt.shape
    dec_batch, dec_seq, _ = x_sample.shape
    tm = 512

    cvecs = jnp.concatenate(
        [c_ctx[None], c, jnp.zeros((MOD_ROWS - 1 - dec_batch, D_MODEL), F32)], axis=0)
    mod = _modulation(cvecs, w_mod, b_mod)

    rope_tables = _rope_tables(dec_seq)
    caches = [_transposed_cache(t) for t in (cache_a_k, cache_a_v, cache_b_k, cache_b_v)]

    w_in_bf = w_in.astype(BF16)
    wt_bf = jnp.swapaxes(lax.optimization_barrier(jnp.concatenate(
        [w_in[:, :, lo:lo + width] for lo, width in _T_SEGMENTS], axis=-1)), 1, 2).astype(BF16)
    wg_bf, wo_bf, ws_bf = w_gate.astype(BF16), w_out.astype(BF16), w_spatial.astype(BF16)
    wpa_bf, wpb_bf, wpc_bf = w_proj_a.astype(BF16), w_proj_b.astype(BF16), w_proj_c.astype(BF16)
    bg3, lg3, lb3 = b_gate[:, None], ln_g[:, None], ln_b[:, None]

    xp = x_prompt.reshape(batch * seq, D_MODEL)
    xs = x_sample.reshape(dec_batch * dec_seq, D_MODEL)
    states = ()
    for l in range(DEPTH):
        bs = jnp.repeat(b_spatial[l].T, LANES, axis=1)
        qn = jnp.broadcast_to(q_norm[l][:, None], (HEAD_DIM, tm))
        kn = jnp.broadcast_to(k_norm[l][:, None], (HEAD_DIM, tm))
        sg, sb = sgu_g[l][None], sgu_b[l][None]
        mod_ctx = mod[l, 0:1][None]
        mod_lat = mod[l, 1:1 + dec_batch][:, None]

        *acts, st_ak, st_av, st_bk, st_bv = _front(
            xp, mod_ctx, w_in_bf, wt_bf, qn, kn, sg, sb, None, states, layer=l, tm=tm, seq=seq)
        states = [st_ak, st_av, st_bk, st_bv]
        z = _mixer_context(*acts[:-1], _tile_stats(acts[-1]), sink[l], ws_bf, bs,
                           layer=l, seq=seq, tokens_per_stat=tm)
        xp = _back(xp, z, mod_ctx, wg_bf, bg3, wpa_bf, wpb_bf, wpc_bf, wo_bf, lg3, lb3,
                   layer=l, tm=2 * tm, rows_per_mod=batch * seq, name="back_context")

        acts = _front(xs, mod_lat, w_in_bf, wt_bf, qn, kn, sg, sb, rope_tables, None,
                      layer=l, tm=tm, seq=dec_seq)
        z = _mixer_latent(*acts[:-1], _tile_stats(acts[-1]), *caches, sink[l], ws_bf, bs,
                          layer=l, seq=dec_seq, tq=256, tokens_per_stat=tm)
        xs = _back(xs, z, mod_lat, wg_bf, bg3, wpa_bf, wpb_bf, wpc_bf, wo_bf, lg3, lb3,
                   layer=l, tm=2 * tm, rows_per_mod=dec_seq, name="back_latent")

    def state(t):
        return jnp.transpose(t.reshape(batch, DEPTH, N_KV, HEAD_DIM, seq), (0, 1, 4, 2, 3))

    return (xp.reshape(batch, seq, D_MODEL), xs.reshape(dec_batch, dec_seq, D_MODEL),
            *[state(t) for t in states])
```

```python
import functools

import jax
import jax.numpy as jnp
from jax import lax
from jax.experimental import pallas as pl
from jax.experimental.pallas import tpu as pltpu

D_MODEL = 1024
DEPTH = 4
GRID_W = 64
HEAD_DIM = 64
HALF = HEAD_DIM // 2
N_HEADS = 8
N_KV = 2
GROUP = N_HEADS // N_KV
KV_WIDTH = N_KV * HEAD_DIM
WIDTH = 512
C_GROUPS = 4
CHUNK = 128
WINDOW = 128
IN_WIDTH = 4096
ROPE_BASE = 10000.0
ROPE_FREQS = HEAD_DIM // 4
EPS = 1e-6
NEG_INF = -1e30
DEEPNORM_ALPHA = (2 * DEPTH) ** 0.25
LOG2E = 1.4426950408889634
Q_SCALE = HEAD_DIM ** -0.5 * LOG2E

LANES = 128
BF16_SUBLANES = 16
MOD_ROWS = 8

_QA, _KA, _VA, _GA = 0, 512, 640, 768
_QB, _KB, _VB, _GB = 1280, 1792, 1920, 2048
_UC, _VC, _GC = 2560, 3072, 3584

(PZ_GA, PZ_GB, PZ_UC, PZ_VN, PZ_GC) = range(5)
PZ_WIDTH = 5 * WIDTH

BF16 = jnp.bfloat16
F32 = jnp.float32

VMEM_LIMIT = 56 * 1024 * 1024
SHIFT_MARGIN = 1.02
MAX_SAFE_SHIFT = 48.0
STAT_ROWS = 8
STAT_Q = 2 * N_KV
CTX_SEQS_PER_STEP = 2
BACK_SUB_TILES = 4
BACK_STAGE_LAG = 4


def _silu(t):
    return t * jax.nn.sigmoid(t)


def _dot(a, b):
    return jnp.dot(a, b, preferred_element_type=F32)


def _layer_block(shape, layer):
    zeros = (0,) * len(shape)
    return pl.BlockSpec((None,) + tuple(shape), lambda *_: (layer,) + zeros,
                        pipeline_mode=pl.Buffered(1))


def _mod_kernel(c_ref, w_ref, b_ref, o_ref):
    s = _silu(c_ref[...]).astype(BF16)
    o_ref[0] = _dot(s, w_ref[0].astype(BF16)) + b_ref[0]


def _modulation(cvecs, w_mod, b_mod):
    tn = 1024
    return pl.pallas_call(
        _mod_kernel,
        grid=(DEPTH, 3 * D_MODEL // tn),
        in_specs=[
            pl.BlockSpec((MOD_ROWS, D_MODEL), lambda l, j: (0, 0)),
            pl.BlockSpec((1, D_MODEL, tn), lambda l, j: (l, 0, j)),
            pl.BlockSpec((1, 1, tn), lambda l, j: (l, 0, j)),
        ],
        out_specs=pl.BlockSpec((1, MOD_ROWS, tn), lambda l, j: (l, 0, j)),
        out_shape=jax.ShapeDtypeStruct((DEPTH, MOD_ROWS, 3 * D_MODEL), F32),
        compiler_params=pltpu.CompilerParams(
            dimension_semantics=("arbitrary", "arbitrary"), vmem_limit_bytes=VMEM_LIMIT),
        name="modulation",
    )(cvecs, w_mod, b_mod.reshape(DEPTH, 1, 3 * D_MODEL))


def _modulated_norm(x, mod):
    mu = jnp.mean(x, axis=-1, keepdims=True)
    xc = x - mu
    var = jnp.mean(xc * xc, axis=-1, keepdims=True)
    xn = xc * lax.rsqrt(var + EPS)
    return xn * (1.0 + mod[:, D_MODEL:2 * D_MODEL]) + mod[:, :D_MODEL]


def _write_token_blocks(ref, t):
    w = ref.shape[2]
    for j in range(ref.shape[0]):
        ref[j] = t[:, j * w:(j + 1) * w].astype(ref.dtype)


def _front_kernel(*refs, rope, state_layer, state_start):
    refs = list(refs)
    x_ref, mod_ref, w_ref, qn_ref, kn_ref, sg_ref, sb_ref = refs[:7]
    refs = refs[7:]
    if rope:
        cos_ref, sin_ref = refs[:2]
        refs = refs[2:]
    with_state = state_layer is not None
    if with_state and not state_start:
        refs = refs[4:]
    qat_ref, qbt_ref, ka_ref, kb_ref, vat_ref, vbt_ref, pz_ref, stat_ref = refs[:8]
    st_refs = refs[8:] if with_state else None

    h = _modulated_norm(x_ref[...], mod_ref[0]).astype(BF16)

    def rotary_t(t):
        if not rope:
            return t
        x1, x2 = t[:HALF], t[HALF:]
        c, s = cos_ref[...], sin_ref[...]
        return jnp.concatenate([x1 * c - x2 * s, x2 * c + x1 * s], axis=0)

    def rms_t(t, gain):
        ms = jnp.mean(t * t, axis=0, keepdims=True)
        return t * lax.rsqrt(ms + EPS) * gain

    def proj_t(lo, width):
        return lax.dot_general(w_ref[:, lo:lo + width], h, (((0,), (1,)), ((), ())),
                               preferred_element_type=F32)

    def proj(lo):
        return _dot(h, w_ref[:, lo:lo + WIDTH])

    def heads_of(t):
        return [t[j * HEAD_DIM:(j + 1) * HEAD_DIM] for j in range(t.shape[0] // HEAD_DIM)]

    def store(block, val):
        pz_ref[:, block * WIDTH:(block + 1) * WIDTH] = val.astype(BF16)

    def max_sq_norm(heads):
        m = None
        for t in heads:
            r = jnp.sum(t * t, axis=0, keepdims=True)
            m = r if m is None else jnp.maximum(m, r)
        return jnp.max(m, axis=1, keepdims=True)

    qn, kn = qn_ref[...], kn_ref[...]
    qa_t = proj_t(_QA, WIDTH)
    ga = proj(_GA)
    qa = [rotary_t(rms_t(t, qn)) * Q_SCALE for t in heads_of(qa_t)]
    qat_ref[...] = jnp.concatenate(qa, axis=0).astype(BF16)
    qb_t = proj_t(_QB, WIDTH)
    store(PZ_GA, _silu(ga))
    gb = proj(_GB)
    qb = [rotary_t(t) * Q_SCALE for t in heads_of(qb_t)]
    qbt_ref[...] = jnp.concatenate(qb, axis=0).astype(BF16)
    kva_t = proj_t(_KA, 2 * KV_WIDTH)
    kvb_t = proj_t(_KB, 2 * KV_WIDTH)
    store(PZ_GB, _silu(gb))
    uc = proj(_UC)

    ka = [rms_t(t, kn) for t in heads_of(kva_t[:KV_WIDTH])]
    kb = heads_of(kvb_t[:KV_WIDTH])
    va, vb = kva_t[KV_WIDTH:], kvb_t[KV_WIDTH:]
    ka_ref[...] = jnp.concatenate([rotary_t(t) for t in ka], axis=0).T.astype(BF16)
    kb_ref[...] = jnp.concatenate([rotary_t(t) for t in kb], axis=0).T.astype(BF16)
    vat_ref[...] = va.astype(BF16)
    if len(vbt_ref.shape) == 3:
        _write_token_blocks(vbt_ref, vb)
    else:
        vbt_ref[...] = vb.astype(BF16)
    stats = [max_sq_norm([t]) for t in ka + kb] + [max_sq_norm(qa), max_sq_norm(qb)]
    for r, v in enumerate(stats):
        stat_ref[r:r + 1, :] = jnp.broadcast_to(v, (1, LANES))
    stat_ref[len(stats):, :] = jnp.zeros((STAT_ROWS - len(stats), LANES), F32)
    if with_state:
        for st_ref, t in zip(st_refs, (jnp.concatenate(ka, axis=0), va, jnp.concatenate(kb, axis=0), vb)):
            if state_start:
                for d in range(st_ref.shape[1]):
                    if d != state_layer:
                        st_ref[:, d] = jnp.zeros(st_ref.shape[:1] + st_ref.shape[2:], F32)
                st_ref = st_ref.at[:, state_layer]
            _write_token_blocks(st_ref, t)

    store(PZ_UC, uc)
    vc = proj(_VC)
    gc = proj(_GC)
    mu = jnp.mean(vc, axis=-1, keepdims=True)
    vcc = vc - mu
    var = jnp.mean(vcc * vcc, axis=-1, keepdims=True)
    store(PZ_VN, vcc * lax.rsqrt(var + EPS) * sg_ref[...] + sb_ref[...])
    store(PZ_GC, _silu(gc))


def _front(x, mod, w_in, qn, kn, sg, sb, rope_tables, states, *, layer, tm, seq):
    with_state = states is not None
    state_start = with_state and not states
    m = x.shape[0]
    n_seq = m // seq
    rope = rope_tables is not None
    const = lambda i: (0, 0)
    tps = max(seq // tm, 1)
    mod_idx = (lambda i: (0, 0, 0)) if tm >= seq else (lambda i: (i // tps, 0, 0))
    in_specs = [
        pl.BlockSpec((tm, D_MODEL), lambda i: (i, 0)),
        pl.BlockSpec((1, 1, 3 * D_MODEL), mod_idx),
        _layer_block((D_MODEL, IN_WIDTH), layer),
        pl.BlockSpec((HEAD_DIM, tm), const),
        pl.BlockSpec((HEAD_DIM, tm), const),
        pl.BlockSpec((1, WIDTH), const),
        pl.BlockSpec((1, WIDTH), const),
    ]
    args = [x, mod, w_in, qn, kn, sg, sb]
    if rope:
        in_specs += [pl.BlockSpec((HALF, tm), lambda i: (0, i % tps))] * 2
        args += list(rope_tables)
    qt_spec = pl.BlockSpec((WIDTH, tm), lambda i: (0, i))
    k_spec = pl.BlockSpec((tm, KV_WIDTH), lambda i: (i, 0))
    vt_shape = (KV_WIDTH, m)
    vt_spec = pl.BlockSpec((KV_WIDTH, tm), lambda i: (0, i))
    if rope:
        vbt_shape = (m // LANES, KV_WIDTH, LANES)
        vbt_spec = pl.BlockSpec((tm // LANES, KV_WIDTH, LANES), lambda i: (i, 0, 0))
    else:
        vbt_shape, vbt_spec = vt_shape, vt_spec
    out_specs = [qt_spec, qt_spec, k_spec, k_spec, vt_spec, vbt_spec,
                 pl.BlockSpec((tm, PZ_WIDTH), lambda i: (i, 0)),
                 pl.BlockSpec((STAT_ROWS, LANES), lambda i: (i, 0))]
    out_shape = [jax.ShapeDtypeStruct((WIDTH, m), BF16)] * 2
    out_shape += [jax.ShapeDtypeStruct((m, KV_WIDTH), BF16)] * 2
    out_shape += [jax.ShapeDtypeStruct(vt_shape, BF16), jax.ShapeDtypeStruct(vbt_shape, BF16)]
    out_shape += [jax.ShapeDtypeStruct((m, PZ_WIDTH), BF16), jax.ShapeDtypeStruct((m // tm * STAT_ROWS, LANES), F32)]
    aliases = {}
    if state_start:
        out_specs += [pl.BlockSpec((tm // seq, DEPTH, KV_WIDTH, seq), lambda i: (i, 0, 0, 0))] * 4
    elif with_state:
        aliases = {len(args) + j: len(out_shape) + j for j in range(4)}
        in_specs += [pl.BlockSpec(memory_space=pl.ANY)] * 4
        args += list(states)
        out_specs += [pl.BlockSpec((tm // seq, None, KV_WIDTH, seq), lambda i: (i, layer, 0, 0))] * 4
    if with_state:
        out_shape += [jax.ShapeDtypeStruct((n_seq, DEPTH, KV_WIDTH, seq), F32)] * 4
    return pl.pallas_call(
        functools.partial(_front_kernel, rope=rope, state_layer=layer if with_state else None,
                          state_start=state_start),
        grid=(m // tm,),
        in_specs=in_specs,
        out_specs=out_specs,
        out_shape=out_shape,
        input_output_aliases=aliases,
        compiler_params=pltpu.CompilerParams(
            dimension_semantics=("arbitrary",), vmem_limit_bytes=VMEM_LIMIT),
        name="front_latent" if rope else "front_context",
    )(*args)


def _scores_t(q_heads_t, sources, kv_head):
    top = jnp.concatenate(q_heads_t, axis=1) if len(q_heads_t) > 1 else q_heads_t[0]
    zeros = jnp.zeros_like(top)
    q_aug = jnp.concatenate([top, zeros] if kv_head == 0 else [zeros, top], axis=0)
    scores = []
    for k, _, mask in sources:
        s = _dot(k(), q_aug)
        if mask is not None:
            s = jnp.where(mask(), s, NEG_INF)
        scores.append(s)
    return scores


def _softmax_pv_t(scores, sources, sinks, kv_head, tq, shift=None):
    m = shift
    if m is None:
        for s in scores:
            ms = jnp.max(s, axis=0, keepdims=True)
            m = ms if m is None else jnp.maximum(m, ms)
    if sinks is not None:
        sink = jnp.concatenate([jnp.full((1, tq), sk * LOG2E, F32) for sk in sinks], axis=1)
        m = jnp.maximum(m, sink)
    rows = slice(kv_head * HEAD_DIM, (kv_head + 1) * HEAD_DIM)
    r = None
    for s, src in zip(scores, sources):
        v_t = src[1](rows)
        v_aug = jnp.concatenate([v_t, jnp.ones((BF16_SUBLANES, v_t.shape[1]), BF16)], axis=0)
        pv = _dot(v_aug, jnp.exp2(s - m).astype(BF16))
        r = pv if r is None else r + pv
    denom = r[HEAD_DIM:HEAD_DIM + 1]
    if sinks is not None:
        denom = denom + jnp.exp2(sink - m)
    return r[:HEAD_DIM] * (1.0 / denom)


def _attention_branches(branches, heads_per_dot, tail):
    tq = branches[0][6].shape[0]
    items = [(br, kv_head, h0) for br in range(len(branches)) for kv_head in range(N_KV)
             for h0 in range(kv_head * GROUP, (kv_head + 1) * GROUP, heads_per_dot)]

    def q_heads_of(item):
        br, _, h0 = item
        qt_ref = branches[br][0]
        return [qt_ref[h * HEAD_DIM:(h + 1) * HEAD_DIM, :] for h in range(h0, h0 + heads_per_dot)]

    def scores_of(item):
        return _scores_t(q_heads_of(item), branches[item[0]][1], item[1])

    shifts = [SHIFT_MARGIN * jnp.sqrt(jnp.full((1, heads_per_dot * tq), branches[br][5][kv_head], F32))
              for br, kv_head, _ in items]
    worst = None
    for branch in branches:
        for sq in branch[5]:
            worst = sq if worst is None else jnp.maximum(worst, sq)
    bounded = worst * SHIFT_MARGIN ** 2 <= MAX_SAFE_SHIFT ** 2

    def run(use_shift):
        pending = scores_of(items[0])
        for i, (br, kv_head, h0) in enumerate(items):
            scores = pending
            if i + 1 < len(items):
                pending = scores_of(items[i + 1])
            _, sources, sink_ref, gate_block, z_block, _, pz_ref, z_ref = branches[br]
            sinks = None if sink_ref is None else [sink_ref[h] for h in range(h0, h0 + heads_per_dot)]
            o_t = _softmax_pv_t(scores, sources, sinks, kv_head, tq, shifts[i] if use_shift else None)
            for j in range(0, heads_per_dot, 2):
                pair = (h0 + j) // 2
                o = o_t[:, j * tq:(j + 2) * tq]
                o = jnp.concatenate([o[:, :tq], o[:, tq:]], axis=0).T
                gate = pz_ref[:, gate_block * WIDTH + pair * LANES: gate_block * WIDTH + (pair + 1) * LANES]
                z_ref[:, z_block * WIDTH + pair * LANES: z_block * WIDTH + (pair + 1) * LANES] = (
                    o * gate.astype(F32)).astype(BF16)
        tail()

    pl.when(bounded)(functools.partial(run, True))
    pl.when(jnp.logical_not(bounded))(functools.partial(run, False))


def _spatial_gate(pz_ref, ws_ref, bs_ref, z_ref, n_rows):
    for ch0 in range(0, n_rows // CHUNK, 2):
        chunks = [slice(ch * CHUNK, (ch + 1) * CHUNK) for ch in (ch0, ch0 + 1)]
        for g in range(C_GROUPS):
            cols = lambda blk: slice(blk * WIDTH + g * LANES, blk * WIDTH + (g + 1) * LANES)
            vn = jnp.concatenate([pz_ref[rows, cols(PZ_VN)] for rows in chunks], axis=1)
            mixed2 = _dot(ws_ref[g], vn)
            for j, rows in enumerate(chunks):
                mixed = mixed2[:, j * LANES:(j + 1) * LANES] + bs_ref[:, g * LANES:(g + 1) * LANES]
                zc = pz_ref[rows, cols(PZ_UC)].astype(F32) * mixed * pz_ref[rows, cols(PZ_GC)].astype(F32)
                z_ref[rows, 2 * WIDTH + g * LANES: 2 * WIDTH + (g + 1) * LANES] = zc.astype(BF16)


def _ref_source(k_ref, vt_ref):
    return (lambda: k_ref[...], lambda rows: vt_ref[rows, :], None)


def _mixer_context_kernel(qat_ref, qbt_ref, ka_ref, kb_ref, vat_ref, vbt_ref, pz_ref, stat_ref,
                          sink_ref, ws_ref, bs_ref, z_ref, *, seq, tokens_per_stat):
    t = lax.div(pl.program_id(0) * pz_ref.shape[0], tokens_per_stat)
    sq = lambda br: [stat_ref[t, br * N_KV + g] * stat_ref[t, STAT_Q + br] for g in range(N_KV)]
    branches = []
    for s in range(pz_ref.shape[0] // seq):
        tok = slice(s * seq, (s + 1) * seq)
        views = (pz_ref.at[tok, :], z_ref.at[tok, :])
        branches += [
            (qat_ref.at[:, tok], [_ref_source(ka_ref.at[tok, :], vat_ref.at[:, tok])], None, PZ_GA, 0, sq(0)) + views,
            (qbt_ref.at[:, tok], [_ref_source(kb_ref.at[tok, :], vbt_ref.at[:, tok])], sink_ref, PZ_GB, 1, sq(1)) + views]
    _attention_branches(
        branches, GROUP, functools.partial(_spatial_gate, pz_ref, ws_ref, bs_ref, z_ref, pz_ref.shape[0]))


def _mixer_context(qat, qbt, ka, kb, vat, vbt, pz, stats, sink, ws, bs, *, layer, seq, tokens_per_stat):
    m = pz.shape[0]
    tok = CTX_SEQS_PER_STEP * seq
    qt_spec = pl.BlockSpec((WIDTH, tok), lambda i: (0, i))
    k_spec = pl.BlockSpec((tok, KV_WIDTH), lambda i: (i, 0))
    vt_spec = pl.BlockSpec((KV_WIDTH, tok), lambda i: (0, i))
    return pl.pallas_call(
        functools.partial(_mixer_context_kernel, seq=seq, tokens_per_stat=tokens_per_stat),
        grid=(m // tok,),
        in_specs=[
            qt_spec, qt_spec, k_spec, k_spec, vt_spec, vt_spec,
            pl.BlockSpec((tok, PZ_WIDTH), lambda i: (i, 0)),
            pl.BlockSpec(memory_space=pltpu.SMEM),
            pl.BlockSpec(memory_space=pltpu.SMEM),
            _layer_block((C_GROUPS, CHUNK, CHUNK), layer),
            pl.BlockSpec((CHUNK, WIDTH), lambda i: (0, 0)),
        ],
        out_specs=pl.BlockSpec((tok, 3 * WIDTH), lambda i: (i, 0)),
        out_shape=jax.ShapeDtypeStruct((m, 3 * WIDTH), BF16),
        compiler_params=pltpu.CompilerParams(
            dimension_semantics=("arbitrary",), vmem_limit_bytes=VMEM_LIMIT),
        name="mixer_context",
    )(qat, qbt, ka, kb, vat, vbt, pz, stats, sink, ws, bs)


def _mixer_latent_kernel(qat_ref, qbt_ref, ka_ref, kb_ref, vat_ref, vbt_ref, pz_ref, stat_ref,
                         cak_ref, cav_ref, cbk_ref, cbv_ref, sink_ref, ws_ref, bs_ref,
                         z_ref, cak_s, cav_s, cbk_s, cbv_s, ck2_s, *, seq, heads_per_dot, tokens_per_stat):
    tq = pz_ref.shape[0]
    band = tq + 2 * WINDOW
    qi = pl.program_id(1)

    @pl.when(qi == 0)
    def _():
        cak_s[...] = cak_ref[0, 0].T.astype(BF16)
        cbk_s[...] = cbk_ref[0, 0].T.astype(BF16)
        cav_s[...] = cav_ref[0, 0].astype(BF16)
        cbv_s[...] = cbv_ref[0, 0].astype(BF16)
        for br, cache_ref in enumerate((cak_ref, cbk_ref)):
            ck = cache_ref[0, 0]
            for g in range(N_KV):
                c2 = jnp.sum(jnp.square(ck[g * HEAD_DIM:(g + 1) * HEAD_DIM]), axis=0, keepdims=True)
                ck2_s[br * N_KV + g] = jnp.max(c2)

    start_blk = jnp.clip(qi * (tq // LANES) - WINDOW // LANES, 0, (seq - band) // LANES)
    start = pl.multiple_of(start_blk * LANES, LANES)

    def in_window():
        k_pos = start + lax.broadcasted_iota(jnp.int32, (band, heads_per_dot * tq), 0)
        q_pos = qi * tq + (lax.broadcasted_iota(jnp.int32, (band, heads_per_dot * tq), 1) & (tq - 1))
        return jnp.abs(q_pos - k_pos) <= WINDOW

    window = (lambda: kb_ref[pl.ds(start, band), :],
              lambda rows: jnp.concatenate(
                  [vbt_ref[start_blk + j, rows, :] for j in range(band // LANES)], axis=1),
              in_window)

    def sq(br):
        b = pl.program_id(0)
        tps = seq // tokens_per_stat
        q2 = stat_ref[b * tps + lax.div(qi * tq, tokens_per_stat), STAT_Q + br]
        out = []
        for g in range(N_KV):
            k2 = ck2_s[br * N_KV + g]
            for j in range(tps):
                k2 = jnp.maximum(k2, stat_ref[b * tps + j, br * N_KV + g])
            out.append(q2 * k2)
        return out

    _attention_branches(
        [(qat_ref, [_ref_source(ka_ref, vat_ref), _ref_source(cak_s, cav_s)], None, PZ_GA, 0,
          sq(0), pz_ref, z_ref),
         (qbt_ref, [window, _ref_source(cbk_s, cbv_s)],
          sink_ref, PZ_GB, 1, sq(1), pz_ref, z_ref)],
        heads_per_dot,
        functools.partial(_spatial_gate, pz_ref, ws_ref, bs_ref, z_ref, tq))


def _mixer_latent(qat, qbt, ka, kb, vat, vbt, pz, stats, cak, cav, cbk, cbv, sink, ws, bs, *, layer, seq, tq,
                  tokens_per_stat):
    m = pz.shape[0]
    n_seq = m // seq
    nq = seq // tq
    past = cak.shape[3]
    qt_spec = pl.BlockSpec((WIDTH, tq), lambda b, q: (0, b * nq + q))
    k_spec = pl.BlockSpec((seq, KV_WIDTH), lambda b, q: (b, 0))
    cache_spec = pl.BlockSpec((1, 1, KV_WIDTH, past), lambda b, q: (b, layer, 0, 0))
    return pl.pallas_call(
        functools.partial(_mixer_latent_kernel, seq=seq, heads_per_dot=2, tokens_per_stat=tokens_per_stat),
        grid=(n_seq, nq),
        in_specs=[
            qt_spec, qt_spec, k_spec, k_spec,
            pl.BlockSpec((KV_WIDTH, seq), lambda b, q: (0, b)),
            pl.BlockSpec((seq // LANES, KV_WIDTH, LANES), lambda b, q: (b, 0, 0)),
            pl.BlockSpec((tq, PZ_WIDTH), lambda b, q: (b * nq + q, 0)),
            pl.BlockSpec(memory_space=pltpu.SMEM),
            cache_spec, cache_spec, cache_spec, cache_spec,
            pl.BlockSpec(memory_space=pltpu.SMEM),
            _layer_block((C_GROUPS, CHUNK, CHUNK), layer),
            pl.BlockSpec((CHUNK, WIDTH), lambda b, q: (0, 0)),
        ],
        out_specs=pl.BlockSpec((tq, 3 * WIDTH), lambda b, q: (b * nq + q, 0)),
        out_shape=jax.ShapeDtypeStruct((m, 3 * WIDTH), BF16),
        scratch_shapes=[pltpu.VMEM((past, KV_WIDTH), BF16), pltpu.VMEM((KV_WIDTH, past), BF16),
                        pltpu.VMEM((past, KV_WIDTH), BF16), pltpu.VMEM((KV_WIDTH, past), BF16),
                        pltpu.SMEM((2 * N_KV,), F32)],
        compiler_params=pltpu.CompilerParams(
            dimension_semantics=("arbitrary", "arbitrary"), vmem_limit_bytes=VMEM_LIMIT),
        name="mixer_latent",
    )(qat, qbt, ka, kb, vat, vbt, pz, stats, cak, cav, cbk, cbv, sink, ws, bs)


def _back_kernel(x_ref, z_ref, mod_ref, wg_ref, bg_ref, wpa_ref, wpb_ref, wpc_ref, wo_ref,
                 lg_ref, lb_ref, o_ref, *, n_sub):
    mod = mod_ref[0]
    sub = x_ref.shape[0] // n_sub
    wp_refs = (wpa_ref, wpb_ref, wpc_ref)

    def stages(s):
        rows = slice(s * sub, (s + 1) * sub)
        st = {}

        def norm():
            st["h"] = _modulated_norm(x_ref[rows, :], mod).astype(BF16)

        def branch(br):
            cols = slice(br * D_MODEL, (br + 1) * D_MODEL)
            g = jax.nn.sigmoid(_dot(st["h"], wg_ref[:, cols]) + bg_ref[:, cols])
            y = _dot(z_ref[rows, br * WIDTH:(br + 1) * WIDTH], wp_refs[br][...])
            st["mix"] = g * y if br == 0 else st["mix"] + g * y

        def out_proj():
            st["y"] = _dot(st["mix"].astype(BF16), wo_ref[...])

        def residual_norm():
            r = DEEPNORM_ALPHA * x_ref[rows, :] + mod[:, 2 * D_MODEL:] * st["y"]
            mu = jnp.mean(r, axis=-1, keepdims=True)
            rc = r - mu
            var = jnp.mean(rc * rc, axis=-1, keepdims=True)
            o_ref[rows, :] = rc * lax.rsqrt(var + EPS) * lg_ref[...] + lb_ref[...]

        return [norm] + [functools.partial(branch, br) for br in range(3)] + [out_proj, residual_norm]

    chains = [stages(s) for s in range(n_sub)]
    n_stage = len(chains[0])
    for t in range(n_stage + (n_sub - 1) * BACK_STAGE_LAG):
        for s in range(n_sub):
            k = t - s * BACK_STAGE_LAG
            if 0 <= k < n_stage:
                chains[s][k]()


def _back(x, z, mod, wg, bg, wpa, wpb, wpc, wo, lg, lb, *, layer, tm, rows_per_mod, name):
    m = x.shape[0]
    tiles_per_mod = rows_per_mod // tm
    return pl.pallas_call(
        functools.partial(_back_kernel, n_sub=BACK_SUB_TILES),
        grid=(m // tm,),
        in_specs=[
            pl.BlockSpec((tm, D_MODEL), lambda i: (i, 0)),
            pl.BlockSpec((tm, 3 * WIDTH), lambda i: (i, 0)),
            pl.BlockSpec((1, 1, 3 * D_MODEL), lambda i: (i // tiles_per_mod, 0, 0)),
            _layer_block((D_MODEL, 3 * D_MODEL), layer),
            _layer_block((1, 3 * D_MODEL), layer),
            _layer_block((WIDTH, D_MODEL), layer),
            _layer_block((WIDTH, D_MODEL), layer),
            _layer_block((WIDTH, D_MODEL), layer),
            _layer_block((D_MODEL, D_MODEL), layer),
            _layer_block((1, D_MODEL), layer),
            _layer_block((1, D_MODEL), layer),
        ],
        out_specs=pl.BlockSpec((tm, D_MODEL), lambda i: (i, 0)),
        out_shape=jax.ShapeDtypeStruct((m, D_MODEL), F32),
        compiler_params=pltpu.CompilerParams(
            dimension_semantics=("arbitrary",), vmem_limit_bytes=VMEM_LIMIT),
        name=name,
    )(x, z, mod, wg, bg, wpa, wpb, wpc, wo, lg, lb)


def _rope_tables(n_tokens):
    rows = n_tokens // GRID_W
    row = jnp.repeat(jnp.arange(rows, dtype=F32), GRID_W)
    col = jnp.tile(jnp.arange(GRID_W, dtype=F32), rows)
    inv_freq = jnp.power(ROPE_BASE, -jnp.arange(ROPE_FREQS, dtype=F32) / ROPE_FREQS)
    ang = jnp.concatenate([inv_freq[:, None] * row[None], inv_freq[:, None] * col[None]], axis=0)
    return jnp.cos(ang), jnp.sin(ang)


def _tile_stats(stats):
    return stats.reshape(-1, STAT_ROWS, LANES)[:, :, 0]


def _transposed_cache(t):
    b, depth, past = t.shape[:3]
    return jnp.transpose(t, (0, 1, 3, 4, 2)).reshape(b, depth, KV_WIDTH, past)


def kernel(x_prompt, x_sample, cache_a_k, cache_a_v, cache_b_k, cache_b_v, c, c_ctx, w_mod, b_mod,
           w_in, q_norm, k_norm, sink, sgu_g, sgu_b, w_spatial, b_spatial, w_proj_a, w_proj_b,
           w_proj_c, w_gate, b_gate, w_out, ln_g, ln_b):
    batch, seq, _ = x_prompt.shape
    dec_batch, dec_seq, _ = x_sample.shape
    tm = 512

    cvecs = jnp.concatenate(
        [c_ctx[None], c, jnp.zeros((MOD_ROWS - 1 - dec_batch, D_MODEL), F32)], axis=0)
    mod = _modulation(cvecs, w_mod, b_mod)

    rope_tables = _rope_tables(dec_seq)
    caches = [_transposed_cache(t) for t in (cache_a_k, cache_a_v, cache_b_k, cache_b_v)]

    w_in_bf = w_in.astype(BF16)
    wg_bf, wo_bf, ws_bf = w_gate.astype(BF16), w_out.astype(BF16), w_spatial.astype(BF16)
    wpa_bf, wpb_bf, wpc_bf = w_proj_a.astype(BF16), w_proj_b.astype(BF16), w_proj_c.astype(BF16)
    bg3, lg3, lb3 = b_gate[:, None], ln_g[:, None], ln_b[:, None]

    xp = x_prompt.reshape(batch * seq, D_MODEL)
    xs = x_sample.reshape(dec_batch * dec_seq, D_MODEL)
    states = ()
    for l in range(DEPTH):
        bs = jnp.repeat(b_spatial[l].T, LANES, axis=1)
        qn = jnp.broadcast_to(q_norm[l][:, None], (HEAD_DIM, tm))
        kn = jnp.broadcast_to(k_norm[l][:, None], (HEAD_DIM, tm))
        sg, sb = sgu_g[l][None], sgu_b[l][None]
        mod_ctx = mod[l, 0:1][None]
        mod_lat = mod[l, 1:1 + dec_batch][:, None]

        *acts, st_ak, st_av, st_bk, st_bv = _front(
            xp, mod_ctx, w_in_bf, qn, kn, sg, sb, None, states, layer=l, tm=tm, seq=seq)
        states = [st_ak, st_av, st_bk, st_bv]
        z = _mixer_context(*acts[:-1], _tile_stats(acts[-1]), sink[l], ws_bf, bs,
                           layer=l, seq=seq, tokens_per_stat=tm)
        xp = _back(xp, z, mod_ctx, wg_bf, bg3, wpa_bf, wpb_bf, wpc_bf, wo_bf, lg3, lb3,
                   layer=l, tm=2 * tm, rows_per_mod=batch * seq, name="back_context")

        acts = _front(xs, mod_lat, w_in_bf, qn, kn, sg, sb, rope_tables, None,
                      layer=l, tm=tm, seq=dec_seq)
        z = _mixer_latent(*acts[:-1], _tile_stats(acts[-1]), *caches, sink[l], ws_bf, bs,
                          layer=l, seq=dec_seq, tq=256, tokens_per_stat=tm)
        xs = _back(xs, z, mod_lat, wg_bf, bg3, wpa_bf, wpb_bf, wpc_bf, wo_bf, lg3, lb3,
                   layer=l, tm=2 * tm, rows_per_mod=dec_seq, name="back_latent")

    def state(t):
        return jnp.transpose(t.reshape(batch, DEPTH, N_KV, HEAD_DIM, seq), (0, 1, 4, 2, 3))

    return (xp.reshape(batch, seq, D_MODEL), xs.reshape(dec_batch, dec_seq, D_MODEL),
            *[state(t) for t in states])
```

```python
import functools

import jax
import jax.numpy as jnp
from jax import lax
from jax.experimental import pallas as pl
from jax.experimental.pallas import tpu as pltpu

D_MODEL = 1024
DEPTH = 4
GRID_W = 64
HEAD_DIM = 64
HALF = HEAD_DIM // 2
N_HEADS = 8
N_KV = 2
GROUP = N_HEADS // N_KV
KV_WIDTH = N_KV * HEAD_DIM
WIDTH = 512
C_GROUPS = 4
CHUNK = 128
WINDOW = 128
IN_WIDTH = 4096
ROPE_BASE = 10000.0
ROPE_FREQS = HEAD_DIM // 4
EPS = 1e-6
NEG_INF = -1e30
DEEPNORM_ALPHA = (2 * DEPTH) ** 0.25
LOG2E = 1.4426950408889634
Q_SCALE = HEAD_DIM ** -0.5 * LOG2E

LANES = 128
BF16_SUBLANES = 16
MOD_ROWS = 8

_QA, _KA, _VA, _GA = 0, 512, 640, 768
_QB, _KB, _VB, _GB = 1280, 1792, 1920, 2048
_UC, _VC, _GC = 2560, 3072, 3584

(PZ_GA, PZ_GB, PZ_UC, PZ_VN, PZ_GC) = range(5)
PZ_WIDTH = 5 * WIDTH

BF16 = jnp.bfloat16
F32 = jnp.float32

VMEM_LIMIT = 56 * 1024 * 1024
SHIFT_MARGIN = 1.02
MAX_SAFE_SHIFT = 48.0
STAT_ROWS = 8
STAT_Q = 2 * N_KV
CTX_SEQS_PER_STEP = 2
BACK_SUB_TILES = 4
BACK_STAGE_LAG = 1


def _silu(t):
    return t * jax.nn.sigmoid(t)


def _dot(a, b):
    return jnp.dot(a, b, preferred_element_type=F32)


def _layer_block(shape, layer):
    zeros = (0,) * len(shape)
    return pl.BlockSpec((None,) + tuple(shape), lambda *_: (layer,) + zeros,
                        pipeline_mode=pl.Buffered(1))


def _mod_kernel(c_ref, w_ref, b_ref, o_ref):
    s = _silu(c_ref[...]).astype(BF16)
    o_ref[0] = _dot(s, w_ref[0].astype(BF16)) + b_ref[0]


def _modulation(cvecs, w_mod, b_mod):
    tn = 1024
    return pl.pallas_call(
        _mod_kernel,
        grid=(DEPTH, 3 * D_MODEL // tn),
        in_specs=[
            pl.BlockSpec((MOD_ROWS, D_MODEL), lambda l, j: (0, 0)),
            pl.BlockSpec((1, D_MODEL, tn), lambda l, j: (l, 0, j)),
            pl.BlockSpec((1, 1, tn), lambda l, j: (l, 0, j)),
        ],
        out_specs=pl.BlockSpec((1, MOD_ROWS, tn), lambda l, j: (l, 0, j)),
        out_shape=jax.ShapeDtypeStruct((DEPTH, MOD_ROWS, 3 * D_MODEL), F32),
        compiler_params=pltpu.CompilerParams(
            dimension_semantics=("arbitrary", "arbitrary"), vmem_limit_bytes=VMEM_LIMIT),
        name="modulation",
    )(cvecs, w_mod, b_mod.reshape(DEPTH, 1, 3 * D_MODEL))


def _modulated_norm(x, mod):
    mu = jnp.mean(x, axis=-1, keepdims=True)
    xc = x - mu
    var = jnp.mean(xc * xc, axis=-1, keepdims=True)
    xn = xc * lax.rsqrt(var + EPS)
    return xn * (1.0 + mod[:, D_MODEL:2 * D_MODEL]) + mod[:, :D_MODEL]


def _write_token_blocks(ref, t):
    w = ref.shape[2]
    for j in range(ref.shape[0]):
        ref[j] = t[:, j * w:(j + 1) * w].astype(ref.dtype)


def _front_kernel(*refs, rope, state_layer, state_start):
    refs = list(refs)
    x_ref, mod_ref, w_ref, qn_ref, kn_ref, sg_ref, sb_ref = refs[:7]
    refs = refs[7:]
    if rope:
        cos_ref, sin_ref = refs[:2]
        refs = refs[2:]
    with_state = state_layer is not None
    if with_state and not state_start:
        refs = refs[4:]
    qat_ref, qbt_ref, ka_ref, kb_ref, vat_ref, vbt_ref, pz_ref, stat_ref = refs[:8]
    st_refs = refs[8:] if with_state else None

    h = _modulated_norm(x_ref[...], mod_ref[...]).astype(BF16)

    def rotary_t(t):
        if not rope:
            return t
        x1, x2 = t[:HALF], t[HALF:]
        c, s = cos_ref[...], sin_ref[...]
        return jnp.concatenate([x1 * c - x2 * s, x2 * c + x1 * s], axis=0)

    def rms_t(t, gain):
        ms = jnp.mean(t * t, axis=0, keepdims=True)
        return t * lax.rsqrt(ms + EPS) * gain

    def proj_t(lo, width):
        return lax.dot_general(w_ref[:, lo:lo + width], h, (((0,), (1,)), ((), ())),
                               preferred_element_type=F32)

    def proj(lo):
        return _dot(h, w_ref[:, lo:lo + WIDTH])

    def heads_of(t):
        return [t[j * HEAD_DIM:(j + 1) * HEAD_DIM] for j in range(t.shape[0] // HEAD_DIM)]

    def store(block, val):
        pz_ref[:, block * WIDTH:(block + 1) * WIDTH] = val.astype(BF16)

    def max_sq_norm(heads):
        m = None
        for t in heads:
            r = jnp.sum(t * t, axis=0, keepdims=True)
            m = r if m is None else jnp.maximum(m, r)
        return jnp.max(m, axis=1, keepdims=True)

    qn, kn = qn_ref[...], kn_ref[...]
    qa_t = proj_t(_QA, WIDTH)
    ga = proj(_GA)
    qa = [rotary_t(rms_t(t, qn)) * Q_SCALE for t in heads_of(qa_t)]
    qat_ref[...] = jnp.concatenate(qa, axis=0).astype(BF16)
    qb_t = proj_t(_QB, WIDTH)
    store(PZ_GA, _silu(ga))
    gb = proj(_GB)
    qb = [rotary_t(t) * Q_SCALE for t in heads_of(qb_t)]
    qbt_ref[...] = jnp.concatenate(qb, axis=0).astype(BF16)
    kva_t = proj_t(_KA, 2 * KV_WIDTH)
    kvb_t = proj_t(_KB, 2 * KV_WIDTH)
    store(PZ_GB, _silu(gb))
    uc = proj(_UC)

    ka = [rms_t(t, kn) for t in heads_of(kva_t[:KV_WIDTH])]
    kb = heads_of(kvb_t[:KV_WIDTH])
    va, vb = kva_t[KV_WIDTH:], kvb_t[KV_WIDTH:]
    ka_ref[...] = jnp.concatenate([rotary_t(t) for t in ka], axis=0).T.astype(BF16)
    kb_ref[...] = jnp.concatenate([rotary_t(t) for t in kb], axis=0).T.astype(BF16)
    vat_ref[...] = va.astype(BF16)
    if len(vbt_ref.shape) == 3:
        _write_token_blocks(vbt_ref, vb)
    else:
        vbt_ref[...] = vb.astype(BF16)
    stats = [max_sq_norm([t]) for t in ka + kb] + [max_sq_norm(qa), max_sq_norm(qb)]
    for r, v in enumerate(stats):
        stat_ref[r:r + 1, :] = jnp.broadcast_to(v, (1, LANES))
    stat_ref[len(stats):, :] = jnp.zeros((STAT_ROWS - len(stats), LANES), F32)
    if with_state:
        for st_ref, t in zip(st_refs, (jnp.concatenate(ka, axis=0), va, jnp.concatenate(kb, axis=0), vb)):
            if state_start:
                for d in range(st_ref.shape[1]):
                    if d != state_layer:
                        st_ref[:, d] = jnp.zeros(st_ref.shape[:1] + st_ref.shape[2:], F32)
                st_ref = st_ref.at[:, state_layer]
            _write_token_blocks(st_ref, t)

    store(PZ_UC, uc)
    vc = proj(_VC)
    gc = proj(_GC)
    mu = jnp.mean(vc, axis=-1, keepdims=True)
    vcc = vc - mu
    var = jnp.mean(vcc * vcc, axis=-1, keepdims=True)
    store(PZ_VN, vcc * lax.rsqrt(var + EPS) * sg_ref[...] + sb_ref[...])
    store(PZ_GC, _silu(gc))


def _front(x, mod, w_in, qn, kn, sg, sb, rope_tables, states, *, layer, tm, seq):
    with_state = states is not None
    state_start = with_state and not states
    m = x.shape[0]
    n_seq = m // seq
    rope = rope_tables is not None
    tps = max(seq // tm, 1)
    mod_idx = (lambda i: (layer, 0, 0, 0)) if tm >= seq else (lambda i: (layer, 1 + i // tps, 0, 0))
    in_specs = [
        pl.BlockSpec((tm, D_MODEL), lambda i: (i, 0)),
        pl.BlockSpec((None, None, 1, 3 * D_MODEL), mod_idx),
        _layer_block((D_MODEL, IN_WIDTH), layer),
        _layer_block((HEAD_DIM, tm), layer),
        _layer_block((HEAD_DIM, tm), layer),
        _layer_block((1, WIDTH), layer),
        _layer_block((1, WIDTH), layer),
    ]
    args = [x, mod, w_in, qn, kn, sg, sb]
    if rope:
        in_specs += [pl.BlockSpec((HALF, tm), lambda i: (0, i % tps))] * 2
        args += list(rope_tables)
    qt_spec = pl.BlockSpec((WIDTH, tm), lambda i: (0, i))
    k_spec = pl.BlockSpec((tm, KV_WIDTH), lambda i: (i, 0))
    vt_shape = (KV_WIDTH, m)
    vt_spec = pl.BlockSpec((KV_WIDTH, tm), lambda i: (0, i))
    if rope:
        vbt_shape = (m // LANES, KV_WIDTH, LANES)
        vbt_spec = pl.BlockSpec((tm // LANES, KV_WIDTH, LANES), lambda i: (i, 0, 0))
    else:
        vbt_shape, vbt_spec = vt_shape, vt_spec
    out_specs = [qt_spec, qt_spec, k_spec, k_spec, vt_spec, vbt_spec,
                 pl.BlockSpec((tm, PZ_WIDTH), lambda i: (i, 0)),
                 pl.BlockSpec((STAT_ROWS, LANES), lambda i: (i, 0))]
    out_shape = [jax.ShapeDtypeStruct((WIDTH, m), BF16)] * 2
    out_shape += [jax.ShapeDtypeStruct((m, KV_WIDTH), BF16)] * 2
    out_shape += [jax.ShapeDtypeStruct(vt_shape, BF16), jax.ShapeDtypeStruct(vbt_shape, BF16)]
    out_shape += [jax.ShapeDtypeStruct((m, PZ_WIDTH), BF16), jax.ShapeDtypeStruct((m // tm * STAT_ROWS, LANES), F32)]
    aliases = {}
    if state_start:
        out_specs += [pl.BlockSpec((tm // seq, DEPTH, KV_WIDTH, seq), lambda i: (i, 0, 0, 0))] * 4
    elif with_state:
        aliases = {len(args) + j: len(out_shape) + j for j in range(4)}
        in_specs += [pl.BlockSpec(memory_space=pl.ANY)] * 4
        args += list(states)
        out_specs += [pl.BlockSpec((tm // seq, None, KV_WIDTH, seq), lambda i: (i, layer, 0, 0))] * 4
    if with_state:
        out_shape += [jax.ShapeDtypeStruct((n_seq, DEPTH, KV_WIDTH, seq), F32)] * 4
    return pl.pallas_call(
        functools.partial(_front_kernel, rope=rope, state_layer=layer if with_state else None,
                          state_start=state_start),
        grid=(m // tm,),
        in_specs=in_specs,
        out_specs=out_specs,
        out_shape=out_shape,
        input_output_aliases=aliases,
        compiler_params=pltpu.CompilerParams(
            dimension_semantics=("arbitrary",), vmem_limit_bytes=VMEM_LIMIT),
        name="front_latent" if rope else "front_context",
    )(*args)


def _scores_t(q_heads_t, sources, kv_head):
    top = jnp.concatenate(q_heads_t, axis=1) if len(q_heads_t) > 1 else q_heads_t[0]
    zeros = jnp.zeros_like(top)
    q_aug = jnp.concatenate([top, zeros] if kv_head == 0 else [zeros, top], axis=0)
    scores = []
    for k, _, mask in sources:
        s = _dot(k(), q_aug)
        if mask is not None:
            s = jnp.where(mask(), s, NEG_INF)
        scores.append(s)
    return scores


def _softmax_pv_t(scores, sources, sinks, kv_head, tq, shift=None):
    m = shift
    if m is None:
        for s in scores:
            ms = jnp.max(s, axis=0, keepdims=True)
            m = ms if m is None else jnp.maximum(m, ms)
    if sinks is not None:
        sink = jnp.concatenate([jnp.full((1, tq), sk * LOG2E, F32) for sk in sinks], axis=1)
        m = jnp.maximum(m, sink)
    rows = slice(kv_head * HEAD_DIM, (kv_head + 1) * HEAD_DIM)
    r = None
    for s, src in zip(scores, sources):
        v_t = src[1](rows)
        v_aug = jnp.concatenate([v_t, jnp.ones((BF16_SUBLANES, v_t.shape[1]), BF16)], axis=0)
        pv = _dot(v_aug, jnp.exp2(s - m).astype(BF16))
        r = pv if r is None else r + pv
    denom = r[HEAD_DIM:HEAD_DIM + 1]
    if sinks is not None:
        denom = denom + jnp.exp2(sink - m)
    return r[:HEAD_DIM] * (1.0 / denom)


def _attention_branches(branches, heads_per_dot, tail):
    tq = branches[0][6].shape[0]
    items = [(br, kv_head, h0) for br in range(len(branches)) for kv_head in range(N_KV)
             for h0 in range(kv_head * GROUP, (kv_head + 1) * GROUP, heads_per_dot)]

    def q_heads_of(item):
        br, _, h0 = item
        qt_ref = branches[br][0]
        return [qt_ref[h * HEAD_DIM:(h + 1) * HEAD_DIM, :] for h in range(h0, h0 + heads_per_dot)]

    def scores_of(item):
        return _scores_t(q_heads_of(item), branches[item[0]][1], item[1])

    shifts = [SHIFT_MARGIN * jnp.sqrt(jnp.full((1, heads_per_dot * tq), branches[br][5][kv_head], F32))
              for br, kv_head, _ in items]
    worst = None
    for branch in branches:
        for sq in branch[5]:
            worst = sq if worst is None else jnp.maximum(worst, sq)
    bounded = worst * SHIFT_MARGIN ** 2 <= MAX_SAFE_SHIFT ** 2

    def run(use_shift):
        pending = scores_of(items[0])
        for i, (br, kv_head, h0) in enumerate(items):
            scores = pending
            if i + 1 < len(items):
                pending = scores_of(items[i + 1])
            _, sources, sink_ref, gate_block, z_block, _, pz_ref, z_ref = branches[br]
            sinks = None if sink_ref is None else [sink_ref(h) for h in range(h0, h0 + heads_per_dot)]
            o_t = _softmax_pv_t(scores, sources, sinks, kv_head, tq, shifts[i] if use_shift else None)
            for j in range(0, heads_per_dot, 2):
                pair = (h0 + j) // 2
                o = o_t[:, j * tq:(j + 2) * tq]
                o = jnp.concatenate([o[:, :tq], o[:, tq:]], axis=0).T
                gate = pz_ref[:, gate_block * WIDTH + pair * LANES: gate_block * WIDTH + (pair + 1) * LANES]
                z_ref[:, z_block * WIDTH + pair * LANES: z_block * WIDTH + (pair + 1) * LANES] = (
                    o * gate.astype(F32)).astype(BF16)
        tail()

    pl.when(bounded)(functools.partial(run, True))
    pl.when(jnp.logical_not(bounded))(functools.partial(run, False))


def _spatial_gate(pz_ref, ws_ref, bs_ref, z_ref, n_rows):
    for ch0 in range(0, n_rows // CHUNK, 2):
        chunks = [slice(ch * CHUNK, (ch + 1) * CHUNK) for ch in (ch0, ch0 + 1)]
        for g in range(C_GROUPS):
            cols = lambda blk: slice(blk * WIDTH + g * LANES, blk * WIDTH + (g + 1) * LANES)
            vn = jnp.concatenate([pz_ref[rows, cols(PZ_VN)] for rows in chunks], axis=1)
            mixed2 = _dot(ws_ref[g], vn)
            for j, rows in enumerate(chunks):
                mixed = mixed2[:, j * LANES:(j + 1) * LANES] + bs_ref[:, g * LANES:(g + 1) * LANES]
                zc = pz_ref[rows, cols(PZ_UC)].astype(F32) * mixed * pz_ref[rows, cols(PZ_GC)].astype(F32)
                z_ref[rows, 2 * WIDTH + g * LANES: 2 * WIDTH + (g + 1) * LANES] = zc.astype(BF16)


def _ref_source(k_ref, vt_ref):
    return (lambda: k_ref[...], lambda rows: vt_ref[rows, :], None)


def _mixer_context_kernel(qat_ref, qbt_ref, ka_ref, kb_ref, vat_ref, vbt_ref, pz_ref, stat_ref,
                          sink_ref, ws_ref, bs_ref, z_ref, *, layer, seq, tokens_per_stat):
    t = lax.div(pl.program_id(0) * pz_ref.shape[0], tokens_per_stat)
    sq = lambda br: [stat_ref[t, br * N_KV + g] * stat_ref[t, STAT_Q + br] for g in range(N_KV)]
    branches = []
    for s in range(pz_ref.shape[0] // seq):
        tok = slice(s * seq, (s + 1) * seq)
        views = (pz_ref.at[tok, :], z_ref.at[tok, :])
        branches += [
            (qat_ref.at[:, tok], [_ref_source(ka_ref.at[tok, :], vat_ref.at[:, tok])], None, PZ_GA, 0, sq(0)) + views,
            (qbt_ref.at[:, tok], [_ref_source(kb_ref.at[tok, :], vbt_ref.at[:, tok])],
             lambda h: sink_ref[layer, h], PZ_GB, 1, sq(1)) + views]
    _attention_branches(
        branches, GROUP, functools.partial(_spatial_gate, pz_ref, ws_ref, bs_ref, z_ref, pz_ref.shape[0]))


def _mixer_context(qat, qbt, ka, kb, vat, vbt, pz, stats, sink, ws, bs, *, layer, seq, tokens_per_stat):
    m = pz.shape[0]
    tok = CTX_SEQS_PER_STEP * seq
    qt_spec = pl.BlockSpec((WIDTH, tok), lambda i: (0, i))
    k_spec = pl.BlockSpec((tok, KV_WIDTH), lambda i: (i, 0))
    vt_spec = pl.BlockSpec((KV_WIDTH, tok), lambda i: (0, i))
    return pl.pallas_call(
        functools.partial(_mixer_context_kernel, layer=layer, seq=seq, tokens_per_stat=tokens_per_stat),
        grid=(m // tok,),
        in_specs=[
            qt_spec, qt_spec, k_spec, k_spec, vt_spec, vt_spec,
            pl.BlockSpec((tok, PZ_WIDTH), lambda i: (i, 0)),
            pl.BlockSpec(memory_space=pltpu.SMEM),
            pl.BlockSpec(memory_space=pltpu.SMEM),
            _layer_block((C_GROUPS, CHUNK, CHUNK), layer),
            _layer_block((CHUNK, WIDTH), layer),
        ],
        out_specs=pl.BlockSpec((tok, 3 * WIDTH), lambda i: (i, 0)),
        out_shape=jax.ShapeDtypeStruct((m, 3 * WIDTH), BF16),
        compiler_params=pltpu.CompilerParams(
            dimension_semantics=("arbitrary",), vmem_limit_bytes=VMEM_LIMIT),
        name="mixer_context",
    )(qat, qbt, ka, kb, vat, vbt, pz, stats, sink, ws, bs)


def _mixer_latent_kernel(qat_ref, qbt_ref, ka_ref, kb_ref, vat_ref, vbt_ref, pz_ref, stat_ref,
                         cak_ref, cav_ref, cbk_ref, cbv_ref, sink_ref, ws_ref, bs_ref,
                         z_ref, cak_s, cav_s, cbk_s, cbv_s, ck2_s, *, layer, seq, heads_per_dot, tokens_per_stat):
    tq = pz_ref.shape[0]
    band = tq + 2 * WINDOW
    qi = pl.program_id(1)

    @pl.when(qi == 0)
    def _():
        cak_s[...] = cak_ref[0, 0].T.astype(BF16)
        cbk_s[...] = cbk_ref[0, 0].T.astype(BF16)
        cav_s[...] = cav_ref[0, 0].astype(BF16)
        cbv_s[...] = cbv_ref[0, 0].astype(BF16)
        for br, cache_ref in enumerate((cak_ref, cbk_ref)):
            ck = cache_ref[0, 0]
            for g in range(N_KV):
                c2 = jnp.sum(jnp.square(ck[g * HEAD_DIM:(g + 1) * HEAD_DIM]), axis=0, keepdims=True)
                ck2_s[br * N_KV + g] = jnp.max(c2)

    start_blk = jnp.clip(qi * (tq // LANES) - WINDOW // LANES, 0, (seq - band) // LANES)
    start = pl.multiple_of(start_blk * LANES, LANES)

    def in_window():
        k_pos = start + lax.broadcasted_iota(jnp.int32, (band, heads_per_dot * tq), 0)
        q_pos = qi * tq + (lax.broadcasted_iota(jnp.int32, (band, heads_per_dot * tq), 1) & (tq - 1))
        return jnp.abs(q_pos - k_pos) <= WINDOW

    window = (lambda: kb_ref[pl.ds(start, band), :],
              lambda rows: jnp.concatenate(
                  [vbt_ref[start_blk + j, rows, :] for j in range(band // LANES)], axis=1),
              in_window)

    def sq(br):
        b = pl.program_id(0)
        tps = seq // tokens_per_stat
        q2 = stat_ref[b * tps + lax.div(qi * tq, tokens_per_stat), STAT_Q + br]
        out = []
        for g in range(N_KV):
            k2 = ck2_s[br * N_KV + g]
            for j in range(tps):
                k2 = jnp.maximum(k2, stat_ref[b * tps + j, br * N_KV + g])
            out.append(q2 * k2)
        return out

    _attention_branches(
        [(qat_ref, [_ref_source(ka_ref, vat_ref), _ref_source(cak_s, cav_s)], None, PZ_GA, 0,
          sq(0), pz_ref, z_ref),
         (qbt_ref, [window, _ref_source(cbk_s, cbv_s)],
          lambda h: sink_ref[layer, h], PZ_GB, 1, sq(1), pz_ref, z_ref)],
        heads_per_dot,
        functools.partial(_spatial_gate, pz_ref, ws_ref, bs_ref, z_ref, tq))


def _mixer_latent(qat, qbt, ka, kb, vat, vbt, pz, stats, cak, cav, cbk, cbv, sink, ws, bs, *, layer, seq, tq,
                  tokens_per_stat):
    m = pz.shape[0]
    n_seq = m // seq
    nq = seq // tq
    past = cak.shape[3]
    qt_spec = pl.BlockSpec((WIDTH, tq), lambda b, q: (0, b * nq + q))
    k_spec = pl.BlockSpec((seq, KV_WIDTH), lambda b, q: (b, 0))
    cache_spec = pl.BlockSpec((1, 1, KV_WIDTH, past), lambda b, q: (b, layer, 0, 0))
    return pl.pallas_call(
        functools.partial(_mixer_latent_kernel, layer=layer, seq=seq, heads_per_dot=2,
                          tokens_per_stat=tokens_per_stat),
        grid=(n_seq, nq),
        in_specs=[
            qt_spec, qt_spec, k_spec, k_spec,
            pl.BlockSpec((KV_WIDTH, seq), lambda b, q: (0, b)),
            pl.BlockSpec((seq // LANES, KV_WIDTH, LANES), lambda b, q: (b, 0, 0)),
            pl.BlockSpec((tq, PZ_WIDTH), lambda b, q: (b * nq + q, 0)),
            pl.BlockSpec(memory_space=pltpu.SMEM),
            cache_spec, cache_spec, cache_spec, cache_spec,
            pl.BlockSpec(memory_space=pltpu.SMEM),
            _layer_block((C_GROUPS, CHUNK, CHUNK), layer),
            _layer_block((CHUNK, WIDTH), layer),
        ],
        out_specs=pl.BlockSpec((tq, 3 * WIDTH), lambda b, q: (b * nq + q, 0)),
        out_shape=jax.ShapeDtypeStruct((m, 3 * WIDTH), BF16),
        scratch_shapes=[pltpu.VMEM((past, KV_WIDTH), BF16), pltpu.VMEM((KV_WIDTH, past), BF16),
                        pltpu.VMEM((past, KV_WIDTH), BF16), pltpu.VMEM((KV_WIDTH, past), BF16),
                        pltpu.SMEM((2 * N_KV,), F32)],
        compiler_params=pltpu.CompilerParams(
            dimension_semantics=("arbitrary", "arbitrary"), vmem_limit_bytes=VMEM_LIMIT),
        name="mixer_latent",
    )(qat, qbt, ka, kb, vat, vbt, pz, stats, cak, cav, cbk, cbv, sink, ws, bs)


def _back_kernel(x_ref, z_ref, mod_ref, wg_ref, bg_ref, wpa_ref, wpb_ref, wpc_ref, wo_ref,
                 lg_ref, lb_ref, o_ref, *, n_sub):
    mod = mod_ref[...]
    sub = x_ref.shape[0] // n_sub
    wp_refs = (wpa_ref, wpb_ref, wpc_ref)

    def stages(s):
        rows = slice(s * sub, (s + 1) * sub)
        st = {}

        def norm():
            st["h"] = _modulated_norm(x_ref[rows, :], mod).astype(BF16)

        def branch(br):
            cols = slice(br * D_MODEL, (br + 1) * D_MODEL)
            g = jax.nn.sigmoid(_dot(st["h"], wg_ref[:, cols]) + bg_ref[:, cols])
            y = _dot(z_ref[rows, br * WIDTH:(br + 1) * WIDTH], wp_refs[br][...])
            st["mix"] = g * y if br == 0 else st["mix"] + g * y

        def out_proj():
            st["y"] = _dot(st["mix"].astype(BF16), wo_ref[...])

        def residual_norm():
            r = DEEPNORM_ALPHA * x_ref[rows, :] + mod[:, 2 * D_MODEL:] * st["y"]
            mu = jnp.mean(r, axis=-1, keepdims=True)
            rc = r - mu
            var = jnp.mean(rc * rc, axis=-1, keepdims=True)
            o_ref[rows, :] = rc * lax.rsqrt(var + EPS) * lg_ref[...] + lb_ref[...]

        return [norm] + [functools.partial(branch, br) for br in range(3)] + [out_proj, residual_norm]

    chains = [stages(s) for s in range(n_sub)]
    n_stage = len(chains[0])
    for t in range(n_stage + (n_sub - 1) * BACK_STAGE_LAG):
        for s in range(n_sub):
            k = t - s * BACK_STAGE_LAG
            if 0 <= k < n_stage:
                chains[s][k]()


def _back(x, z, mod, wg, bg, wpa, wpb, wpc, wo, lg, lb, *, layer, tm, mod_row, rows_per_mod, name):
    m = x.shape[0]
    tiles_per_mod = rows_per_mod // tm
    return pl.pallas_call(
        functools.partial(_back_kernel, n_sub=BACK_SUB_TILES),
        grid=(m // tm,),
        in_specs=[
            pl.BlockSpec((tm, D_MODEL), lambda i: (i, 0)),
            pl.BlockSpec((tm, 3 * WIDTH), lambda i: (i, 0)),
            pl.BlockSpec((None, None, 1, 3 * D_MODEL), lambda i: (layer, mod_row + i // tiles_per_mod, 0, 0)),
            _layer_block((D_MODEL, 3 * D_MODEL), layer),
            _layer_block((1, 3 * D_MODEL), layer),
            _layer_block((WIDTH, D_MODEL), layer),
            _layer_block((WIDTH, D_MODEL), layer),
            _layer_block((WIDTH, D_MODEL), layer),
            _layer_block((D_MODEL, D_MODEL), layer),
            _layer_block((1, D_MODEL), layer),
            _layer_block((1, D_MODEL), layer),
        ],
        out_specs=pl.BlockSpec((tm, D_MODEL), lambda i: (i, 0)),
        out_shape=jax.ShapeDtypeStruct((m, D_MODEL), F32),
        compiler_params=pltpu.CompilerParams(
            dimension_semantics=("arbitrary",), vmem_limit_bytes=VMEM_LIMIT),
        name=name,
    )(x, z, mod, wg, bg, wpa, wpb, wpc, wo, lg, lb)


def _rope_tables(n_tokens):
    rows = n_tokens // GRID_W
    row = jnp.repeat(jnp.arange(rows, dtype=F32), GRID_W)
    col = jnp.tile(jnp.arange(GRID_W, dtype=F32), rows)
    inv_freq = jnp.power(ROPE_BASE, -jnp.arange(ROPE_FREQS, dtype=F32) / ROPE_FREQS)
    ang = jnp.concatenate([inv_freq[:, None] * row[None], inv_freq[:, None] * col[None]], axis=0)
    return jnp.cos(ang), jnp.sin(ang)


def _tile_stats(stats):
    return stats.reshape(-1, STAT_ROWS, LANES)[:, :, 0]


def _transposed_cache(t):
    b, depth, past = t.shape[:3]
    return jnp.transpose(t, (0, 1, 3, 4, 2)).reshape(b, depth, KV_WIDTH, past)


def kernel(x_prompt, x_sample, cache_a_k, cache_a_v, cache_b_k, cache_b_v, c, c_ctx, w_mod, b_mod,
           w_in, q_norm, k_norm, sink, sgu_g, sgu_b, w_spatial, b_spatial, w_proj_a, w_proj_b,
           w_proj_c, w_gate, b_gate, w_out, ln_g, ln_b):
    batch, seq, _ = x_prompt.shape
    dec_batch, dec_seq, _ = x_sample.shape
    tm = 512

    cvecs = jnp.concatenate(
        [c_ctx[None], c, jnp.zeros((MOD_ROWS - 1 - dec_batch, D_MODEL), F32)], axis=0)
    mod = _modulation(cvecs, w_mod, b_mod)

    rope_tables = _rope_tables(dec_seq)
    caches = [_transposed_cache(t) for t in (cache_a_k, cache_a_v, cache_b_k, cache_b_v)]

    w_in_bf = w_in.astype(BF16)
    wg_bf, wo_bf, ws_bf = w_gate.astype(BF16), w_out.astype(BF16), w_spatial.astype(BF16)
    wpa_bf, wpb_bf, wpc_bf = w_proj_a.astype(BF16), w_proj_b.astype(BF16), w_proj_c.astype(BF16)
    bg3, lg3, lb3 = b_gate[:, None], ln_g[:, None], ln_b[:, None]
    sg3, sb3 = sgu_g[:, None], sgu_b[:, None]
    mod4 = mod[:, :, None]
    qn = jnp.broadcast_to(q_norm[:, :, None], (DEPTH, HEAD_DIM, tm))
    kn = jnp.broadcast_to(k_norm[:, :, None], (DEPTH, HEAD_DIM, tm))
    bs = jnp.repeat(jnp.swapaxes(b_spatial, 1, 2), LANES, axis=2)

    xp = x_prompt.reshape(batch * seq, D_MODEL)
    xs = x_sample.reshape(dec_batch * dec_seq, D_MODEL)
    states = ()
    for l in range(DEPTH):
        *acts, st_ak, st_av, st_bk, st_bv = _front(
            xp, mod4, w_in_bf, qn, kn, sg3, sb3, None, states, layer=l, tm=tm, seq=seq)
        states = [st_ak, st_av, st_bk, st_bv]
        z = _mixer_context(*acts[:-1], _tile_stats(acts[-1]), sink, ws_bf, bs,
                           layer=l, seq=seq, tokens_per_stat=tm)
        xp = _back(xp, z, mod4, wg_bf, bg3, wpa_bf, wpb_bf, wpc_bf, wo_bf, lg3, lb3,
                   layer=l, tm=2 * tm, mod_row=0, rows_per_mod=batch * seq, name="back_context")

        acts = _front(xs, mod4, w_in_bf, qn, kn, sg3, sb3, rope_tables, None,
                      layer=l, tm=tm, seq=dec_seq)
        z = _mixer_latent(*acts[:-1], _tile_stats(acts[-1]), *caches, sink, ws_bf, bs,
                          layer=l, seq=dec_seq, tq=256, tokens_per_stat=tm)
        xs = _back(xs, z, mod4, wg_bf, bg3, wpa_bf, wpb_bf, wpc_bf, wo_bf, lg3, lb3,
                   layer=l, tm=2 * tm, mod_row=1, rows_per_mod=dec_seq, name="back_latent")

    def state(t):
        return jnp.transpose(t.reshape(batch, DEPTH, N_KV, HEAD_DIM, seq), (0, 1, 4, 2, 3))

    return (xp.reshape(batch, seq, D_MODEL), xs.reshape(dec_batch, dec_seq, D_MODEL),
            *[state(t) for t in states])
```

```python
import functools

import jax
import jax.numpy as jnp
from jax import lax
from jax.experimental import pallas as pl
from jax.experimental.pallas import tpu as pltpu

D_MODEL = 1024
DEPTH = 4
GRID_W = 64
HEAD_DIM = 64
HALF = HEAD_DIM // 2
N_HEADS = 8
N_KV = 2
GROUP = N_HEADS // N_KV
KV_WIDTH = N_KV * HEAD_DIM
WIDTH = 512
C_GROUPS = 4
CHUNK = 128
WINDOW = 128
IN_WIDTH = 4096
ROPE_BASE = 10000.0
ROPE_FREQS = HEAD_DIM // 4
EPS = 1e-6
NEG_INF = -1e30
DEEPNORM_ALPHA = (2 * DEPTH) ** 0.25
LOG2E = 1.4426950408889634
Q_SCALE = HEAD_DIM ** -0.5 * LOG2E

LANES = 128
BF16_SUBLANES = 16
MOD_ROWS = 8

_QA, _KA, _VA, _GA = 0, 512, 640, 768
_QB, _KB, _VB, _GB = 1280, 1792, 1920, 2048
_UC, _VC, _GC = 2560, 3072, 3584

(PZ_GA, PZ_GB, PZ_UC, PZ_VN, PZ_GC) = range(5)
PZ_WIDTH = 5 * WIDTH

BF16 = jnp.bfloat16
F32 = jnp.float32

VMEM_LIMIT = 56 * 1024 * 1024
SHIFT_MARGIN = 1.02
MAX_SAFE_SHIFT = 48.0
STAT_ROWS = 8
STAT_Q = 2 * N_KV
CTX_SEQS_PER_STEP = 4
BACK_SUB_TILES = 4
BACK_STAGE_LAG = 1


def _silu(t):
    return t * jax.nn.sigmoid(t)


def _dot(a, b):
    return jnp.dot(a, b, preferred_element_type=F32)


def _layer_block(shape, layer):
    zeros = (0,) * len(shape)
    return pl.BlockSpec((None,) + tuple(shape), lambda *_: (layer,) + zeros,
                        pipeline_mode=pl.Buffered(1))


def _mod_kernel(c_ref, w_ref, b_ref, o_ref):
    s = _silu(c_ref[...]).astype(BF16)
    o_ref[0] = _dot(s, w_ref[0].astype(BF16)) + b_ref[0]


def _modulation(cvecs, w_mod, b_mod):
    tn = 3 * D_MODEL
    return pl.pallas_call(
        _mod_kernel,
        grid=(DEPTH, 3 * D_MODEL // tn),
        in_specs=[
            pl.BlockSpec((MOD_ROWS, D_MODEL), lambda l, j: (0, 0)),
            pl.BlockSpec((1, D_MODEL, tn), lambda l, j: (l, 0, j)),
            pl.BlockSpec((1, 1, tn), lambda l, j: (l, 0, j)),
        ],
        out_specs=pl.BlockSpec((1, MOD_ROWS, tn), lambda l, j: (l, 0, j)),
        out_shape=jax.ShapeDtypeStruct((DEPTH, MOD_ROWS, 3 * D_MODEL), F32),
        compiler_params=pltpu.CompilerParams(
            dimension_semantics=("arbitrary", "arbitrary"), vmem_limit_bytes=VMEM_LIMIT),
        name="modulation",
    )(cvecs, w_mod, b_mod.reshape(DEPTH, 1, 3 * D_MODEL))


def _modulated_norm(x, mod):
    mu = jnp.mean(x, axis=-1, keepdims=True)
    xc = x - mu
    var = jnp.mean(xc * xc, axis=-1, keepdims=True)
    xn = xc * lax.rsqrt(var + EPS)
    return xn * (1.0 + mod[:, D_MODEL:2 * D_MODEL]) + mod[:, :D_MODEL]


def _write_token_blocks(ref, t):
    w = ref.shape[2]
    for j in range(ref.shape[0]):
        ref[j] = t[:, j * w:(j + 1) * w].astype(ref.dtype)


def _front_kernel(*refs, rope, state_layer, state_start):
    refs = list(refs)
    x_ref, mod_ref, w_ref, qn_ref, kn_ref, sg_ref, sb_ref = refs[:7]
    refs = refs[7:]
    if rope:
        cos_ref, sin_ref = refs[:2]
        refs = refs[2:]
    with_state = state_layer is not None
    if with_state and not state_start:
        refs = refs[4:]
    qat_ref, qbt_ref, ka_ref, kb_ref, vat_ref, vbt_ref, pz_ref, stat_ref = refs[:8]
    st_refs = refs[8:] if with_state else None

    h = _modulated_norm(x_ref[...], mod_ref[...]).astype(BF16)

    def rotary_t(t):
        if not rope:
            return t
        x1, x2 = t[:HALF], t[HALF:]
        c, s = cos_ref[...], sin_ref[...]
        return jnp.concatenate([x1 * c - x2 * s, x2 * c + x1 * s], axis=0)

    def rms_t(t, gain):
        ms = jnp.mean(t * t, axis=0, keepdims=True)
        return t * lax.rsqrt(ms + EPS) * gain

    def proj_t(lo, width):
        return lax.dot_general(w_ref[:, lo:lo + width], h, (((0,), (1,)), ((), ())),
                               preferred_element_type=F32)

    def proj(lo):
        return _dot(h, w_ref[:, lo:lo + WIDTH])

    def heads_of(t):
        return [t[j * HEAD_DIM:(j + 1) * HEAD_DIM] for j in range(t.shape[0] // HEAD_DIM)]

    def store(block, val):
        pz_ref[:, block * WIDTH:(block + 1) * WIDTH] = val.astype(BF16)

    def max_sq_norm(heads):
        m = None
        for t in heads:
            r = jnp.sum(t * t, axis=0, keepdims=True)
            m = r if m is None else jnp.maximum(m, r)
        return jnp.max(m, axis=1, keepdims=True)

    qn, kn = qn_ref[...], kn_ref[...]
    qa_t = proj_t(_QA, WIDTH)
    ga = proj(_GA)
    qa = [rotary_t(rms_t(t, qn)) * Q_SCALE for t in heads_of(qa_t)]
    qat_ref[...] = jnp.concatenate(qa, axis=0).astype(BF16)
    qb_t = proj_t(_QB, WIDTH)
    store(PZ_GA, _silu(ga))
    gb = proj(_GB)
    qb = [rotary_t(t) * Q_SCALE for t in heads_of(qb_t)]
    qbt_ref[...] = jnp.concatenate(qb, axis=0).astype(BF16)
    kva_t = proj_t(_KA, 2 * KV_WIDTH)
    kvb_t = proj_t(_KB, 2 * KV_WIDTH)
    store(PZ_GB, _silu(gb))
    uc = proj(_UC)

    ka = [rms_t(t, kn) for t in heads_of(kva_t[:KV_WIDTH])]
    kb = heads_of(kvb_t[:KV_WIDTH])
    va, vb = kva_t[KV_WIDTH:], kvb_t[KV_WIDTH:]
    ka_ref[...] = jnp.concatenate([rotary_t(t) for t in ka], axis=0).T.astype(BF16)
    kb_ref[...] = jnp.concatenate([rotary_t(t) for t in kb], axis=0).T.astype(BF16)
    vat_ref[...] = va.astype(BF16)
    if len(vbt_ref.shape) == 3:
        _write_token_blocks(vbt_ref, vb)
    else:
        vbt_ref[...] = vb.astype(BF16)
    stats = [max_sq_norm([t]) for t in ka + kb] + [max_sq_norm(qa), max_sq_norm(qb)]
    for r, v in enumerate(stats):
        stat_ref[r:r + 1, :] = jnp.broadcast_to(v, (1, LANES))
    stat_ref[len(stats):, :] = jnp.zeros((STAT_ROWS - len(stats), LANES), F32)
    if with_state:
        for st_ref, t in zip(st_refs, (jnp.concatenate(ka, axis=0), va, jnp.concatenate(kb, axis=0), vb)):
            if state_start:
                for d in range(st_ref.shape[1]):
                    if d != state_layer:
                        st_ref[:, d] = jnp.zeros(st_ref.shape[:1] + st_ref.shape[2:], F32)
                st_ref = st_ref.at[:, state_layer]
            _write_token_blocks(st_ref, t)

    store(PZ_UC, uc)
    vc = proj(_VC)
    gc = proj(_GC)
    mu = jnp.mean(vc, axis=-1, keepdims=True)
    vcc = vc - mu
    var = jnp.mean(vcc * vcc, axis=-1, keepdims=True)
    store(PZ_VN, vcc * lax.rsqrt(var + EPS) * sg_ref[...] + sb_ref[...])
    store(PZ_GC, _silu(gc))


def _front(x, mod, w_in, qn, kn, sg, sb, rope_tables, states, *, layer, tm, seq):
    with_state = states is not None
    state_start = with_state and not states
    m = x.shape[0]
    n_seq = m // seq
    rope = rope_tables is not None
    tps = max(seq // tm, 1)
    mod_idx = (lambda i: (layer, 0, 0, 0)) if tm >= seq else (lambda i: (layer, 1 + i // tps, 0, 0))
    in_specs = [
        pl.BlockSpec((tm, D_MODEL), lambda i: (i, 0)),
        pl.BlockSpec((None, None, 1, 3 * D_MODEL), mod_idx),
        _layer_block((D_MODEL, IN_WIDTH), layer),
        _layer_block((HEAD_DIM, tm), layer),
        _layer_block((HEAD_DIM, tm), layer),
        _layer_block((1, WIDTH), layer),
        _layer_block((1, WIDTH), layer),
    ]
    args = [x, mod, w_in, qn, kn, sg, sb]
    if rope:
        in_specs += [pl.BlockSpec((HALF, tm), lambda i: (0, i % tps))] * 2
        args += list(rope_tables)
    qt_spec = pl.BlockSpec((WIDTH, tm), lambda i: (0, i))
    k_spec = pl.BlockSpec((tm, KV_WIDTH), lambda i: (i, 0))
    vt_shape = (KV_WIDTH, m)
    vt_spec = pl.BlockSpec((KV_WIDTH, tm), lambda i: (0, i))
    if rope:
        vbt_shape = (m // LANES, KV_WIDTH, LANES)
        vbt_spec = pl.BlockSpec((tm // LANES, KV_WIDTH, LANES), lambda i: (i, 0, 0))
    else:
        vbt_shape, vbt_spec = vt_shape, vt_spec
    out_specs = [qt_spec, qt_spec, k_spec, k_spec, vt_spec, vbt_spec,
                 pl.BlockSpec((tm, PZ_WIDTH), lambda i: (i, 0)),
                 pl.BlockSpec((STAT_ROWS, LANES), lambda i: (i, 0))]
    out_shape = [jax.ShapeDtypeStruct((WIDTH, m), BF16)] * 2
    out_shape += [jax.ShapeDtypeStruct((m, KV_WIDTH), BF16)] * 2
    out_shape += [jax.ShapeDtypeStruct(vt_shape, BF16), jax.ShapeDtypeStruct(vbt_shape, BF16)]
    out_shape += [jax.ShapeDtypeStruct((m, PZ_WIDTH), BF16), jax.ShapeDtypeStruct((m // tm * STAT_ROWS, LANES), F32)]
    aliases = {}
    if state_start:
        out_specs += [pl.BlockSpec((tm // seq, DEPTH, KV_WIDTH, seq), lambda i: (i, 0, 0, 0))] * 4
    elif with_state:
        aliases = {len(args) + j: len(out_shape) + j for j in range(4)}
        in_specs += [pl.BlockSpec(memory_space=pl.ANY)] * 4
        args += list(states)
        out_specs += [pl.BlockSpec((tm // seq, None, KV_WIDTH, seq), lambda i: (i, layer, 0, 0))] * 4
    if with_state:
        out_shape += [jax.ShapeDtypeStruct((n_seq, DEPTH, KV_WIDTH, seq), F32)] * 4
    return pl.pallas_call(
        functools.partial(_front_kernel, rope=rope, state_layer=layer if with_state else None,
                          state_start=state_start),
        grid=(m // tm,),
        in_specs=in_specs,
        out_specs=out_specs,
        out_shape=out_shape,
        input_output_aliases=aliases,
        compiler_params=pltpu.CompilerParams(
            dimension_semantics=("arbitrary",), vmem_limit_bytes=VMEM_LIMIT),
        name="front_latent" if rope else "front_context",
    )(*args)


def _scores_t(q_heads_t, sources, kv_head):
    top = jnp.concatenate(q_heads_t, axis=1) if len(q_heads_t) > 1 else q_heads_t[0]
    zeros = jnp.zeros_like(top)
    q_aug = jnp.concatenate([top, zeros] if kv_head == 0 else [zeros, top], axis=0)
    scores = []
    for k, _, mask in sources:
        s = _dot(k(), q_aug)
        if mask is not None:
            s = jnp.where(mask(), s, NEG_INF)
        scores.append(s)
    return scores


def _softmax_pv_t(scores, sources, sinks, kv_head, tq, shift=None):
    m = shift
    if m is None:
        for s in scores:
            ms = jnp.max(s, axis=0, keepdims=True)
            m = ms if m is None else jnp.maximum(m, ms)
    if sinks is not None:
        sink = jnp.concatenate([jnp.full((1, tq), sk * LOG2E, F32) for sk in sinks], axis=1)
        m = jnp.maximum(m, sink)
    rows = slice(kv_head * HEAD_DIM, (kv_head + 1) * HEAD_DIM)
    r = None
    for s, src in zip(scores, sources):
        v_t = src[1](rows)
        v_aug = jnp.concatenate([v_t, jnp.ones((BF16_SUBLANES, v_t.shape[1]), BF16)], axis=0)
        pv = _dot(v_aug, jnp.exp2(s - m).astype(BF16))
        r = pv if r is None else r + pv
    denom = r[HEAD_DIM:HEAD_DIM + 1]
    if sinks is not None:
        denom = denom + jnp.exp2(sink - m)
    return r[:HEAD_DIM] * (1.0 / denom)


def _attention_branches(branches, heads_per_dot, tail):
    tq = branches[0][6].shape[0]
    items = [(br, kv_head, h0) for br in range(len(branches)) for kv_head in range(N_KV)
             for h0 in range(kv_head * GROUP, (kv_head + 1) * GROUP, heads_per_dot)]

    def q_heads_of(item):
        br, _, h0 = item
        qt_ref = branches[br][0]
        return [qt_ref[h * HEAD_DIM:(h + 1) * HEAD_DIM, :] for h in range(h0, h0 + heads_per_dot)]

    def scores_of(item):
        return _scores_t(q_heads_of(item), branches[item[0]][1], item[1])

    shifts = [SHIFT_MARGIN * jnp.sqrt(jnp.full((1, heads_per_dot * tq), branches[br][5][kv_head], F32))
              for br, kv_head, _ in items]
    worst = None
    for branch in branches:
        for sq in branch[5]:
            worst = sq if worst is None else jnp.maximum(worst, sq)
    bounded = worst * SHIFT_MARGIN ** 2 <= MAX_SAFE_SHIFT ** 2

    def run(use_shift):
        pending = scores_of(items[0])
        for i, (br, kv_head, h0) in enumerate(items):
            scores = pending
            if i + 1 < len(items):
                pending = scores_of(items[i + 1])
            _, sources, sink_ref, gate_block, z_block, _, pz_ref, z_ref = branches[br]
            sinks = None if sink_ref is None else [sink_ref(h) for h in range(h0, h0 + heads_per_dot)]
            o_t = _softmax_pv_t(scores, sources, sinks, kv_head, tq, shifts[i] if use_shift else None)
            for j in range(0, heads_per_dot, 2):
                pair = (h0 + j) // 2
                o = o_t[:, j * tq:(j + 2) * tq]
                o = jnp.concatenate([o[:, :tq], o[:, tq:]], axis=0).T
                gate = pz_ref[:, gate_block * WIDTH + pair * LANES: gate_block * WIDTH + (pair + 1) * LANES]
                z_ref[:, z_block * WIDTH + pair * LANES: z_block * WIDTH + (pair + 1) * LANES] = (
                    o * gate.astype(F32)).astype(BF16)
        tail()

    pl.when(bounded)(functools.partial(run, True))
    pl.when(jnp.logical_not(bounded))(functools.partial(run, False))


def _spatial_gate(pz_ref, ws_ref, bs_ref, z_ref, n_rows):
    for ch0 in range(0, n_rows // CHUNK, 2):
        chunks = [slice(ch * CHUNK, (ch + 1) * CHUNK) for ch in (ch0, ch0 + 1)]
        for g in range(C_GROUPS):
            cols = lambda blk: slice(blk * WIDTH + g * LANES, blk * WIDTH + (g + 1) * LANES)
            vn = jnp.concatenate([pz_ref[rows, cols(PZ_VN)] for rows in chunks], axis=1)
            mixed2 = _dot(ws_ref[g], vn)
            for j, rows in enumerate(chunks):
                mixed = mixed2[:, j * LANES:(j + 1) * LANES] + bs_ref[:, g * LANES:(g + 1) * LANES]
                zc = pz_ref[rows, cols(PZ_UC)].astype(F32) * mixed * pz_ref[rows, cols(PZ_GC)].astype(F32)
                z_ref[rows, 2 * WIDTH + g * LANES: 2 * WIDTH + (g + 1) * LANES] = zc.astype(BF16)


def _ref_source(k_ref, vt_ref):
    return (lambda: k_ref[...], lambda rows: vt_ref[rows, :], None)


def _mixer_context_kernel(qat_ref, qbt_ref, ka_ref, kb_ref, vat_ref, vbt_ref, pz_ref, stat_ref,
                          sink_ref, ws_ref, bs_ref, z_ref, *, layer, seq, tokens_per_stat):
    branches = []
    for s in range(pz_ref.shape[0] // seq):
        tok = slice(s * seq, (s + 1) * seq)
        t = lax.div(pl.program_id(0) * pz_ref.shape[0] + s * seq, tokens_per_stat)
        sq = lambda br, t=t: [stat_ref[t, br * N_KV + g] * stat_ref[t, STAT_Q + br] for g in range(N_KV)]
        views = (pz_ref.at[tok, :], z_ref.at[tok, :])
        branches += [
            (qat_ref.at[:, tok], [_ref_source(ka_ref.at[tok, :], vat_ref.at[:, tok])], None, PZ_GA, 0, sq(0)) + views,
            (qbt_ref.at[:, tok], [_ref_source(kb_ref.at[tok, :], vbt_ref.at[:, tok])],
             lambda h: sink_ref[layer, h], PZ_GB, 1, sq(1)) + views]
    _attention_branches(
        branches, GROUP, functools.partial(_spatial_gate, pz_ref, ws_ref, bs_ref, z_ref, pz_ref.shape[0]))


def _mixer_context(qat, qbt, ka, kb, vat, vbt, pz, stats, sink, ws, bs, *, layer, seq, tokens_per_stat):
    m = pz.shape[0]
    tok = CTX_SEQS_PER_STEP * seq
    qt_spec = pl.BlockSpec((WIDTH, tok), lambda i: (0, i))
    k_spec = pl.BlockSpec((tok, KV_WIDTH), lambda i: (i, 0))
    vt_spec = pl.BlockSpec((KV_WIDTH, tok), lambda i: (0, i))
    return pl.pallas_call(
        functools.partial(_mixer_context_kernel, layer=layer, seq=seq, tokens_per_stat=tokens_per_stat),
        grid=(m // tok,),
        in_specs=[
            qt_spec, qt_spec, k_spec, k_spec, vt_spec, vt_spec,
            pl.BlockSpec((tok, PZ_WIDTH), lambda i: (i, 0)),
            pl.BlockSpec(memory_space=pltpu.SMEM),
            pl.BlockSpec(memory_space=pltpu.SMEM),
            _layer_block((C_GROUPS, CHUNK, CHUNK), layer),
            _layer_block((CHUNK, WIDTH), layer),
        ],
        out_specs=pl.BlockSpec((tok, 3 * WIDTH), lambda i: (i, 0)),
        out_shape=jax.ShapeDtypeStruct((m, 3 * WIDTH), BF16),
        compiler_params=pltpu.CompilerParams(
            dimension_semantics=("arbitrary",), vmem_limit_bytes=VMEM_LIMIT),
        name="mixer_context",
    )(qat, qbt, ka, kb, vat, vbt, pz, stats, sink, ws, bs)


def _mixer_latent_kernel(qat_ref, qbt_ref, ka_ref, kb_ref, vat_ref, vbt_ref, pz_ref, stat_ref,
                         cak_ref, cav_ref, cbk_ref, cbv_ref, sink_ref, ws_ref, bs_ref,
                         z_ref, cak_s, cav_s, cbk_s, cbv_s, ck2_s, *, layer, seq, heads_per_dot, tokens_per_stat):
    tq = pz_ref.shape[0]
    band = tq + 2 * WINDOW
    qi = pl.program_id(1)

    @pl.when(qi == 0)
    def _():
        cak_s[...] = cak_ref[0, 0].T.astype(BF16)
        cbk_s[...] = cbk_ref[0, 0].T.astype(BF16)
        cav_s[...] = cav_ref[0, 0].astype(BF16)
        cbv_s[...] = cbv_ref[0, 0].astype(BF16)
        for br, cache_ref in enumerate((cak_ref, cbk_ref)):
            ck = cache_ref[0, 0]
            for g in range(N_KV):
                c2 = jnp.sum(jnp.square(ck[g * HEAD_DIM:(g + 1) * HEAD_DIM]), axis=0, keepdims=True)
                ck2_s[br * N_KV + g] = jnp.max(c2)

    start_blk = jnp.clip(qi * (tq // LANES) - WINDOW // LANES, 0, (seq - band) // LANES)
    start = pl.multiple_of(start_blk * LANES, LANES)

    def in_window():
        k_pos = start + lax.broadcasted_iota(jnp.int32, (band, heads_per_dot * tq), 0)
        q_pos = qi * tq + (lax.broadcasted_iota(jnp.int32, (band, heads_per_dot * tq), 1) & (tq - 1))
        return jnp.abs(q_pos - k_pos) <= WINDOW

    window = (lambda: kb_ref[pl.ds(start, band), :],
              lambda rows: jnp.concatenate(
                  [vbt_ref[start_blk + j, rows, :] for j in range(band // LANES)], axis=1),
              in_window)

    def sq(br):
        b = pl.program_id(0)
        tps = seq // tokens_per_stat
        q2 = stat_ref[b * tps + lax.div(qi * tq, tokens_per_stat), STAT_Q + br]
        out = []
        for g in range(N_KV):
            k2 = ck2_s[br * N_KV + g]
            for j in range(tps):
                k2 = jnp.maximum(k2, stat_ref[b * tps + j, br * N_KV + g])
            out.append(q2 * k2)
        return out

    _attention_branches(
        [(qat_ref, [_ref_source(ka_ref, vat_ref), _ref_source(cak_s, cav_s)], None, PZ_GA, 0,
          sq(0), pz_ref, z_ref),
         (qbt_ref, [window, _ref_source(cbk_s, cbv_s)],
          lambda h: sink_ref[layer, h], PZ_GB, 1, sq(1), pz_ref, z_ref)],
        heads_per_dot,
        functools.partial(_spatial_gate, pz_ref, ws_ref, bs_ref, z_ref, tq))


def _mixer_latent(qat, qbt, ka, kb, vat, vbt, pz, stats, cak, cav, cbk, cbv, sink, ws, bs, *, layer, seq, tq,
                  tokens_per_stat):
    m = pz.shape[0]
    n_seq = m // seq
    nq = seq // tq
    past = cak.shape[3]
    qt_spec = pl.BlockSpec((WIDTH, tq), lambda b, q: (0, b * nq + q))
    k_spec = pl.BlockSpec((seq, KV_WIDTH), lambda b, q: (b, 0))
    cache_spec = pl.BlockSpec((1, 1, KV_WIDTH, past), lambda b, q: (b, layer, 0, 0))
    return pl.pallas_call(
        functools.partial(_mixer_latent_kernel, layer=layer, seq=seq, heads_per_dot=2,
                          tokens_per_stat=tokens_per_stat),
        grid=(n_seq, nq),
        in_specs=[
            qt_spec, qt_spec, k_spec, k_spec,
            pl.BlockSpec((KV_WIDTH, seq), lambda b, q: (0, b)),
            pl.BlockSpec((seq // LANES, KV_WIDTH, LANES), lambda b, q: (b, 0, 0)),
            pl.BlockSpec((tq, PZ_WIDTH), lambda b, q: (b * nq + q, 0)),
            pl.BlockSpec(memory_space=pltpu.SMEM),
            cache_spec, cache_spec, cache_spec, cache_spec,
            pl.BlockSpec(memory_space=pltpu.SMEM),
            _layer_block((C_GROUPS, CHUNK, CHUNK), layer),
            _layer_block((CHUNK, WIDTH), layer),
        ],
        out_specs=pl.BlockSpec((tq, 3 * WIDTH), lambda b, q: (b * nq + q, 0)),
        out_shape=jax.ShapeDtypeStruct((m, 3 * WIDTH), BF16),
        scratch_shapes=[pltpu.VMEM((past, KV_WIDTH), BF16), pltpu.VMEM((KV_WIDTH, past), BF16),
                        pltpu.VMEM((past, KV_WIDTH), BF16), pltpu.VMEM((KV_WIDTH, past), BF16),
                        pltpu.SMEM((2 * N_KV,), F32)],
        compiler_params=pltpu.CompilerParams(
            dimension_semantics=("arbitrary", "arbitrary"), vmem_limit_bytes=VMEM_LIMIT),
        name="mixer_latent",
    )(qat, qbt, ka, kb, vat, vbt, pz, stats, cak, cav, cbk, cbv, sink, ws, bs)


def _back_kernel(x_ref, z_ref, mod_ref, wg_ref, bg_ref, wpa_ref, wpb_ref, wpc_ref, wo_ref,
                 lg_ref, lb_ref, o_ref, *, n_sub):
    mod = mod_ref[...]
    sub = x_ref.shape[0] // n_sub
    wp_refs = (wpa_ref, wpb_ref, wpc_ref)

    def stages(s):
        rows = slice(s * sub, (s + 1) * sub)
        st = {}

        def norm():
            st["h"] = _modulated_norm(x_ref[rows, :], mod).astype(BF16)

        def branch(br):
            cols = slice(br * D_MODEL, (br + 1) * D_MODEL)
            g = jax.nn.sigmoid(_dot(st["h"], wg_ref[:, cols]) + bg_ref[:, cols])
            y = _dot(z_ref[rows, br * WIDTH:(br + 1) * WIDTH], wp_refs[br][...])
            st["mix"] = g * y if br == 0 else st["mix"] + g * y

        def out_proj():
            st["y"] = _dot(st["mix"].astype(BF16), wo_ref[...])

        def residual_norm():
            r = DEEPNORM_ALPHA * x_ref[rows, :] + mod[:, 2 * D_MODEL:] * st["y"]
            mu = jnp.mean(r, axis=-1, keepdims=True)
            rc = r - mu
            var = jnp.mean(rc * rc, axis=-1, keepdims=True)
            o_ref[rows, :] = rc * lax.rsqrt(var + EPS) * lg_ref[...] + lb_ref[...]

        return [norm] + [functools.partial(branch, br) for br in range(3)] + [out_proj, residual_norm]

    chains = [stages(s) for s in range(n_sub)]
    n_stage = len(chains[0])
    for t in range(n_stage + (n_sub - 1) * BACK_STAGE_LAG):
        for s in range(n_sub):
            k = t - s * BACK_STAGE_LAG
            if 0 <= k < n_stage:
                chains[s][k]()


def _back(x, z, mod, wg, bg, wpa, wpb, wpc, wo, lg, lb, *, layer, tm, mod_row, rows_per_mod, name):
    m = x.shape[0]
    tiles_per_mod = rows_per_mod // tm
    return pl.pallas_call(
        functools.partial(_back_kernel, n_sub=BACK_SUB_TILES),
        grid=(m // tm,),
        in_specs=[
            pl.BlockSpec((tm, D_MODEL), lambda i: (i, 0)),
            pl.BlockSpec((tm, 3 * WIDTH), lambda i: (i, 0)),
            pl.BlockSpec((None, None, 1, 3 * D_MODEL), lambda i: (layer, mod_row + i // tiles_per_mod, 0, 0)),
            _layer_block((D_MODEL, 3 * D_MODEL), layer),
            _layer_block((1, 3 * D_MODEL), layer),
            _layer_block((WIDTH, D_MODEL), layer),
            _layer_block((WIDTH, D_MODEL), layer),
            _layer_block((WIDTH, D_MODEL), layer),
            _layer_block((D_MODEL, D_MODEL), layer),
            _layer_block((1, D_MODEL), layer),
            _layer_block((1, D_MODEL), layer),
        ],
        out_specs=pl.BlockSpec((tm, D_MODEL), lambda i: (i, 0)),
        out_shape=jax.ShapeDtypeStruct((m, D_MODEL), F32),
        compiler_params=pltpu.CompilerParams(
            dimension_semantics=("arbitrary",), vmem_limit_bytes=VMEM_LIMIT),
        name=name,
    )(x, z, mod, wg, bg, wpa, wpb, wpc, wo, lg, lb)


def _rope_tables(n_tokens):
    rows = n_tokens // GRID_W
    row = jnp.repeat(jnp.arange(rows, dtype=F32), GRID_W)
    col = jnp.tile(jnp.arange(GRID_W, dtype=F32), rows)
    inv_freq = jnp.power(ROPE_BASE, -jnp.arange(ROPE_FREQS, dtype=F32) / ROPE_FREQS)
    ang = jnp.concatenate([inv_freq[:, None] * row[None], inv_freq[:, None] * col[None]], axis=0)
    return jnp.cos(ang), jnp.sin(ang)


def _tile_stats(stats):
    return stats.reshape(-1, STAT_ROWS, LANES)[:, :, 0]


def _transposed_cache(t):
    b, depth, past = t.shape[:3]
    return jnp.transpose(t, (0, 1, 3, 4, 2)).reshape(b, depth, KV_WIDTH, past)


def kernel(x_prompt, x_sample, cache_a_k, cache_a_v, cache_b_k, cache_b_v, c, c_ctx, w_mod, b_mod,
           w_in, q_norm, k_norm, sink, sgu_g, sgu_b, w_spatial, b_spatial, w_proj_a, w_proj_b,
           w_proj_c, w_gate, b_gate, w_out, ln_g, ln_b):
    batch, seq, _ = x_prompt.shape
    dec_batch, dec_seq, _ = x_sample.shape
    tm = 512

    cvecs = jnp.concatenate(
        [c_ctx[None], c, jnp.zeros((MOD_ROWS - 1 - dec_batch, D_MODEL), F32)], axis=0)
    mod = _modulation(cvecs, w_mod, b_mod)

    rope_tables = _rope_tables(dec_seq)
    caches = [_transposed_cache(t) for t in (cache_a_k, cache_a_v, cache_b_k, cache_b_v)]

    w_in_bf = w_in.astype(BF16)
    wg_bf, wo_bf, ws_bf = w_gate.astype(BF16), w_out.astype(BF16), w_spatial.astype(BF16)
    wpa_bf, wpb_bf, wpc_bf = w_proj_a.astype(BF16), w_proj_b.astype(BF16), w_proj_c.astype(BF16)
    bg3, lg3, lb3 = b_gate[:, None], ln_g[:, None], ln_b[:, None]
    sg3, sb3 = sgu_g[:, None], sgu_b[:, None]
    mod4 = mod[:, :, None]
    qn = jnp.broadcast_to(q_norm[:, :, None], (DEPTH, HEAD_DIM, tm))
    kn = jnp.broadcast_to(k_norm[:, :, None], (DEPTH, HEAD_DIM, tm))
    bs = jnp.repeat(jnp.swapaxes(b_spatial, 1, 2), LANES, axis=2)

    xp = x_prompt.reshape(batch * seq, D_MODEL)
    xs = x_sample.reshape(dec_batch * dec_seq, D_MODEL)
    states = ()
    for l in range(DEPTH):
        *acts, st_ak, st_av, st_bk, st_bv = _front(
            xp, mod4, w_in_bf, qn, kn, sg3, sb3, None, states, layer=l, tm=tm, seq=seq)
        states = [st_ak, st_av, st_bk, st_bv]
        z = _mixer_context(*acts[:-1], _tile_stats(acts[-1]), sink, ws_bf, bs,
                           layer=l, seq=seq, tokens_per_stat=tm)
        xp = _back(xp, z, mod4, wg_bf, bg3, wpa_bf, wpb_bf, wpc_bf, wo_bf, lg3, lb3,
                   layer=l, tm=2 * tm, mod_row=0, rows_per_mod=batch * seq, name="back_context")

        acts = _front(xs, mod4, w_in_bf, qn, kn, sg3, sb3, rope_tables, None,
                      layer=l, tm=tm, seq=dec_seq)
        z = _mixer_latent(*acts[:-1], _tile_stats(acts[-1]), *caches, sink, ws_bf, bs,
                          layer=l, seq=dec_seq, tq=256, tokens_per_stat=tm)
        xs = _back(xs, z, mod4, wg_bf, bg3, wpa_bf, wpb_bf, wpc_bf, wo_bf, lg3, lb3,
                   layer=l, tm=2 * tm, mod_row=1, rows_per_mod=dec_seq, name="back_latent")

    def state(t):
        return jnp.transpose(t.reshape(batch, DEPTH, N_KV, HEAD_DIM, seq), (0, 1, 4, 2, 3))

    return (xp.reshape(batch, seq, D_MODEL), xs.reshape(dec_batch, dec_seq, D_MODEL),
            *[state(t) for t in states])
```

```python
import functools

import jax
import jax.numpy as jnp
from jax import lax
from jax.experimental import pallas as pl
from jax.experimental.pallas import tpu as pltpu

D_MODEL = 1024
DEPTH = 4
GRID_W = 64
HEAD_DIM = 64
HALF = HEAD_DIM // 2
N_HEADS = 8
N_KV = 2
GROUP = N_HEADS // N_KV
KV_WIDTH = N_KV * HEAD_DIM
WIDTH = 512
C_GROUPS = 4
CHUNK = 128
WINDOW = 128
IN_WIDTH = 4096
ROPE_BASE = 10000.0
ROPE_FREQS = HEAD_DIM // 4
EPS = 1e-6
NEG_INF = -1e30
DEEPNORM_ALPHA = (2 * DEPTH) ** 0.25
LOG2E = 1.4426950408889634
Q_SCALE = HEAD_DIM ** -0.5 * LOG2E

LANES = 128
BF16_SUBLANES = 16
MOD_ROWS = 8

_QA, _KA, _VA, _GA = 0, 512, 640, 768
_QB, _KB, _VB, _GB = 1280, 1792, 1920, 2048
_UC, _VC, _GC = 2560, 3072, 3584

(PZ_GA, PZ_GB, PZ_UC, PZ_VN, PZ_GC) = range(5)
PZ_WIDTH = 5 * WIDTH

BF16 = jnp.bfloat16
F32 = jnp.float32

VMEM_LIMIT = 56 * 1024 * 1024
SHIFT_MARGIN = 1.02
MAX_SAFE_SHIFT = 48.0
STAT_ROWS = 8
STAT_Q = 2 * N_KV
CTX_SEQS_PER_STEP = 4
BACK_SUB_TILES = 4
BACK_STAGE_LAG = 1


def _silu(t):
    return t * jax.nn.sigmoid(t)


def _dot(a, b):
    return jnp.dot(a, b, preferred_element_type=F32)


def _layer_block(shape, layer):
    zeros = (0,) * len(shape)
    return pl.BlockSpec((None,) + tuple(shape), lambda *_: (layer,) + zeros,
                        pipeline_mode=pl.Buffered(1))


def _mod_kernel(c_ref, w_ref, b_ref, o_ref):
    s = _silu(c_ref[...]).astype(BF16)
    o_ref[0] = _dot(s, w_ref[0].astype(BF16)) + b_ref[0]


def _modulation(cvecs, w_mod, b_mod):
    tn = 3 * D_MODEL
    return pl.pallas_call(
        _mod_kernel,
        grid=(DEPTH, 3 * D_MODEL // tn),
        in_specs=[
            pl.BlockSpec((MOD_ROWS, D_MODEL), lambda l, j: (0, 0)),
            pl.BlockSpec((1, D_MODEL, tn), lambda l, j: (l, 0, j)),
            pl.BlockSpec((1, 1, tn), lambda l, j: (l, 0, j)),
        ],
        out_specs=pl.BlockSpec((1, MOD_ROWS, tn), lambda l, j: (l, 0, j)),
        out_shape=jax.ShapeDtypeStruct((DEPTH, MOD_ROWS, 3 * D_MODEL), F32),
        compiler_params=pltpu.CompilerParams(
            dimension_semantics=("arbitrary", "arbitrary"), vmem_limit_bytes=VMEM_LIMIT),
        name="modulation",
    )(cvecs, w_mod, b_mod.reshape(DEPTH, 1, 3 * D_MODEL))


def _modulated_norm(x, mod):
    mu = jnp.mean(x, axis=-1, keepdims=True)
    xc = x - mu
    var = jnp.mean(xc * xc, axis=-1, keepdims=True)
    xn = xc * lax.rsqrt(var + EPS)
    return xn * (1.0 + mod[:, D_MODEL:2 * D_MODEL]) + mod[:, :D_MODEL]


def _write_token_blocks(ref, t):
    w = ref.shape[2]
    for j in range(ref.shape[0]):
        ref[j] = t[:, j * w:(j + 1) * w].astype(ref.dtype)


def _front_kernel(*refs, rope, state_layer, state_start):
    refs = list(refs)
    x_ref, mod_ref, w_ref, qn_ref, kn_ref, sg_ref, sb_ref = refs[:7]
    refs = refs[7:]
    if rope:
        cos_ref, sin_ref = refs[:2]
        refs = refs[2:]
    with_state = state_layer is not None
    if with_state and not state_start:
        refs = refs[4:]
    qat_ref, qbt_ref, ka_ref, kb_ref, vat_ref, vbt_ref, pz_ref, stat_ref = refs[:8]
    st_refs = refs[8:] if with_state else None

    h = _modulated_norm(x_ref[...], mod_ref[...]).astype(BF16)

    def rotary_t(t):
        if not rope:
            return t
        x1, x2 = t[:HALF], t[HALF:]
        c, s = cos_ref[...], sin_ref[...]
        return jnp.concatenate([x1 * c - x2 * s, x2 * c + x1 * s], axis=0)

    def rms_t(t, gain):
        ms = jnp.mean(t * t, axis=0, keepdims=True)
        return t * lax.rsqrt(ms + EPS) * gain

    def proj_t(lo, width):
        return lax.dot_general(w_ref[:, lo:lo + width], h, (((0,), (1,)), ((), ())),
                               preferred_element_type=F32)

    def proj(lo):
        return _dot(h, w_ref[:, lo:lo + WIDTH])

    def heads_of(t):
        return [t[j * HEAD_DIM:(j + 1) * HEAD_DIM] for j in range(t.shape[0] // HEAD_DIM)]

    def store(block, val):
        pz_ref[:, block * WIDTH:(block + 1) * WIDTH] = val.astype(BF16)

    def max_sq_norm(heads):
        m = None
        for t in heads:
            r = jnp.sum(t * t, axis=0, keepdims=True)
            m = r if m is None else jnp.maximum(m, r)
        return jnp.max(m)

    qn, kn = qn_ref[...], kn_ref[...]
    qa_t = proj_t(_QA, WIDTH)
    ga = proj(_GA)
    qa = [rotary_t(rms_t(t, qn)) * Q_SCALE for t in heads_of(qa_t)]
    qat_ref[...] = jnp.concatenate(qa, axis=0).astype(BF16)
    qb_t = proj_t(_QB, WIDTH)
    store(PZ_GA, _silu(ga))
    gb = proj(_GB)
    qb = [rotary_t(t) * Q_SCALE for t in heads_of(qb_t)]
    qbt_ref[...] = jnp.concatenate(qb, axis=0).astype(BF16)
    kva_t = proj_t(_KA, 2 * KV_WIDTH)
    kvb_t = proj_t(_KB, 2 * KV_WIDTH)
    store(PZ_GB, _silu(gb))
    uc = proj(_UC)

    ka = [rms_t(t, kn) for t in heads_of(kva_t[:KV_WIDTH])]
    kb = heads_of(kvb_t[:KV_WIDTH])
    va, vb = kva_t[KV_WIDTH:], kvb_t[KV_WIDTH:]
    ka_ref[...] = jnp.concatenate([rotary_t(t) for t in ka], axis=0).T.astype(BF16)
    kb_ref[...] = jnp.concatenate([rotary_t(t) for t in kb], axis=0).T.astype(BF16)
    vat_ref[...] = va.astype(BF16)
    if len(vbt_ref.shape) == 3:
        _write_token_blocks(vbt_ref, vb)
    else:
        vbt_ref[...] = vb.astype(BF16)
    stats = [max_sq_norm([t]) for t in ka + kb] + [max_sq_norm(qa), max_sq_norm(qb)]
    for r in range(STAT_ROWS):
        stat_ref[pl.program_id(0), r] = stats[r] if r < len(stats) else jnp.float32(0.0)
    if with_state:
        for st_ref, t in zip(st_refs, (jnp.concatenate(ka, axis=0), va, jnp.concatenate(kb, axis=0), vb)):
            if state_start:
                for d in range(st_ref.shape[1]):
                    if d != state_layer:
                        st_ref[:, d] = jnp.zeros(st_ref.shape[:1] + st_ref.shape[2:], F32)
                st_ref = st_ref.at[:, state_layer]
            _write_token_blocks(st_ref, t)

    store(PZ_UC, uc)
    vc = proj(_VC)
    gc = proj(_GC)
    mu = jnp.mean(vc, axis=-1, keepdims=True)
    vcc = vc - mu
    var = jnp.mean(vcc * vcc, axis=-1, keepdims=True)
    store(PZ_VN, vcc * lax.rsqrt(var + EPS) * sg_ref[...] + sb_ref[...])
    store(PZ_GC, _silu(gc))


def _front(x, mod, w_in, qn, kn, sg, sb, rope_tables, states, *, layer, tm, seq):
    with_state = states is not None
    state_start = with_state and not states
    m = x.shape[0]
    n_seq = m // seq
    rope = rope_tables is not None
    tps = max(seq // tm, 1)
    mod_idx = (lambda i: (layer, 0, 0, 0)) if tm >= seq else (lambda i: (layer, 1 + i // tps, 0, 0))
    in_specs = [
        pl.BlockSpec((tm, D_MODEL), lambda i: (i, 0)),
        pl.BlockSpec((None, None, 1, 3 * D_MODEL), mod_idx),
        _layer_block((D_MODEL, IN_WIDTH), layer),
        _layer_block((HEAD_DIM, tm), layer),
        _layer_block((HEAD_DIM, tm), layer),
        _layer_block((1, WIDTH), layer),
        _layer_block((1, WIDTH), layer),
    ]
    args = [x, mod, w_in, qn, kn, sg, sb]
    if rope:
        in_specs += [pl.BlockSpec((HALF, tm), lambda i: (0, i % tps))] * 2
        args += list(rope_tables)
    qt_spec = pl.BlockSpec((WIDTH, tm), lambda i: (0, i))
    k_spec = pl.BlockSpec((tm, KV_WIDTH), lambda i: (i, 0))
    vt_shape = (KV_WIDTH, m)
    vt_spec = pl.BlockSpec((KV_WIDTH, tm), lambda i: (0, i))
    if rope:
        vbt_shape = (m // LANES, KV_WIDTH, LANES)
        vbt_spec = pl.BlockSpec((tm // LANES, KV_WIDTH, LANES), lambda i: (i, 0, 0))
    else:
        vbt_shape, vbt_spec = vt_shape, vt_spec
    out_specs = [qt_spec, qt_spec, k_spec, k_spec, vt_spec, vbt_spec,
                 pl.BlockSpec((tm, PZ_WIDTH), lambda i: (i, 0)),
                 pl.BlockSpec(memory_space=pltpu.SMEM)]
    out_shape = [jax.ShapeDtypeStruct((WIDTH, m), BF16)] * 2
    out_shape += [jax.ShapeDtypeStruct((m, KV_WIDTH), BF16)] * 2
    out_shape += [jax.ShapeDtypeStruct(vt_shape, BF16), jax.ShapeDtypeStruct(vbt_shape, BF16)]
    out_shape += [jax.ShapeDtypeStruct((m, PZ_WIDTH), BF16), jax.ShapeDtypeStruct((m // tm, STAT_ROWS), F32)]
    aliases = {}
    if state_start:
        out_specs += [pl.BlockSpec((tm // seq, DEPTH, KV_WIDTH, seq), lambda i: (i, 0, 0, 0))] * 4
    elif with_state:
        aliases = {len(args) + j: len(out_shape) + j for j in range(4)}
        in_specs += [pl.BlockSpec(memory_space=pl.ANY)] * 4
        args += list(states)
        out_specs += [pl.BlockSpec((tm // seq, None, KV_WIDTH, seq), lambda i: (i, layer, 0, 0))] * 4
    if with_state:
        out_shape += [jax.ShapeDtypeStruct((n_seq, DEPTH, KV_WIDTH, seq), F32)] * 4
    return pl.pallas_call(
        functools.partial(_front_kernel, rope=rope, state_layer=layer if with_state else None,
                          state_start=state_start),
        grid=(m // tm,),
        in_specs=in_specs,
        out_specs=out_specs,
        out_shape=out_shape,
        input_output_aliases=aliases,
        compiler_params=pltpu.CompilerParams(
            dimension_semantics=("arbitrary",), vmem_limit_bytes=VMEM_LIMIT),
        name="front_latent" if rope else "front_context",
    )(*args)


def _scores_t(q_heads_t, sources, kv_head):
    top = jnp.concatenate(q_heads_t, axis=1) if len(q_heads_t) > 1 else q_heads_t[0]
    zeros = jnp.zeros_like(top)
    q_aug = jnp.concatenate([top, zeros] if kv_head == 0 else [zeros, top], axis=0)
    scores = []
    for k, _, mask in sources:
        s = _dot(k(), q_aug)
        if mask is not None:
            s = jnp.where(mask(), s, NEG_INF)
        scores.append(s)
    return scores


def _softmax_pv_t(scores, sources, sinks, kv_head, tq, shift=None):
    m = shift
    if m is None:
        for s in scores:
            ms = jnp.max(s, axis=0, keepdims=True)
            m = ms if m is None else jnp.maximum(m, ms)
    if sinks is not None:
        sink = jnp.concatenate([jnp.full((1, tq), sk * LOG2E, F32) for sk in sinks], axis=1)
        m = jnp.maximum(m, sink)
    rows = slice(kv_head * HEAD_DIM, (kv_head + 1) * HEAD_DIM)
    r = None
    for s, src in zip(scores, sources):
        v_t = src[1](rows)
        v_aug = jnp.concatenate([v_t, jnp.ones((BF16_SUBLANES, v_t.shape[1]), BF16)], axis=0)
        pv = _dot(v_aug, jnp.exp2(s - m).astype(BF16))
        r = pv if r is None else r + pv
    denom = r[HEAD_DIM:HEAD_DIM + 1]
    if sinks is not None:
        denom = denom + jnp.exp2(sink - m)
    return r[:HEAD_DIM] * (1.0 / denom)


def _attention_branches(branches, heads_per_dot, tail):
    tq = branches[0][6].shape[0]
    items = [(br, kv_head, h0) for br in range(len(branches)) for kv_head in range(N_KV)
             for h0 in range(kv_head * GROUP, (kv_head + 1) * GROUP, heads_per_dot)]

    def q_heads_of(item):
        br, _, h0 = item
        qt_ref = branches[br][0]
        return [qt_ref[h * HEAD_DIM:(h + 1) * HEAD_DIM, :] for h in range(h0, h0 + heads_per_dot)]

    def scores_of(item):
        return _scores_t(q_heads_of(item), branches[item[0]][1], item[1])

    shifts = [SHIFT_MARGIN * jnp.sqrt(jnp.full((1, heads_per_dot * tq), branches[br][5][kv_head], F32))
              for br, kv_head, _ in items]
    worst = None
    for branch in branches:
        for sq in branch[5]:
            worst = sq if worst is None else jnp.maximum(worst, sq)
    bounded = worst * SHIFT_MARGIN ** 2 <= MAX_SAFE_SHIFT ** 2

    def run(use_shift):
        pending = scores_of(items[0])
        for i, (br, kv_head, h0) in enumerate(items):
            scores = pending
            if i + 1 < len(items):
                pending = scores_of(items[i + 1])
            _, sources, sink_ref, gate_block, z_block, _, pz_ref, z_ref = branches[br]
            sinks = None if sink_ref is None else [sink_ref(h) for h in range(h0, h0 + heads_per_dot)]
            o_t = _softmax_pv_t(scores, sources, sinks, kv_head, tq, shifts[i] if use_shift else None)
            for j in range(0, heads_per_dot, 2):
                pair = (h0 + j) // 2
                o = o_t[:, j * tq:(j + 2) * tq]
                o = jnp.concatenate([o[:, :tq], o[:, tq:]], axis=0).T
                gate = pz_ref[:, gate_block * WIDTH + pair * LANES: gate_block * WIDTH + (pair + 1) * LANES]
                z_ref[:, z_block * WIDTH + pair * LANES: z_block * WIDTH + (pair + 1) * LANES] = (
                    o * gate.astype(F32)).astype(BF16)
        tail()

    pl.when(bounded)(functools.partial(run, True))
    pl.when(jnp.logical_not(bounded))(functools.partial(run, False))


def _spatial_gate(pz_ref, ws_ref, bs_ref, z_ref, n_rows):
    for ch0 in range(0, n_rows // CHUNK, 2):
        chunks = [slice(ch * CHUNK, (ch + 1) * CHUNK) for ch in (ch0, ch0 + 1)]
        for g in range(C_GROUPS):
            cols = lambda blk: slice(blk * WIDTH + g * LANES, blk * WIDTH + (g + 1) * LANES)
            vn = jnp.concatenate([pz_ref[rows, cols(PZ_VN)] for rows in chunks], axis=1)
            mixed2 = _dot(ws_ref[g], vn)
            for j, rows in enumerate(chunks):
                mixed = mixed2[:, j * LANES:(j + 1) * LANES] + bs_ref[:, g * LANES:(g + 1) * LANES]
                zc = pz_ref[rows, cols(PZ_UC)].astype(F32) * mixed * pz_ref[rows, cols(PZ_GC)].astype(F32)
                z_ref[rows, 2 * WIDTH + g * LANES: 2 * WIDTH + (g + 1) * LANES] = zc.astype(BF16)


def _ref_source(k_ref, vt_ref):
    return (lambda: k_ref[...], lambda rows: vt_ref[rows, :], None)


def _mixer_context_kernel(qat_ref, qbt_ref, ka_ref, kb_ref, vat_ref, vbt_ref, pz_ref, stat_ref,
                          sink_ref, ws_ref, bs_ref, z_ref, *, layer, seq, tokens_per_stat):
    branches = []
    for s in range(pz_ref.shape[0] // seq):
        tok = slice(s * seq, (s + 1) * seq)
        t = lax.div(pl.program_id(0) * pz_ref.shape[0] + s * seq, tokens_per_stat)
        sq = lambda br, t=t: [stat_ref[t, br * N_KV + g] * stat_ref[t, STAT_Q + br] for g in range(N_KV)]
        views = (pz_ref.at[tok, :], z_ref.at[tok, :])
        branches += [
            (qat_ref.at[:, tok], [_ref_source(ka_ref.at[tok, :], vat_ref.at[:, tok])], None, PZ_GA, 0, sq(0)) + views,
            (qbt_ref.at[:, tok], [_ref_source(kb_ref.at[tok, :], vbt_ref.at[:, tok])],
             lambda h: sink_ref[layer, h], PZ_GB, 1, sq(1)) + views]
    _attention_branches(
        branches, GROUP, functools.partial(_spatial_gate, pz_ref, ws_ref, bs_ref, z_ref, pz_ref.shape[0]))


def _mixer_context(qat, qbt, ka, kb, vat, vbt, pz, stats, sink, ws, bs, *, layer, seq, tokens_per_stat):
    m = pz.shape[0]
    tok = CTX_SEQS_PER_STEP * seq
    qt_spec = pl.BlockSpec((WIDTH, tok), lambda i: (0, i))
    k_spec = pl.BlockSpec((tok, KV_WIDTH), lambda i: (i, 0))
    vt_spec = pl.BlockSpec((KV_WIDTH, tok), lambda i: (0, i))
    return pl.pallas_call(
        functools.partial(_mixer_context_kernel, layer=layer, seq=seq, tokens_per_stat=tokens_per_stat),
        grid=(m // tok,),
        in_specs=[
            qt_spec, qt_spec, k_spec, k_spec, vt_spec, vt_spec,
            pl.BlockSpec((tok, PZ_WIDTH), lambda i: (i, 0)),
            pl.BlockSpec(memory_space=pltpu.SMEM),
            pl.BlockSpec(memory_space=pltpu.SMEM),
            _layer_block((C_GROUPS, CHUNK, CHUNK), layer),
            _layer_block((CHUNK, WIDTH), layer),
        ],
        out_specs=pl.BlockSpec((tok, 3 * WIDTH), lambda i: (i, 0)),
        out_shape=jax.ShapeDtypeStruct((m, 3 * WIDTH), BF16),
        compiler_params=pltpu.CompilerParams(
            dimension_semantics=("arbitrary",), vmem_limit_bytes=VMEM_LIMIT),
        name="mixer_context",
    )(qat, qbt, ka, kb, vat, vbt, pz, stats, sink, ws, bs)


def _mixer_latent_kernel(qat_ref, qbt_ref, ka_ref, kb_ref, vat_ref, vbt_ref, pz_ref, stat_ref,
                         cak_ref, cav_ref, cbk_ref, cbv_ref, sink_ref, ws_ref, bs_ref,
                         z_ref, cak_s, cav_s, cbk_s, cbv_s, ck2_s, *, layer, seq, heads_per_dot, tokens_per_stat):
    tq = pz_ref.shape[0]
    band = tq + 2 * WINDOW
    qi = pl.program_id(1)

    @pl.when(qi == 0)
    def _():
        cak_s[...] = cak_ref[0, 0].T.astype(BF16)
        cbk_s[...] = cbk_ref[0, 0].T.astype(BF16)
        cav_s[...] = cav_ref[0, 0].astype(BF16)
        cbv_s[...] = cbv_ref[0, 0].astype(BF16)
        for br, cache_ref in enumerate((cak_ref, cbk_ref)):
            ck = cache_ref[0, 0]
            for g in range(N_KV):
                c2 = jnp.sum(jnp.square(ck[g * HEAD_DIM:(g + 1) * HEAD_DIM]), axis=0, keepdims=True)
                ck2_s[br * N_KV + g] = jnp.max(c2)

    start_blk = jnp.clip(qi * (tq // LANES) - WINDOW // LANES, 0, (seq - band) // LANES)
    start = pl.multiple_of(start_blk * LANES, LANES)

    def in_window():
        k_pos = start + lax.broadcasted_iota(jnp.int32, (band, heads_per_dot * tq), 0)
        q_pos = qi * tq + (lax.broadcasted_iota(jnp.int32, (band, heads_per_dot * tq), 1) & (tq - 1))
        return jnp.abs(q_pos - k_pos) <= WINDOW

    window = (lambda: kb_ref[pl.ds(start, band), :],
              lambda rows: jnp.concatenate(
                  [vbt_ref[start_blk + j, rows, :] for j in range(band // LANES)], axis=1),
              in_window)

    def sq(br):
        b = pl.program_id(0)
        tps = seq // tokens_per_stat
        q2 = stat_ref[b * tps + lax.div(qi * tq, tokens_per_stat), STAT_Q + br]
        out = []
        for g in range(N_KV):
            k2 = ck2_s[br * N_KV + g]
            for j in range(tps):
                k2 = jnp.maximum(k2, stat_ref[b * tps + j, br * N_KV + g])
            out.append(q2 * k2)
        return out

    _attention_branches(
        [(qat_ref, [_ref_source(ka_ref, vat_ref), _ref_source(cak_s, cav_s)], None, PZ_GA, 0,
          sq(0), pz_ref, z_ref),
         (qbt_ref, [window, _ref_source(cbk_s, cbv_s)],
          lambda h: sink_ref[layer, h], PZ_GB, 1, sq(1), pz_ref, z_ref)],
        heads_per_dot,
        functools.partial(_spatial_gate, pz_ref, ws_ref, bs_ref, z_ref, tq))


def _mixer_latent(qat, qbt, ka, kb, vat, vbt, pz, stats, cak, cav, cbk, cbv, sink, ws, bs, *, layer, seq, tq,
                  tokens_per_stat):
    m = pz.shape[0]
    n_seq = m // seq
    nq = seq // tq
    past = cak.shape[3]
    qt_spec = pl.BlockSpec((WIDTH, tq), lambda b, q: (0, b * nq + q))
    k_spec = pl.BlockSpec((seq, KV_WIDTH), lambda b, q: (b, 0))
    cache_spec = pl.BlockSpec((1, 1, KV_WIDTH, past), lambda b, q: (b, layer, 0, 0))
    return pl.pallas_call(
        functools.partial(_mixer_latent_kernel, layer=layer, seq=seq, heads_per_dot=2,
                          tokens_per_stat=tokens_per_stat),
        grid=(n_seq, nq),
        in_specs=[
            qt_spec, qt_spec, k_spec, k_spec,
            pl.BlockSpec((KV_WIDTH, seq), lambda b, q: (0, b)),
            pl.BlockSpec((seq // LANES, KV_WIDTH, LANES), lambda b, q: (b, 0, 0)),
            pl.BlockSpec((tq, PZ_WIDTH), lambda b, q: (b * nq + q, 0)),
            pl.BlockSpec(memory_space=pltpu.SMEM),
            cache_spec, cache_spec, cache_spec, cache_spec,
            pl.BlockSpec(memory_space=pltpu.SMEM),
            _layer_block((C_GROUPS, CHUNK, CHUNK), layer),
            _layer_block((CHUNK, WIDTH), layer),
        ],
        out_specs=pl.BlockSpec((tq, 3 * WIDTH), lambda b, q: (b * nq + q, 0)),
        out_shape=jax.ShapeDtypeStruct((m, 3 * WIDTH), BF16),
        scratch_shapes=[pltpu.VMEM((past, KV_WIDTH), BF16), pltpu.VMEM((KV_WIDTH, past), BF16),
                        pltpu.VMEM((past, KV_WIDTH), BF16), pltpu.VMEM((KV_WIDTH, past), BF16),
                        pltpu.SMEM((2 * N_KV,), F32)],
        compiler_params=pltpu.CompilerParams(
            dimension_semantics=("arbitrary", "arbitrary"), vmem_limit_bytes=VMEM_LIMIT),
        name="mixer_latent",
    )(qat, qbt, ka, kb, vat, vbt, pz, stats, cak, cav, cbk, cbv, sink, ws, bs)


def _back_kernel(x_ref, z_ref, mod_ref, wg_ref, bg_ref, wpa_ref, wpb_ref, wpc_ref, wo_ref,
                 lg_ref, lb_ref, o_ref, *, n_sub):
    mod = mod_ref[...]
    sub = x_ref.shape[0] // n_sub
    wp_refs = (wpa_ref, wpb_ref, wpc_ref)

    def stages(s):
        rows = slice(s * sub, (s + 1) * sub)
        st = {}

        def norm():
            st["h"] = _modulated_norm(x_ref[rows, :], mod).astype(BF16)

        def branch(br):
            cols = slice(br * D_MODEL, (br + 1) * D_MODEL)
            g = jax.nn.sigmoid(_dot(st["h"], wg_ref[:, cols]) + bg_ref[:, cols])
            y = _dot(z_ref[rows, br * WIDTH:(br + 1) * WIDTH], wp_refs[br][...])
            st["mix"] = g * y if br == 0 else st["mix"] + g * y

        def out_proj():
            st["y"] = _dot(st["mix"].astype(BF16), wo_ref[...])

        def residual_norm():
            r = DEEPNORM_ALPHA * x_ref[rows, :] + mod[:, 2 * D_MODEL:] * st["y"]
            mu = jnp.mean(r, axis=-1, keepdims=True)
            rc = r - mu
            var = jnp.mean(rc * rc, axis=-1, keepdims=True)
            o_ref[rows, :] = rc * lax.rsqrt(var + EPS) * lg_ref[...] + lb_ref[...]

        return [norm] + [functools.partial(branch, br) for br in range(3)] + [out_proj, residual_norm]

    chains = [stages(s) for s in range(n_sub)]
    n_stage = len(chains[0])
    for t in range(n_stage + (n_sub - 1) * BACK_STAGE_LAG):
        for s in range(n_sub):
            k = t - s * BACK_STAGE_LAG
            if 0 <= k < n_stage:
                chains[s][k]()


def _back(x, z, mod, wg, bg, wpa, wpb, wpc, wo, lg, lb, *, layer, tm, mod_row, rows_per_mod, name):
    m = x.shape[0]
    tiles_per_mod = rows_per_mod // tm
    return pl.pallas_call(
        functools.partial(_back_kernel, n_sub=BACK_SUB_TILES),
        grid=(m // tm,),
        in_specs=[
            pl.BlockSpec((tm, D_MODEL), lambda i: (i, 0)),
            pl.BlockSpec((tm, 3 * WIDTH), lambda i: (i, 0)),
            pl.BlockSpec((None, None, 1, 3 * D_MODEL), lambda i: (layer, mod_row + i // tiles_per_mod, 0, 0)),
            _layer_block((D_MODEL, 3 * D_MODEL), layer),
            _layer_block((1, 3 * D_MODEL), layer),
            _layer_block((WIDTH, D_MODEL), layer),
            _layer_block((WIDTH, D_MODEL), layer),
            _layer_block((WIDTH, D_MODEL), layer),
            _layer_block((D_MODEL, D_MODEL), layer),
            _layer_block((1, D_MODEL), layer),
            _layer_block((1, D_MODEL), layer),
        ],
        out_specs=pl.BlockSpec((tm, D_MODEL), lambda i: (i, 0)),
        out_shape=jax.ShapeDtypeStruct((m, D_MODEL), F32),
        compiler_params=pltpu.CompilerParams(
            dimension_semantics=("arbitrary",), vmem_limit_bytes=VMEM_LIMIT),
        name=name,
    )(x, z, mod, wg, bg, wpa, wpb, wpc, wo, lg, lb)


def _rope_tables(n_tokens):
    rows = n_tokens // GRID_W
    row = jnp.repeat(jnp.arange(rows, dtype=F32), GRID_W)
    col = jnp.tile(jnp.arange(GRID_W, dtype=F32), rows)
    inv_freq = jnp.power(ROPE_BASE, -jnp.arange(ROPE_FREQS, dtype=F32) / ROPE_FREQS)
    ang = jnp.concatenate([inv_freq[:, None] * row[None], inv_freq[:, None] * col[None]], axis=0)
    return jnp.cos(ang), jnp.sin(ang)


def _transposed_cache(t):
    b, depth, past = t.shape[:3]
    return jnp.transpose(t, (0, 1, 3, 4, 2)).reshape(b, depth, KV_WIDTH, past)


def kernel(x_prompt, x_sample, cache_a_k, cache_a_v, cache_b_k, cache_b_v, c, c_ctx, w_mod, b_mod,
           w_in, q_norm, k_norm, sink, sgu_g, sgu_b, w_spatial, b_spatial, w_proj_a, w_proj_b,
           w_proj_c, w_gate, b_gate, w_out, ln_g, ln_b):
    batch, seq, _ = x_prompt.shape
    dec_batch, dec_seq, _ = x_sample.shape
    tm = 512

    cvecs = jnp.concatenate(
        [c_ctx[None], c, jnp.zeros((MOD_ROWS - 1 - dec_batch, D_MODEL), F32)], axis=0)
    mod = _modulation(cvecs, w_mod, b_mod)

    rope_tables = _rope_tables(dec_seq)
    caches = [_transposed_cache(t) for t in (cache_a_k, cache_a_v, cache_b_k, cache_b_v)]

    w_in_bf = w_in.astype(BF16)
    wg_bf, wo_bf, ws_bf = w_gate.astype(BF16), w_out.astype(BF16), w_spatial.astype(BF16)
    wpa_bf, wpb_bf, wpc_bf = w_proj_a.astype(BF16), w_proj_b.astype(BF16), w_proj_c.astype(BF16)
    bg3, lg3, lb3 = b_gate[:, None], ln_g[:, None], ln_b[:, None]
    sg3, sb3 = sgu_g[:, None], sgu_b[:, None]
    mod4 = mod[:, :, None]
    qn = jnp.broadcast_to(q_norm[:, :, None], (DEPTH, HEAD_DIM, tm))
    kn = jnp.broadcast_to(k_norm[:, :, None], (DEPTH, HEAD_DIM, tm))
    bs = jnp.repeat(jnp.swapaxes(b_spatial, 1, 2), LANES, axis=2)

    xp = x_prompt.reshape(batch * seq, D_MODEL)
    xs = x_sample.reshape(dec_batch * dec_seq, D_MODEL)
    states = ()
    for l in range(DEPTH):
        *acts, st_ak, st_av, st_bk, st_bv = _front(
            xp, mod4, w_in_bf, qn, kn, sg3, sb3, None, states, layer=l, tm=tm, seq=seq)
        states = [st_ak, st_av, st_bk, st_bv]
        z = _mixer_context(*acts, sink, ws_bf, bs,
                           layer=l, seq=seq, tokens_per_stat=tm)
        xp = _back(xp, z, mod4, wg_bf, bg3, wpa_bf, wpb_bf, wpc_bf, wo_bf, lg3, lb3,
                   layer=l, tm=2 * tm, mod_row=0, rows_per_mod=batch * seq, name="back_context")

        acts = _front(xs, mod4, w_in_bf, qn, kn, sg3, sb3, rope_tables, None,
                      layer=l, tm=tm, seq=dec_seq)
        z = _mixer_latent(*acts, *caches, sink, ws_bf, bs,
                          layer=l, seq=dec_seq, tq=256, tokens_per_stat=tm)
        xs = _back(xs, z, mod4, wg_bf, bg3, wpa_bf, wpb_bf, wpc_bf, wo_bf, lg3, lb3,
                   layer=l, tm=2 * tm, mod_row=1, rows_per_mod=dec_seq, name="back_latent")

    def state(t):
        return jnp.transpose(t.reshape(batch, DEPTH, N_KV, HEAD_DIM, seq), (0, 1, 4, 2, 3))

    return (xp.reshape(batch, seq, D_MODEL), xs.reshape(dec_batch, dec_seq, D_MODEL),
            *[state(t) for t in states])
```

```python
import functools

import jax
import jax.numpy as jnp
from jax import lax
from jax.experimental import pallas as pl
from jax.experimental.pallas import tpu as pltpu

D_MODEL = 1024
DEPTH = 4
GRID_W = 64
HEAD_DIM = 64
HALF = HEAD_DIM // 2
N_HEADS = 8
N_KV = 2
GROUP = N_HEADS // N_KV
KV_WIDTH = N_KV * HEAD_DIM
WIDTH = 512
C_GROUPS = 4
CHUNK = 128
WINDOW = 128
IN_WIDTH = 4096
ROPE_BASE = 10000.0
ROPE_FREQS = HEAD_DIM // 4
EPS = 1e-6
NEG_INF = -1e30
DEEPNORM_ALPHA = (2 * DEPTH) ** 0.25
LOG2E = 1.4426950408889634
Q_SCALE = HEAD_DIM ** -0.5 * LOG2E

LANES = 128
BF16_SUBLANES = 16
MOD_ROWS = 8

_QA, _KA, _VA, _GA = 0, 512, 640, 768
_QB, _KB, _VB, _GB = 1280, 1792, 1920, 2048
_UC, _VC, _GC = 2560, 3072, 3584

(PZ_GA, PZ_GB, PZ_UC, PZ_VN, PZ_GC) = range(5)
PZ_WIDTH = 5 * WIDTH

BF16 = jnp.bfloat16
F32 = jnp.float32

VMEM_LIMIT = 56 * 1024 * 1024
SHIFT_MARGIN = 1.02
MAX_SAFE_SHIFT = 48.0
STAT_ROWS = 8
STAT_Q = 2 * N_KV
CTX_SEQS_PER_STEP = 4
LATENT_TILES_PER_STEP = 2
BACK_SUB_TILES = 4
BACK_STAGE_LAG = 1


def _silu(t):
    return t * jax.nn.sigmoid(t)


def _dot(a, b):
    return jnp.dot(a, b, preferred_element_type=F32)


def _layer_block(shape, layer):
    zeros = (0,) * len(shape)
    return pl.BlockSpec((None,) + tuple(shape), lambda *_: (layer,) + zeros,
                        pipeline_mode=pl.Buffered(1))


def _mod_kernel(c_ref, w_ref, b_ref, o_ref):
    s = _silu(c_ref[...]).astype(BF16)
    o_ref[0] = _dot(s, w_ref[0].astype(BF16)) + b_ref[0]


def _modulation(cvecs, w_mod, b_mod):
    tn = 3 * D_MODEL
    return pl.pallas_call(
        _mod_kernel,
        grid=(DEPTH, 3 * D_MODEL // tn),
        in_specs=[
            pl.BlockSpec((MOD_ROWS, D_MODEL), lambda l, j: (0, 0)),
            pl.BlockSpec((1, D_MODEL, tn), lambda l, j: (l, 0, j)),
            pl.BlockSpec((1, 1, tn), lambda l, j: (l, 0, j)),
        ],
        out_specs=pl.BlockSpec((1, MOD_ROWS, tn), lambda l, j: (l, 0, j)),
        out_shape=jax.ShapeDtypeStruct((DEPTH, MOD_ROWS, 3 * D_MODEL), F32),
        compiler_params=pltpu.CompilerParams(
            dimension_semantics=("arbitrary", "arbitrary"), vmem_limit_bytes=VMEM_LIMIT),
        name="modulation",
    )(cvecs, w_mod, b_mod.reshape(DEPTH, 1, 3 * D_MODEL))


def _modulated_norm(x, mod):
    mu = jnp.mean(x, axis=-1, keepdims=True)
    xc = x - mu
    var = jnp.mean(xc * xc, axis=-1, keepdims=True)
    xn = xc * lax.rsqrt(var + EPS)
    return xn * (1.0 + mod[:, D_MODEL:2 * D_MODEL]) + mod[:, :D_MODEL]


def _write_token_blocks(ref, t):
    w = ref.shape[2]
    for j in range(ref.shape[0]):
        ref[j] = t[:, j * w:(j + 1) * w].astype(ref.dtype)


def _front_kernel(*refs, rope, state_layer, state_start):
    refs = list(refs)
    x_ref, mod_ref, w_ref, qn_ref, kn_ref, sg_ref, sb_ref = refs[:7]
    refs = refs[7:]
    if rope:
        cos_ref, sin_ref = refs[:2]
        refs = refs[2:]
    with_state = state_layer is not None
    if with_state and not state_start:
        refs = refs[4:]
    qat_ref, qbt_ref, ka_ref, kb_ref, vat_ref, vbt_ref, pz_ref, stat_ref = refs[:8]
    st_refs = refs[8:] if with_state else None

    h = _modulated_norm(x_ref[...], mod_ref[...]).astype(BF16)

    def rotary_t(t):
        if not rope:
            return t
        x1, x2 = t[:HALF], t[HALF:]
        c, s = cos_ref[...], sin_ref[...]
        return jnp.concatenate([x1 * c - x2 * s, x2 * c + x1 * s], axis=0)

    def rms_t(t, gain):
        ms = jnp.mean(t * t, axis=0, keepdims=True)
        return t * lax.rsqrt(ms + EPS) * gain

    def proj_t(lo, width):
        return lax.dot_general(w_ref[:, lo:lo + width], h, (((0,), (1,)), ((), ())),
                               preferred_element_type=F32)

    def proj(lo):
        return _dot(h, w_ref[:, lo:lo + WIDTH])

    def heads_of(t):
        return [t[j * HEAD_DIM:(j + 1) * HEAD_DIM] for j in range(t.shape[0] // HEAD_DIM)]

    def store(block, val):
        pz_ref[:, block * WIDTH:(block + 1) * WIDTH] = val.astype(BF16)

    def max_sq_norm(heads):
        m = None
        for t in heads:
            r = jnp.sum(t * t, axis=0, keepdims=True)
            m = r if m is None else jnp.maximum(m, r)
        return jnp.max(m)

    qn, kn = qn_ref[...], kn_ref[...]
    qa_t = proj_t(_QA, WIDTH)
    ga = proj(_GA)
    qa = [rotary_t(rms_t(t, qn)) * Q_SCALE for t in heads_of(qa_t)]
    qat_ref[...] = jnp.concatenate(qa, axis=0).astype(BF16)
    qb_t = proj_t(_QB, WIDTH)
    store(PZ_GA, _silu(ga))
    gb = proj(_GB)
    qb = [rotary_t(t) * Q_SCALE for t in heads_of(qb_t)]
    qbt_ref[...] = jnp.concatenate(qb, axis=0).astype(BF16)
    kva_t = proj_t(_KA, 2 * KV_WIDTH)
    kvb_t = proj_t(_KB, 2 * KV_WIDTH)
    store(PZ_GB, _silu(gb))
    uc = proj(_UC)

    ka = [rms_t(t, kn) for t in heads_of(kva_t[:KV_WIDTH])]
    kb = heads_of(kvb_t[:KV_WIDTH])
    va, vb = kva_t[KV_WIDTH:], kvb_t[KV_WIDTH:]
    ka_ref[...] = jnp.concatenate([rotary_t(t) for t in ka], axis=0).T.astype(BF16)
    kb_ref[...] = jnp.concatenate([rotary_t(t) for t in kb], axis=0).T.astype(BF16)
    vat_ref[...] = va.astype(BF16)
    if len(vbt_ref.shape) == 3:
        _write_token_blocks(vbt_ref, vb)
    else:
        vbt_ref[...] = vb.astype(BF16)
    stats = [max_sq_norm([t]) for t in ka + kb] + [max_sq_norm(qa), max_sq_norm(qb)]
    for r in range(STAT_ROWS):
        stat_ref[pl.program_id(0), r] = stats[r] if r < len(stats) else jnp.float32(0.0)
    if with_state:
        for st_ref, t in zip(st_refs, (jnp.concatenate(ka, axis=0), va, jnp.concatenate(kb, axis=0), vb)):
            if state_start:
                for d in range(st_ref.shape[1]):
                    if d != state_layer:
                        st_ref[:, d] = jnp.zeros(st_ref.shape[:1] + st_ref.shape[2:], F32)
                st_ref = st_ref.at[:, state_layer]
            _write_token_blocks(st_ref, t)

    store(PZ_UC, uc)
    vc = proj(_VC)
    gc = proj(_GC)
    mu = jnp.mean(vc, axis=-1, keepdims=True)
    vcc = vc - mu
    var = jnp.mean(vcc * vcc, axis=-1, keepdims=True)
    store(PZ_VN, vcc * lax.rsqrt(var + EPS) * sg_ref[...] + sb_ref[...])
    store(PZ_GC, _silu(gc))


def _front(x, mod, w_in, qn, kn, sg, sb, rope_tables, states, *, layer, tm, seq):
    with_state = states is not None
    state_start = with_state and not states
    m = x.shape[0]
    n_seq = m // seq
    rope = rope_tables is not None
    tps = max(seq // tm, 1)
    mod_idx = (lambda i: (layer, 0, 0, 0)) if tm >= seq else (lambda i: (layer, 1 + i // tps, 0, 0))
    in_specs = [
        pl.BlockSpec((tm, D_MODEL), lambda i: (i, 0)),
        pl.BlockSpec((None, None, 1, 3 * D_MODEL), mod_idx),
        _layer_block((D_MODEL, IN_WIDTH), layer),
        _layer_block((HEAD_DIM, tm), layer),
        _layer_block((HEAD_DIM, tm), layer),
        _layer_block((1, WIDTH), layer),
        _layer_block((1, WIDTH), layer),
    ]
    args = [x, mod, w_in, qn, kn, sg, sb]
    if rope:
        in_specs += [pl.BlockSpec((HALF, tm), lambda i: (0, i % tps))] * 2
        args += list(rope_tables)
    qt_spec = pl.BlockSpec((WIDTH, tm), lambda i: (0, i))
    k_spec = pl.BlockSpec((tm, KV_WIDTH), lambda i: (i, 0))
    vt_shape = (KV_WIDTH, m)
    vt_spec = pl.BlockSpec((KV_WIDTH, tm), lambda i: (0, i))
    if rope:
        vbt_shape = (m // LANES, KV_WIDTH, LANES)
        vbt_spec = pl.BlockSpec((tm // LANES, KV_WIDTH, LANES), lambda i: (i, 0, 0))
    else:
        vbt_shape, vbt_spec = vt_shape, vt_spec
    out_specs = [qt_spec, qt_spec, k_spec, k_spec, vt_spec, vbt_spec,
                 pl.BlockSpec((tm, PZ_WIDTH), lambda i: (i, 0)),
                 pl.BlockSpec(memory_space=pltpu.SMEM)]
    out_shape = [jax.ShapeDtypeStruct((WIDTH, m), BF16)] * 2
    out_shape += [jax.ShapeDtypeStruct((m, KV_WIDTH), BF16)] * 2
    out_shape += [jax.ShapeDtypeStruct(vt_shape, BF16), jax.ShapeDtypeStruct(vbt_shape, BF16)]
    out_shape += [jax.ShapeDtypeStruct((m, PZ_WIDTH), BF16), jax.ShapeDtypeStruct((m // tm, STAT_ROWS), F32)]
    aliases = {}
    if state_start:
        out_specs += [pl.BlockSpec((tm // seq, DEPTH, KV_WIDTH, seq), lambda i: (i, 0, 0, 0))] * 4
    elif with_state:
        aliases = {len(args) + j: len(out_shape) + j for j in range(4)}
        in_specs += [pl.BlockSpec(memory_space=pl.ANY)] * 4
        args += list(states)
        out_specs += [pl.BlockSpec((tm // seq, None, KV_WIDTH, seq), lambda i: (i, layer, 0, 0))] * 4
    if with_state:
        out_shape += [jax.ShapeDtypeStruct((n_seq, DEPTH, KV_WIDTH, seq), F32)] * 4
    return pl.pallas_call(
        functools.partial(_front_kernel, rope=rope, state_layer=layer if with_state else None,
                          state_start=state_start),
        grid=(m // tm,),
        in_specs=in_specs,
        out_specs=out_specs,
        out_shape=out_shape,
        input_output_aliases=aliases,
        compiler_params=pltpu.CompilerParams(
            dimension_semantics=("arbitrary",), vmem_limit_bytes=VMEM_LIMIT),
        name="front_latent" if rope else "front_context",
    )(*args)


def _scores_t(q_heads_t, sources, kv_head):
    top = jnp.concatenate(q_heads_t, axis=1) if len(q_heads_t) > 1 else q_heads_t[0]
    zeros = jnp.zeros_like(top)
    q_aug = jnp.concatenate([top, zeros] if kv_head == 0 else [zeros, top], axis=0)
    scores = []
    for k, _, mask in sources:
        s = _dot(k(), q_aug)
        if mask is not None:
            s = jnp.where(mask(), s, NEG_INF)
        scores.append(s)
    return scores


def _softmax_pv_t(scores, sources, sinks, kv_head, tq, shift=None):
    m = shift
    if m is None:
        for s in scores:
            ms = jnp.max(s, axis=0, keepdims=True)
            m = ms if m is None else jnp.maximum(m, ms)
    if sinks is not None:
        sink = jnp.concatenate([jnp.full((1, tq), sk * LOG2E, F32) for sk in sinks], axis=1)
        m = jnp.maximum(m, sink)
    rows = slice(kv_head * HEAD_DIM, (kv_head + 1) * HEAD_DIM)
    r = None
    for s, src in zip(scores, sources):
        v_t = src[1](rows)
        v_aug = jnp.concatenate([v_t, jnp.ones((BF16_SUBLANES, v_t.shape[1]), BF16)], axis=0)
        pv = _dot(v_aug, jnp.exp2(s - m).astype(BF16))
        r = pv if r is None else r + pv
    denom = r[HEAD_DIM:HEAD_DIM + 1]
    if sinks is not None:
        denom = denom + jnp.exp2(sink - m)
    return r[:HEAD_DIM] * (1.0 / denom)


def _attention_branches(branches, heads_per_dot, tail):
    tq = branches[0][6].shape[0]
    items = [(br, kv_head, h0) for br in range(len(branches)) for kv_head in range(N_KV)
             for h0 in range(kv_head * GROUP, (kv_head + 1) * GROUP, heads_per_dot)]

    def q_heads_of(item):
        br, _, h0 = item
        qt_ref = branches[br][0]
        return [qt_ref[h * HEAD_DIM:(h + 1) * HEAD_DIM, :] for h in range(h0, h0 + heads_per_dot)]

    def scores_of(item):
        return _scores_t(q_heads_of(item), branches[item[0]][1], item[1])

    shifts = [SHIFT_MARGIN * jnp.sqrt(jnp.full((1, heads_per_dot * tq), branches[br][5][kv_head], F32))
              for br, kv_head, _ in items]
    worst = None
    for branch in branches:
        for sq in branch[5]:
            worst = sq if worst is None else jnp.maximum(worst, sq)
    bounded = worst * SHIFT_MARGIN ** 2 <= MAX_SAFE_SHIFT ** 2

    def run(use_shift):
        pending = scores_of(items[0])
        for i, (br, kv_head, h0) in enumerate(items):
            scores = pending
            if i + 1 < len(items):
                pending = scores_of(items[i + 1])
            _, sources, sink_ref, gate_block, z_block, _, pz_ref, z_ref = branches[br]
            sinks = None if sink_ref is None else [sink_ref(h) for h in range(h0, h0 + heads_per_dot)]
            o_t = _softmax_pv_t(scores, sources, sinks, kv_head, tq, shifts[i] if use_shift else None)
            for j in range(0, heads_per_dot, 2):
                pair = (h0 + j) // 2
                o = o_t[:, j * tq:(j + 2) * tq]
                o = jnp.concatenate([o[:, :tq], o[:, tq:]], axis=0).T
                gate = pz_ref[:, gate_block * WIDTH + pair * LANES: gate_block * WIDTH + (pair + 1) * LANES]
                z_ref[:, z_block * WIDTH + pair * LANES: z_block * WIDTH + (pair + 1) * LANES] = (
                    o * gate.astype(F32)).astype(BF16)
        tail()

    pl.when(bounded)(functools.partial(run, True))
    pl.when(jnp.logical_not(bounded))(functools.partial(run, False))


def _spatial_gate(pz_ref, ws_ref, bs_ref, z_ref, n_rows):
    for ch0 in range(0, n_rows // CHUNK, 2):
        chunks = [slice(ch * CHUNK, (ch + 1) * CHUNK) for ch in (ch0, ch0 + 1)]
        for g in range(C_GROUPS):
            cols = lambda blk: slice(blk * WIDTH + g * LANES, blk * WIDTH + (g + 1) * LANES)
            vn = jnp.concatenate([pz_ref[rows, cols(PZ_VN)] for rows in chunks], axis=1)
            mixed2 = _dot(ws_ref[g], vn)
            for j, rows in enumerate(chunks):
                mixed = mixed2[:, j * LANES:(j + 1) * LANES] + bs_ref[:, g * LANES:(g + 1) * LANES]
                zc = pz_ref[rows, cols(PZ_UC)].astype(F32) * mixed * pz_ref[rows, cols(PZ_GC)].astype(F32)
                z_ref[rows, 2 * WIDTH + g * LANES: 2 * WIDTH + (g + 1) * LANES] = zc.astype(BF16)


def _ref_source(k_ref, vt_ref):
    return (lambda: k_ref[...], lambda rows: vt_ref[rows, :], None)


def _mixer_context_kernel(qat_ref, qbt_ref, ka_ref, kb_ref, vat_ref, vbt_ref, pz_ref, stat_ref,
                          sink_ref, ws_ref, bs_ref, z_ref, *, layer, seq, tokens_per_stat):
    branches = []
    for s in range(pz_ref.shape[0] // seq):
        tok = slice(s * seq, (s + 1) * seq)
        t = lax.div(pl.program_id(0) * pz_ref.shape[0] + s * seq, tokens_per_stat)
        sq = lambda br, t=t: [stat_ref[t, br * N_KV + g] * stat_ref[t, STAT_Q + br] for g in range(N_KV)]
        views = (pz_ref.at[tok, :], z_ref.at[tok, :])
        branches += [
            (qat_ref.at[:, tok], [_ref_source(ka_ref.at[tok, :], vat_ref.at[:, tok])], None, PZ_GA, 0, sq(0)) + views,
            (qbt_ref.at[:, tok], [_ref_source(kb_ref.at[tok, :], vbt_ref.at[:, tok])],
             lambda h: sink_ref[layer, h], PZ_GB, 1, sq(1)) + views]
    _attention_branches(
        branches, GROUP, functools.partial(_spatial_gate, pz_ref, ws_ref, bs_ref, z_ref, pz_ref.shape[0]))


def _mixer_context(qat, qbt, ka, kb, vat, vbt, pz, stats, sink, ws, bs, *, layer, seq, tokens_per_stat):
    m = pz.shape[0]
    tok = CTX_SEQS_PER_STEP * seq
    qt_spec = pl.BlockSpec((WIDTH, tok), lambda i: (0, i))
    k_spec = pl.BlockSpec((tok, KV_WIDTH), lambda i: (i, 0))
    vt_spec = pl.BlockSpec((KV_WIDTH, tok), lambda i: (0, i))
    return pl.pallas_call(
        functools.partial(_mixer_context_kernel, layer=layer, seq=seq, tokens_per_stat=tokens_per_stat),
        grid=(m // tok,),
        in_specs=[
            qt_spec, qt_spec, k_spec, k_spec, vt_spec, vt_spec,
            pl.BlockSpec((tok, PZ_WIDTH), lambda i: (i, 0)),
            pl.BlockSpec(memory_space=pltpu.SMEM),
            pl.BlockSpec(memory_space=pltpu.SMEM),
            _layer_block((C_GROUPS, CHUNK, CHUNK), layer),
            _layer_block((CHUNK, WIDTH), layer),
        ],
        out_specs=pl.BlockSpec((tok, 3 * WIDTH), lambda i: (i, 0)),
        out_shape=jax.ShapeDtypeStruct((m, 3 * WIDTH), BF16),
        compiler_params=pltpu.CompilerParams(
            dimension_semantics=("arbitrary",), vmem_limit_bytes=VMEM_LIMIT),
        name="mixer_context",
    )(qat, qbt, ka, kb, vat, vbt, pz, stats, sink, ws, bs)


def _mixer_latent_kernel(qat_ref, qbt_ref, ka_ref, kb_ref, vat_ref, vbt_ref, pz_ref, stat_ref,
                         cak_ref, cav_ref, cbk_ref, cbv_ref, sink_ref, ws_ref, bs_ref,
                         z_ref, cak_s, cav_s, cbk_s, cbv_s, ck2_s, *, layer, seq, tq, heads_per_dot,
                         tokens_per_stat):
    band = tq + 2 * WINDOW
    step = pl.program_id(1)
    tiles = pz_ref.shape[0] // tq

    @pl.when(step == 0)
    def _():
        cak_s[...] = cak_ref[0, 0].T.astype(BF16)
        cbk_s[...] = cbk_ref[0, 0].T.astype(BF16)
        cav_s[...] = cav_ref[0, 0].astype(BF16)
        cbv_s[...] = cbv_ref[0, 0].astype(BF16)
        for br, cache_ref in enumerate((cak_ref, cbk_ref)):
            ck = cache_ref[0, 0]
            for g in range(N_KV):
                c2 = jnp.sum(jnp.square(ck[g * HEAD_DIM:(g + 1) * HEAD_DIM]), axis=0, keepdims=True)
                ck2_s[br * N_KV + g] = jnp.max(c2)

    def window_source(qi):
        start_blk = jnp.clip(qi * (tq // LANES) - WINDOW // LANES, 0, (seq - band) // LANES)
        start = pl.multiple_of(start_blk * LANES, LANES)

        def in_window():
            k_pos = start + lax.broadcasted_iota(jnp.int32, (band, heads_per_dot * tq), 0)
            q_pos = qi * tq + (lax.broadcasted_iota(jnp.int32, (band, heads_per_dot * tq), 1) & (tq - 1))
            return jnp.abs(q_pos - k_pos) <= WINDOW

        return (lambda: kb_ref[pl.ds(start, band), :],
                lambda rows: jnp.concatenate(
                    [vbt_ref[start_blk + j, rows, :] for j in range(band // LANES)], axis=1),
                in_window)

    def sq(br, qi):
        b = pl.program_id(0)
        tps = seq // tokens_per_stat
        q2 = stat_ref[b * tps + lax.div(qi * tq, tokens_per_stat), STAT_Q + br]
        out = []
        for g in range(N_KV):
            k2 = ck2_s[br * N_KV + g]
            for j in range(tps):
                k2 = jnp.maximum(k2, stat_ref[b * tps + j, br * N_KV + g])
            out.append(q2 * k2)
        return out

    branches = []
    for t in range(tiles):
        qi = step * tiles + t
        tok = slice(t * tq, (t + 1) * tq)
        views = (pz_ref.at[tok, :], z_ref.at[tok, :])
        branches += [
            (qat_ref.at[:, tok], [_ref_source(ka_ref, vat_ref), _ref_source(cak_s, cav_s)], None, PZ_GA, 0,
             sq(0, qi)) + views,
            (qbt_ref.at[:, tok], [window_source(qi), _ref_source(cbk_s, cbv_s)],
             lambda h: sink_ref[layer, h], PZ_GB, 1, sq(1, qi)) + views]
    _attention_branches(
        branches, heads_per_dot,
        functools.partial(_spatial_gate, pz_ref, ws_ref, bs_ref, z_ref, pz_ref.shape[0]))


def _mixer_latent(qat, qbt, ka, kb, vat, vbt, pz, stats, cak, cav, cbk, cbv, sink, ws, bs, *, layer, seq, tq,
                  tokens_per_stat):
    m = pz.shape[0]
    n_seq = m // seq
    tok = LATENT_TILES_PER_STEP * tq
    nq = seq // tok
    past = cak.shape[3]
    qt_spec = pl.BlockSpec((WIDTH, tok), lambda b, q: (0, b * nq + q))
    k_spec = pl.BlockSpec((seq, KV_WIDTH), lambda b, q: (b, 0))
    cache_spec = pl.BlockSpec((1, 1, KV_WIDTH, past), lambda b, q: (b, layer, 0, 0))
    return pl.pallas_call(
        functools.partial(_mixer_latent_kernel, layer=layer, seq=seq, tq=tq, heads_per_dot=2,
                          tokens_per_stat=tokens_per_stat),
        grid=(n_seq, nq),
        in_specs=[
            qt_spec, qt_spec, k_spec, k_spec,
            pl.BlockSpec((KV_WIDTH, seq), lambda b, q: (0, b)),
            pl.BlockSpec((seq // LANES, KV_WIDTH, LANES), lambda b, q: (b, 0, 0)),
            pl.BlockSpec((tok, PZ_WIDTH), lambda b, q: (b * nq + q, 0)),
            pl.BlockSpec(memory_space=pltpu.SMEM),
            cache_spec, cache_spec, cache_spec, cache_spec,
            pl.BlockSpec(memory_space=pltpu.SMEM),
            _layer_block((C_GROUPS, CHUNK, CHUNK), layer),
            _layer_block((CHUNK, WIDTH), layer),
        ],
        out_specs=pl.BlockSpec((tok, 3 * WIDTH), lambda b, q: (b * nq + q, 0)),
        out_shape=jax.ShapeDtypeStruct((m, 3 * WIDTH), BF16),
        scratch_shapes=[pltpu.VMEM((past, KV_WIDTH), BF16), pltpu.VMEM((KV_WIDTH, past), BF16),
                        pltpu.VMEM((past, KV_WIDTH), BF16), pltpu.VMEM((KV_WIDTH, past), BF16),
                        pltpu.SMEM((2 * N_KV,), F32)],
        compiler_params=pltpu.CompilerParams(
            dimension_semantics=("arbitrary", "arbitrary"), vmem_limit_bytes=VMEM_LIMIT),
        name="mixer_latent",
    )(qat, qbt, ka, kb, vat, vbt, pz, stats, cak, cav, cbk, cbv, sink, ws, bs)


def _back_kernel(x_ref, z_ref, mod_ref, wg_ref, bg_ref, wpa_ref, wpb_ref, wpc_ref, wo_ref,
                 lg_ref, lb_ref, o_ref, *, n_sub):
    mod = mod_ref[...]
    sub = x_ref.shape[0] // n_sub
    wp_refs = (wpa_ref, wpb_ref, wpc_ref)

    def stages(s):
        rows = slice(s * sub, (s + 1) * sub)
        st = {}

        def norm():
            st["h"] = _modulated_norm(x_ref[rows, :], mod).astype(BF16)

        def branch(br):
            cols = slice(br * D_MODEL, (br + 1) * D_MODEL)
            g = jax.nn.sigmoid(_dot(st["h"], wg_ref[:, cols]) + bg_ref[:, cols])
            y = _dot(z_ref[rows, br * WIDTH:(br + 1) * WIDTH], wp_refs[br][...])
            st["mix"] = g * y if br == 0 else st["mix"] + g * y

        def out_proj():
            st["y"] = _dot(st["mix"].astype(BF16), wo_ref[...])

        def residual_norm():
            r = DEEPNORM_ALPHA * x_ref[rows, :] + mod[:, 2 * D_MODEL:] * st["y"]
            mu = jnp.mean(r, axis=-1, keepdims=True)
            rc = r - mu
            var = jnp.mean(rc * rc, axis=-1, keepdims=True)
            o_ref[rows, :] = rc * lax.rsqrt(var + EPS) * lg_ref[...] + lb_ref[...]

        return [norm] + [functools.partial(branch, br) for br in range(3)] + [out_proj, residual_norm]

    chains = [stages(s) for s in range(n_sub)]
    n_stage = len(chains[0])
    for t in range(n_stage + (n_sub - 1) * BACK_STAGE_LAG):
        for s in range(n_sub):
            k = t - s * BACK_STAGE_LAG
            if 0 <= k < n_stage:
                chains[s][k]()


def _back(x, z, mod, wg, bg, wpa, wpb, wpc, wo, lg, lb, *, layer, tm, mod_row, rows_per_mod, name):
    m = x.shape[0]
    tiles_per_mod = rows_per_mod // tm
    return pl.pallas_call(
        functools.partial(_back_kernel, n_sub=BACK_SUB_TILES),
        grid=(m // tm,),
        in_specs=[
            pl.BlockSpec((tm, D_MODEL), lambda i: (i, 0)),
            pl.BlockSpec((tm, 3 * WIDTH), lambda i: (i, 0)),
            pl.BlockSpec((None, None, 1, 3 * D_MODEL), lambda i: (layer, mod_row + i // tiles_per_mod, 0, 0)),
            _layer_block((D_MODEL, 3 * D_MODEL), layer),
            _layer_block((1, 3 * D_MODEL), layer),
            _layer_block((WIDTH, D_MODEL), layer),
            _layer_block((WIDTH, D_MODEL), layer),
            _layer_block((WIDTH, D_MODEL), layer),
            _layer_block((D_MODEL, D_MODEL), layer),
            _layer_block((1, D_MODEL), layer),
            _layer_block((1, D_MODEL), layer),
        ],
        out_specs=pl.BlockSpec((tm, D_MODEL), lambda i: (i, 0)),
        out_shape=jax.ShapeDtypeStruct((m, D_MODEL), F32),
        compiler_params=pltpu.CompilerParams(
            dimension_semantics=("arbitrary",), vmem_limit_bytes=VMEM_LIMIT),
        name=name,
    )(x, z, mod, wg, bg, wpa, wpb, wpc, wo, lg, lb)


def _rope_tables(n_tokens):
    rows = n_tokens // GRID_W
    row = jnp.repeat(jnp.arange(rows, dtype=F32), GRID_W)
    col = jnp.tile(jnp.arange(GRID_W, dtype=F32), rows)
    inv_freq = jnp.power(ROPE_BASE, -jnp.arange(ROPE_FREQS, dtype=F32) / ROPE_FREQS)
    ang = jnp.concatenate([inv_freq[:, None] * row[None], inv_freq[:, None] * col[None]], axis=0)
    return jnp.cos(ang), jnp.sin(ang)


def _transposed_cache(t):
    b, depth, past = t.shape[:3]
    return jnp.transpose(t, (0, 1, 3, 4, 2)).reshape(b, depth, KV_WIDTH, past)


def kernel(x_prompt, x_sample, cache_a_k, cache_a_v, cache_b_k, cache_b_v, c, c_ctx, w_mod, b_mod,
           w_in, q_norm, k_norm, sink, sgu_g, sgu_b, w_spatial, b_spatial, w_proj_a, w_proj_b,
           w_proj_c, w_gate, b_gate, w_out, ln_g, ln_b):
    batch, seq, _ = x_prompt.shape
    dec_batch, dec_seq, _ = x_sample.shape
    tm = 512

    cvecs = jnp.concatenate(
        [c_ctx[None], c, jnp.zeros((MOD_ROWS - 1 - dec_batch, D_MODEL), F32)], axis=0)
    mod = _modulation(cvecs, w_mod, b_mod)

    rope_tables = _rope_tables(dec_seq)
    caches = [_transposed_cache(t) for t in (cache_a_k, cache_a_v, cache_b_k, cache_b_v)]

    w_in_bf = w_in.astype(BF16)
    wg_bf, wo_bf, ws_bf = w_gate.astype(BF16), w_out.astype(BF16), w_spatial.astype(BF16)
    wpa_bf, wpb_bf, wpc_bf = w_proj_a.astype(BF16), w_proj_b.astype(BF16), w_proj_c.astype(BF16)
    bg3, lg3, lb3 = b_gate[:, None], ln_g[:, None], ln_b[:, None]
    sg3, sb3 = sgu_g[:, None], sgu_b[:, None]
    mod4 = mod[:, :, None]
    qn = jnp.broadcast_to(q_norm[:, :, None], (DEPTH, HEAD_DIM, tm))
    kn = jnp.broadcast_to(k_norm[:, :, None], (DEPTH, HEAD_DIM, tm))
    bs = jnp.repeat(jnp.swapaxes(b_spatial, 1, 2), LANES, axis=2)

    xp = x_prompt.reshape(batch * seq, D_MODEL)
    xs = x_sample.reshape(dec_batch * dec_seq, D_MODEL)
    states = ()
    for l in range(DEPTH):
        *acts, st_ak, st_av, st_bk, st_bv = _front(
            xp, mod4, w_in_bf, qn, kn, sg3, sb3, None, states, layer=l, tm=tm, seq=seq)
        states = [st_ak, st_av, st_bk, st_bv]
        z = _mixer_context(*acts, sink, ws_bf, bs,
                           layer=l, seq=seq, tokens_per_stat=tm)
        xp = _back(xp, z, mod4, wg_bf, bg3, wpa_bf, wpb_bf, wpc_bf, wo_bf, lg3, lb3,
                   layer=l, tm=2 * tm, mod_row=0, rows_per_mod=batch * seq, name="back_context")

        acts = _front(xs, mod4, w_in_bf, qn, kn, sg3, sb3, rope_tables, None,
                      layer=l, tm=tm, seq=dec_seq)
        z = _mixer_latent(*acts, *caches, sink, ws_bf, bs,
                          layer=l, seq=dec_seq, tq=256, tokens_per_stat=tm)
        xs = _back(xs, z, mod4, wg_bf, bg3, wpa_bf, wpb_bf, wpc_bf, wo_bf, lg3, lb3,
                   layer=l, tm=2 * tm, mod_row=1, rows_per_mod=dec_seq, name="back_latent")

    def state(t):
        return jnp.transpose(t.reshape(batch, DEPTH, N_KV, HEAD_DIM, seq), (0, 1, 4, 2, 3))

    return (xp.reshape(batch, seq, D_MODEL), xs.reshape(dec_batch, dec_seq, D_MODEL),
            *[state(t) for t in states])
```

```python
import functools

import jax
import jax.numpy as jnp
from jax import lax
from jax.experimental import pallas as pl
from jax.experimental.pallas import tpu as pltpu

D_MODEL = 1024
DEPTH = 4
GRID_W = 64
HEAD_DIM = 64
HALF = HEAD_DIM // 2
N_HEADS = 8
N_KV = 2
GROUP = N_HEADS // N_KV
KV_WIDTH = N_KV * HEAD_DIM
WIDTH = 512
C_GROUPS = 4
CHUNK = 128
WINDOW = 128
IN_WIDTH = 4096
ROPE_BASE = 10000.0
ROPE_FREQS = HEAD_DIM // 4
EPS = 1e-6
NEG_INF = -1e30
DEEPNORM_ALPHA = (2 * DEPTH) ** 0.25
LOG2E = 1.4426950408889634
Q_SCALE = HEAD_DIM ** -0.5 * LOG2E

LANES = 128
BF16_SUBLANES = 16
MOD_ROWS = 8

_QA, _KA, _VA, _GA = 0, 512, 640, 768
_QB, _KB, _VB, _GB = 1280, 1792, 1920, 2048
_UC, _VC, _GC = 2560, 3072, 3584

(PZ_GA, PZ_GB, PZ_UC, PZ_VN, PZ_GC) = range(5)
PZ_WIDTH = 5 * WIDTH

BF16 = jnp.bfloat16
F32 = jnp.float32

VMEM_LIMIT = 56 * 1024 * 1024
SHIFT_MARGIN = 1.02
MAX_SAFE_SHIFT = 48.0
STAT_ROWS = 8
STAT_Q = 2 * N_KV
CTX_SEQS_PER_STEP = 4
LATENT_TILES_PER_STEP = 1
BACK_SUB_TILES = 4
BACK_STAGE_LAG = 1


def _silu(t):
    return t * jax.nn.sigmoid(t)


def _dot(a, b):
    return jnp.dot(a, b, preferred_element_type=F32)


def _layer_block(shape, layer):
    zeros = (0,) * len(shape)
    return pl.BlockSpec((None,) + tuple(shape), lambda *_: (layer,) + zeros,
                        pipeline_mode=pl.Buffered(1))


def _mod_kernel(c_ref, w_ref, b_ref, o_ref):
    s = _silu(c_ref[...]).astype(BF16)
    o_ref[0] = _dot(s, w_ref[0].astype(BF16)) + b_ref[0]


def _modulation(cvecs, w_mod, b_mod):
    tn = 3 * D_MODEL
    return pl.pallas_call(
        _mod_kernel,
        grid=(DEPTH, 3 * D_MODEL // tn),
        in_specs=[
            pl.BlockSpec((MOD_ROWS, D_MODEL), lambda l, j: (0, 0)),
            pl.BlockSpec((1, D_MODEL, tn), lambda l, j: (l, 0, j)),
            pl.BlockSpec((1, 1, tn), lambda l, j: (l, 0, j)),
        ],
        out_specs=pl.BlockSpec((1, MOD_ROWS, tn), lambda l, j: (l, 0, j)),
        out_shape=jax.ShapeDtypeStruct((DEPTH, MOD_ROWS, 3 * D_MODEL), F32),
        compiler_params=pltpu.CompilerParams(
            dimension_semantics=("arbitrary", "arbitrary"), vmem_limit_bytes=VMEM_LIMIT),
        name="modulation",
    )(cvecs, w_mod, b_mod.reshape(DEPTH, 1, 3 * D_MODEL))


def _modulated_norm(x, mod):
    mu = jnp.mean(x, axis=-1, keepdims=True)
    xc = x - mu
    var = jnp.mean(xc * xc, axis=-1, keepdims=True)
    xn = xc * lax.rsqrt(var + EPS)
    return xn * (1.0 + mod[:, D_MODEL:2 * D_MODEL]) + mod[:, :D_MODEL]


def _write_token_blocks(ref, t):
    w = ref.shape[2]
    for j in range(ref.shape[0]):
        ref[j] = t[:, j * w:(j + 1) * w].astype(ref.dtype)


def _front_kernel(*refs, rope, state_layer, state_start):
    refs = list(refs)
    x_ref, mod_ref, w_ref, qn_ref, kn_ref, sg_ref, sb_ref = refs[:7]
    refs = refs[7:]
    if rope:
        cos_ref, sin_ref = refs[:2]
        refs = refs[2:]
    with_state = state_layer is not None
    if with_state and not state_start:
        refs = refs[4:]
    qat_ref, qbt_ref, ka_ref, kb_ref, vat_ref, vbt_ref, pz_ref, stat_ref = refs[:8]
    st_refs = refs[8:] if with_state else None

    h = _modulated_norm(x_ref[...], mod_ref[...]).astype(BF16)

    def rotary_t(t):
        if not rope:
            return t
        x1, x2 = t[:HALF], t[HALF:]
        c, s = cos_ref[...], sin_ref[...]
        return jnp.concatenate([x1 * c - x2 * s, x2 * c + x1 * s], axis=0)

    def rms_t(t, gain):
        ms = jnp.mean(t * t, axis=0, keepdims=True)
        return t * lax.rsqrt(ms + EPS) * gain

    def proj_t(lo, width):
        return lax.dot_general(w_ref[:, lo:lo + width], h, (((0,), (1,)), ((), ())),
                               preferred_element_type=F32)

    def proj(lo):
        return _dot(h, w_ref[:, lo:lo + WIDTH])

    def heads_of(t):
        return [t[j * HEAD_DIM:(j + 1) * HEAD_DIM] for j in range(t.shape[0] // HEAD_DIM)]

    def store(block, val):
        pz_ref[:, block * WIDTH:(block + 1) * WIDTH] = val.astype(BF16)

    def max_sq_norm(heads):
        m = None
        for t in heads:
            r = jnp.sum(t * t, axis=0, keepdims=True)
            m = r if m is None else jnp.maximum(m, r)
        return jnp.max(m)

    qn, kn = qn_ref[...], kn_ref[...]
    qa_t = proj_t(_QA, WIDTH)
    ga = proj(_GA)
    qa = [rotary_t(rms_t(t, qn)) * Q_SCALE for t in heads_of(qa_t)]
    qat_ref[...] = jnp.concatenate(qa, axis=0).astype(BF16)
    qb_t = proj_t(_QB, WIDTH)
    store(PZ_GA, _silu(ga))
    gb = proj(_GB)
    qb = [rotary_t(t) * Q_SCALE for t in heads_of(qb_t)]
    qbt_ref[...] = jnp.concatenate(qb, axis=0).astype(BF16)
    kva_t = proj_t(_KA, 2 * KV_WIDTH)
    kvb_t = proj_t(_KB, 2 * KV_WIDTH)
    store(PZ_GB, _silu(gb))
    uc = proj(_UC)

    ka = [rms_t(t, kn) for t in heads_of(kva_t[:KV_WIDTH])]
    kb = heads_of(kvb_t[:KV_WIDTH])
    va, vb = kva_t[KV_WIDTH:], kvb_t[KV_WIDTH:]
    ka_ref[...] = jnp.concatenate([rotary_t(t) for t in ka], axis=0).T.astype(BF16)
    kb_ref[...] = jnp.concatenate([rotary_t(t) for t in kb], axis=0).T.astype(BF16)
    vat_ref[...] = va.astype(BF16)
    if len(vbt_ref.shape) == 3:
        _write_token_blocks(vbt_ref, vb)
    else:
        vbt_ref[...] = vb.astype(BF16)
    stats = [max_sq_norm([t]) for t in ka + kb] + [max_sq_norm(qa), max_sq_norm(qb)]
    for r in range(STAT_ROWS):
        stat_ref[pl.program_id(0), r] = stats[r] if r < len(stats) else jnp.float32(0.0)
    if with_state:
        for st_ref, t in zip(st_refs, (jnp.concatenate(ka, axis=0), va, jnp.concatenate(kb, axis=0), vb)):
            if state_start:
                for d in range(st_ref.shape[1]):
                    if d != state_layer:
                        st_ref[:, d] = jnp.zeros(st_ref.shape[:1] + st_ref.shape[2:], F32)
                st_ref = st_ref.at[:, state_layer]
            _write_token_blocks(st_ref, t)

    store(PZ_UC, uc)
    vc = proj(_VC)
    gc = proj(_GC)
    mu = jnp.mean(vc, axis=-1, keepdims=True)
    vcc = vc - mu
    var = jnp.mean(vcc * vcc, axis=-1, keepdims=True)
    store(PZ_VN, vcc * lax.rsqrt(var + EPS) * sg_ref[...] + sb_ref[...])
    store(PZ_GC, _silu(gc))


def _front(x, mod, w_in, qn, kn, sg, sb, rope_tables, states, *, layer, tm, seq):
    with_state = states is not None
    state_start = with_state and not states
    m = x.shape[0]
    n_seq = m // seq
    rope = rope_tables is not None
    tps = max(seq // tm, 1)
    mod_idx = (lambda i: (layer, 0, 0, 0)) if tm >= seq else (lambda i: (layer, 1 + i // tps, 0, 0))
    in_specs = [
        pl.BlockSpec((tm, D_MODEL), lambda i: (i, 0)),
        pl.BlockSpec((None, None, 1, 3 * D_MODEL), mod_idx),
        _layer_block((D_MODEL, IN_WIDTH), layer),
        _layer_block((HEAD_DIM, tm), layer),
        _layer_block((HEAD_DIM, tm), layer),
        _layer_block((1, WIDTH), layer),
        _layer_block((1, WIDTH), layer),
    ]
    args = [x, mod, w_in, qn, kn, sg, sb]
    if rope:
        in_specs += [pl.BlockSpec((HALF, tm), lambda i: (0, i % tps))] * 2
        args += list(rope_tables)
    qt_spec = pl.BlockSpec((WIDTH, tm), lambda i: (0, i))
    k_spec = pl.BlockSpec((tm, KV_WIDTH), lambda i: (i, 0))
    vt_shape = (KV_WIDTH, m)
    vt_spec = pl.BlockSpec((KV_WIDTH, tm), lambda i: (0, i))
    if rope:
        vbt_shape = (m // LANES, KV_WIDTH, LANES)
        vbt_spec = pl.BlockSpec((tm // LANES, KV_WIDTH, LANES), lambda i: (i, 0, 0))
    else:
        vbt_shape, vbt_spec = vt_shape, vt_spec
    out_specs = [qt_spec, qt_spec, k_spec, k_spec, vt_spec, vbt_spec,
                 pl.BlockSpec((tm, PZ_WIDTH), lambda i: (i, 0)),
                 pl.BlockSpec(memory_space=pltpu.SMEM)]
    out_shape = [jax.ShapeDtypeStruct((WIDTH, m), BF16)] * 2
    out_shape += [jax.ShapeDtypeStruct((m, KV_WIDTH), BF16)] * 2
    out_shape += [jax.ShapeDtypeStruct(vt_shape, BF16), jax.ShapeDtypeStruct(vbt_shape, BF16)]
    out_shape += [jax.ShapeDtypeStruct((m, PZ_WIDTH), BF16), jax.ShapeDtypeStruct((m // tm, STAT_ROWS), F32)]
    aliases = {}
    if state_start:
        out_specs += [pl.BlockSpec((tm // seq, DEPTH, KV_WIDTH, seq), lambda i: (i, 0, 0, 0))] * 4
    elif with_state:
        aliases = {len(args) + j: len(out_shape) + j for j in range(4)}
        in_specs += [pl.BlockSpec(memory_space=pl.ANY)] * 4
        args += list(states)
        out_specs += [pl.BlockSpec((tm // seq, None, KV_WIDTH, seq), lambda i: (i, layer, 0, 0))] * 4
    if with_state:
        out_shape += [jax.ShapeDtypeStruct((n_seq, DEPTH, KV_WIDTH, seq), F32)] * 4
    return pl.pallas_call(
        functools.partial(_front_kernel, rope=rope, state_layer=layer if with_state else None,
                          state_start=state_start),
        grid=(m // tm,),
        in_specs=in_specs,
        out_specs=out_specs,
        out_shape=out_shape,
        input_output_aliases=aliases,
        compiler_params=pltpu.CompilerParams(
            dimension_semantics=("arbitrary",), vmem_limit_bytes=VMEM_LIMIT),
        name="front_latent" if rope else "front_context",
    )(*args)


def _scores_t(q_heads_t, sources, kv_head):
    top = jnp.concatenate(q_heads_t, axis=1) if len(q_heads_t) > 1 else q_heads_t[0]
    zeros = jnp.zeros_like(top)
    q_aug = jnp.concatenate([top, zeros] if kv_head == 0 else [zeros, top], axis=0)
    scores = []
    for k, _, mask in sources:
        s = _dot(k(), q_aug)
        if mask is not None:
            s = jnp.where(mask(), s, NEG_INF)
        scores.append(s)
    return scores


def _softmax_pv_t(scores, sources, sinks, kv_head, tq, shift=None):
    m = shift
    if m is None:
        for s in scores:
            ms = jnp.max(s, axis=0, keepdims=True)
            m = ms if m is None else jnp.maximum(m, ms)
    if sinks is not None:
        sink = jnp.concatenate([jnp.full((1, tq), sk * LOG2E, F32) for sk in sinks], axis=1)
        m = jnp.maximum(m, sink)
    rows = slice(kv_head * HEAD_DIM, (kv_head + 1) * HEAD_DIM)
    r = None
    for s, src in zip(scores, sources):
        v_t = src[1](rows)
        v_aug = jnp.concatenate([v_t, jnp.ones((BF16_SUBLANES, v_t.shape[1]), BF16)], axis=0)
        pv = _dot(v_aug, jnp.exp2(s - m).astype(BF16))
        r = pv if r is None else r + pv
    denom = r[HEAD_DIM:HEAD_DIM + 1]
    if sinks is not None:
        denom = denom + jnp.exp2(sink - m)
    return r[:HEAD_DIM] * (1.0 / denom)


def _attention_branches(branches, heads_per_dot, tail):
    tq = branches[0][6].shape[0]
    items = [(br, kv_head, h0) for br in range(len(branches)) for kv_head in range(N_KV)
             for h0 in range(kv_head * GROUP, (kv_head + 1) * GROUP, heads_per_dot)]

    def q_heads_of(item):
        br, _, h0 = item
        qt_ref = branches[br][0]
        return [qt_ref[h * HEAD_DIM:(h + 1) * HEAD_DIM, :] for h in range(h0, h0 + heads_per_dot)]

    def scores_of(item):
        return _scores_t(q_heads_of(item), branches[item[0]][1], item[1])

    shifts = [SHIFT_MARGIN * jnp.sqrt(jnp.full((1, heads_per_dot * tq), branches[br][5][kv_head], F32))
              for br, kv_head, _ in items]
    worst = None
    for branch in branches:
        for sq in branch[5]:
            worst = sq if worst is None else jnp.maximum(worst, sq)
    bounded = worst * SHIFT_MARGIN ** 2 <= MAX_SAFE_SHIFT ** 2

    def run(use_shift):
        pending = scores_of(items[0])
        for i, (br, kv_head, h0) in enumerate(items):
            scores = pending
            if i + 1 < len(items):
                pending = scores_of(items[i + 1])
            _, sources, sink_ref, gate_block, z_block, _, pz_ref, z_ref = branches[br]
            sinks = None if sink_ref is None else [sink_ref(h) for h in range(h0, h0 + heads_per_dot)]
            o_t = _softmax_pv_t(scores, sources, sinks, kv_head, tq, shifts[i] if use_shift else None)
            for j in range(0, heads_per_dot, 2):
                pair = (h0 + j) // 2
                o = o_t[:, j * tq:(j + 2) * tq]
                o = jnp.concatenate([o[:, :tq], o[:, tq:]], axis=0).T
                gate = pz_ref[:, gate_block * WIDTH + pair * LANES: gate_block * WIDTH + (pair + 1) * LANES]
                z_ref[:, z_block * WIDTH + pair * LANES: z_block * WIDTH + (pair + 1) * LANES] = (
                    o * gate.astype(F32)).astype(BF16)
        tail()

    pl.when(bounded)(functools.partial(run, True))
    pl.when(jnp.logical_not(bounded))(functools.partial(run, False))


def _spatial_gate(pz_ref, ws_ref, bs_ref, z_ref, n_rows):
    for ch0 in range(0, n_rows // CHUNK, 2):
        chunks = [slice(ch * CHUNK, (ch + 1) * CHUNK) for ch in (ch0, ch0 + 1)]
        for g in range(C_GROUPS):
            cols = lambda blk: slice(blk * WIDTH + g * LANES, blk * WIDTH + (g + 1) * LANES)
            vn = jnp.concatenate([pz_ref[rows, cols(PZ_VN)] for rows in chunks], axis=1)
            mixed2 = _dot(ws_ref[g], vn)
            for j, rows in enumerate(chunks):
                mixed = mixed2[:, j * LANES:(j + 1) * LANES] + bs_ref[:, g * LANES:(g + 1) * LANES]
                zc = pz_ref[rows, cols(PZ_UC)].astype(F32) * mixed * pz_ref[rows, cols(PZ_GC)].astype(F32)
                z_ref[rows, 2 * WIDTH + g * LANES: 2 * WIDTH + (g + 1) * LANES] = zc.astype(BF16)


def _ref_source(k_ref, vt_ref):
    return (lambda: k_ref[...], lambda rows: vt_ref[rows, :], None)


def _mixer_context_kernel(qat_ref, qbt_ref, ka_ref, kb_ref, vat_ref, vbt_ref, pz_ref, stat_ref,
                          sink_ref, ws_ref, bs_ref, z_ref, *, layer, seq, tokens_per_stat):
    branches = []
    for s in range(pz_ref.shape[0] // seq):
        tok = slice(s * seq, (s + 1) * seq)
        t = lax.div(pl.program_id(0) * pz_ref.shape[0] + s * seq, tokens_per_stat)
        sq = lambda br, t=t: [stat_ref[t, br * N_KV + g] * stat_ref[t, STAT_Q + br] for g in range(N_KV)]
        views = (pz_ref.at[tok, :], z_ref.at[tok, :])
        branches += [
            (qat_ref.at[:, tok], [_ref_source(ka_ref.at[tok, :], vat_ref.at[:, tok])], None, PZ_GA, 0, sq(0)) + views,
            (qbt_ref.at[:, tok], [_ref_source(kb_ref.at[tok, :], vbt_ref.at[:, tok])],
             lambda h: sink_ref[layer, h], PZ_GB, 1, sq(1)) + views]
    _attention_branches(
        branches, GROUP, functools.partial(_spatial_gate, pz_ref, ws_ref, bs_ref, z_ref, pz_ref.shape[0]))


def _mixer_context(qat, qbt, ka, kb, vat, vbt, pz, stats, sink, ws, bs, *, layer, seq, tokens_per_stat):
    m = pz.shape[0]
    tok = CTX_SEQS_PER_STEP * seq
    qt_spec = pl.BlockSpec((WIDTH, tok), lambda i: (0, i))
    k_spec = pl.BlockSpec((tok, KV_WIDTH), lambda i: (i, 0))
    vt_spec = pl.BlockSpec((KV_WIDTH, tok), lambda i: (0, i))
    return pl.pallas_call(
        functools.partial(_mixer_context_kernel, layer=layer, seq=seq, tokens_per_stat=tokens_per_stat),
        grid=(m // tok,),
        in_specs=[
            qt_spec, qt_spec, k_spec, k_spec, vt_spec, vt_spec,
            pl.BlockSpec((tok, PZ_WIDTH), lambda i: (i, 0)),
            pl.BlockSpec(memory_space=pltpu.SMEM),
            pl.BlockSpec(memory_space=pltpu.SMEM),
            _layer_block((C_GROUPS, CHUNK, CHUNK), layer),
            _layer_block((CHUNK, WIDTH), layer),
        ],
        out_specs=pl.BlockSpec((tok, 3 * WIDTH), lambda i: (i, 0)),
        out_shape=jax.ShapeDtypeStruct((m, 3 * WIDTH), BF16),
        compiler_params=pltpu.CompilerParams(
            dimension_semantics=("arbitrary",), vmem_limit_bytes=VMEM_LIMIT),
        name="mixer_context",
    )(qat, qbt, ka, kb, vat, vbt, pz, stats, sink, ws, bs)


def _mixer_latent_kernel(qat_ref, qbt_ref, ka_ref, kb_ref, vat_ref, vbt_ref, pz_ref, stat_ref,
                         cak_ref, cav_ref, cbk_ref, cbv_ref, sink_ref, ws_ref, bs_ref,
                         z_ref, cak_s, cav_s, cbk_s, cbv_s, ck2_s, *, layer, seq, tq, heads_per_dot,
                         tokens_per_stat):
    band = tq + 2 * WINDOW
    step = pl.program_id(1)
    tiles = pz_ref.shape[0] // tq

    @pl.when(step == 0)
    def _():
        cak_s[...] = cak_ref[0, 0].T.astype(BF16)
        cbk_s[...] = cbk_ref[0, 0].T.astype(BF16)
        cav_s[...] = cav_ref[0, 0].astype(BF16)
        cbv_s[...] = cbv_ref[0, 0].astype(BF16)
        for br, cache_ref in enumerate((cak_ref, cbk_ref)):
            ck = cache_ref[0, 0]
            for g in range(N_KV):
                c2 = jnp.sum(jnp.square(ck[g * HEAD_DIM:(g + 1) * HEAD_DIM]), axis=0, keepdims=True)
                ck2_s[br * N_KV + g] = jnp.max(c2)

    def window_source(qi):
        start_blk = jnp.clip(qi * (tq // LANES) - WINDOW // LANES, 0, (seq - band) // LANES)
        start = pl.multiple_of(start_blk * LANES, LANES)

        def in_window():
            k_pos = start + lax.broadcasted_iota(jnp.int32, (band, heads_per_dot * tq), 0)
            q_pos = qi * tq + (lax.broadcasted_iota(jnp.int32, (band, heads_per_dot * tq), 1) & (tq - 1))
            return jnp.abs(q_pos - k_pos) <= WINDOW

        return (lambda: kb_ref[pl.ds(start, band), :],
                lambda rows: jnp.concatenate(
                    [vbt_ref[start_blk + j, rows, :] for j in range(band // LANES)], axis=1),
                in_window)

    def sq(br, qi):
        b = pl.program_id(0)
        tps = seq // tokens_per_stat
        q2 = stat_ref[b * tps + lax.div(qi * tq, tokens_per_stat), STAT_Q + br]
        out = []
        for g in range(N_KV):
            k2 = ck2_s[br * N_KV + g]
            for j in range(tps):
                k2 = jnp.maximum(k2, stat_ref[b * tps + j, br * N_KV + g])
            out.append(q2 * k2)
        return out

    branches = []
    for t in range(tiles):
        qi = step * tiles + t
        tok = slice(t * tq, (t + 1) * tq)
        views = (pz_ref.at[tok, :], z_ref.at[tok, :])
        branches += [
            (qat_ref.at[:, tok], [_ref_source(ka_ref, vat_ref), _ref_source(cak_s, cav_s)], None, PZ_GA, 0,
             sq(0, qi)) + views,
            (qbt_ref.at[:, tok], [window_source(qi), _ref_source(cbk_s, cbv_s)],
             lambda h: sink_ref[layer, h], PZ_GB, 1, sq(1, qi)) + views]
    _attention_branches(
        branches, heads_per_dot,
        functools.partial(_spatial_gate, pz_ref, ws_ref, bs_ref, z_ref, pz_ref.shape[0]))


def _mixer_latent(qat, qbt, ka, kb, vat, vbt, pz, stats, cak, cav, cbk, cbv, sink, ws, bs, *, layer, seq, tq,
                  tokens_per_stat):
    m = pz.shape[0]
    n_seq = m // seq
    tok = LATENT_TILES_PER_STEP * tq
    nq = seq // tok
    past = cak.shape[3]
    qt_spec = pl.BlockSpec((WIDTH, tok), lambda b, q: (0, b * nq + q))
    k_spec = pl.BlockSpec((seq, KV_WIDTH), lambda b, q: (b, 0))
    cache_spec = pl.BlockSpec((1, 1, KV_WIDTH, past), lambda b, q: (b, layer, 0, 0))
    return pl.pallas_call(
        functools.partial(_mixer_latent_kernel, layer=layer, seq=seq, tq=tq, heads_per_dot=2,
                          tokens_per_stat=tokens_per_stat),
        grid=(n_seq, nq),
        in_specs=[
            qt_spec, qt_spec, k_spec, k_spec,
            pl.BlockSpec((KV_WIDTH, seq), lambda b, q: (0, b)),
            pl.BlockSpec((seq // LANES, KV_WIDTH, LANES), lambda b, q: (b, 0, 0)),
            pl.BlockSpec((tok, PZ_WIDTH), lambda b, q: (b * nq + q, 0)),
            pl.BlockSpec(memory_space=pltpu.SMEM),
            cache_spec, cache_spec, cache_spec, cache_spec,
            pl.BlockSpec(memory_space=pltpu.SMEM),
            _layer_block((C_GROUPS, CHUNK, CHUNK), layer),
            _layer_block((CHUNK, WIDTH), layer),
        ],
        out_specs=pl.BlockSpec((tok, 3 * WIDTH), lambda b, q: (b * nq + q, 0)),
        out_shape=jax.ShapeDtypeStruct((m, 3 * WIDTH), BF16),
        scratch_shapes=[pltpu.VMEM((past, KV_WIDTH), BF16), pltpu.VMEM((KV_WIDTH, past), BF16),
                        pltpu.VMEM((past, KV_WIDTH), BF16), pltpu.VMEM((KV_WIDTH, past), BF16),
                        pltpu.SMEM((2 * N_KV,), F32)],
        compiler_params=pltpu.CompilerParams(
            dimension_semantics=("arbitrary", "arbitrary"), vmem_limit_bytes=VMEM_LIMIT),
        name="mixer_latent",
    )(qat, qbt, ka, kb, vat, vbt, pz, stats, cak, cav, cbk, cbv, sink, ws, bs)


def _back_kernel(x_ref, z_ref, mod_ref, wg_ref, bg_ref, wpa_ref, wpb_ref, wpc_ref, wo_ref,
                 lg_ref, lb_ref, o_ref, *, n_sub):
    mod = mod_ref[...]
    sub = x_ref.shape[0] // n_sub
    wp_refs = (wpa_ref, wpb_ref, wpc_ref)

    def stages(s):
        rows = slice(s * sub, (s + 1) * sub)
        st = {}

        def norm():
            st["h"] = _modulated_norm(x_ref[rows, :], mod).astype(BF16)

        def branch(br):
            cols = slice(br * D_MODEL, (br + 1) * D_MODEL)
            g = jax.nn.sigmoid(_dot(st["h"], wg_ref[:, cols]) + bg_ref[:, cols])
            y = _dot(z_ref[rows, br * WIDTH:(br + 1) * WIDTH], wp_refs[br][...])
            st["mix"] = g * y if br == 0 else st["mix"] + g * y

        def out_proj():
            st["y"] = _dot(st["mix"].astype(BF16), wo_ref[...])

        def residual_norm():
            r = DEEPNORM_ALPHA * x_ref[rows, :] + mod[:, 2 * D_MODEL:] * st["y"]
            mu = jnp.mean(r, axis=-1, keepdims=True)
            rc = r - mu
            var = jnp.mean(rc * rc, axis=-1, keepdims=True)
            o_ref[rows, :] = rc * lax.rsqrt(var + EPS) * lg_ref[...] + lb_ref[...]

        return [norm] + [functools.partial(branch, br) for br in range(3)] + [out_proj, residual_norm]

    chains = [stages(s) for s in range(n_sub)]
    n_stage = len(chains[0])
    for t in range(n_stage + (n_sub - 1) * BACK_STAGE_LAG):
        for s in range(n_sub):
            k = t - s * BACK_STAGE_LAG
            if 0 <= k < n_stage:
                chains[s][k]()


def _back(x, z, mod, wg, bg, wpa, wpb, wpc, wo, lg, lb, *, layer, tm, mod_row, rows_per_mod, name):
    m = x.shape[0]
    tiles_per_mod = rows_per_mod // tm
    return pl.pallas_call(
        functools.partial(_back_kernel, n_sub=BACK_SUB_TILES),
        grid=(m // tm,),
        in_specs=[
            pl.BlockSpec((tm, D_MODEL), lambda i: (i, 0)),
            pl.BlockSpec((tm, 3 * WIDTH), lambda i: (i, 0)),
            pl.BlockSpec((None, None, 1, 3 * D_MODEL), lambda i: (layer, mod_row + i // tiles_per_mod, 0, 0)),
            _layer_block((D_MODEL, 3 * D_MODEL), layer),
            _layer_block((1, 3 * D_MODEL), layer),
            _layer_block((WIDTH, D_MODEL), layer),
            _layer_block((WIDTH, D_MODEL), layer),
            _layer_block((WIDTH, D_MODEL), layer),
            _layer_block((D_MODEL, D_MODEL), layer),
            _layer_block((1, D_MODEL), layer),
            _layer_block((1, D_MODEL), layer),
        ],
        out_specs=pl.BlockSpec((tm, D_MODEL), lambda i: (i, 0)),
        out_shape=jax.ShapeDtypeStruct((m, D_MODEL), F32),
        compiler_params=pltpu.CompilerParams(
            dimension_semantics=("arbitrary",), vmem_limit_bytes=VMEM_LIMIT),
        name=name,
    )(x, z, mod, wg, bg, wpa, wpb, wpc, wo, lg, lb)


def _rope_tables(n_tokens):
    rows = n_tokens // GRID_W
    row = jnp.repeat(jnp.arange(rows, dtype=F32), GRID_W)
    col = jnp.tile(jnp.arange(GRID_W, dtype=F32), rows)
    inv_freq = jnp.power(ROPE_BASE, -jnp.arange(ROPE_FREQS, dtype=F32) / ROPE_FREQS)
    ang = jnp.concatenate([inv_freq[:, None] * row[None], inv_freq[:, None] * col[None]], axis=0)
    return jnp.cos(ang), jnp.sin(ang)


def _transposed_cache(t):
    b, depth, past = t.shape[:3]
    return jnp.transpose(t, (0, 1, 3, 4, 2)).reshape(b, depth, KV_WIDTH, past)


def kernel(x_prompt, x_sample, cache_a_k, cache_a_v, cache_b_k, cache_b_v, c, c_ctx, w_mod, b_mod,
           w_in, q_norm, k_norm, sink, sgu_g, sgu_b, w_spatial, b_spatial, w_proj_a, w_proj_b,
           w_proj_c, w_gate, b_gate, w_out, ln_g, ln_b):
    batch, seq, _ = x_prompt.shape
    dec_batch, dec_seq, _ = x_sample.shape
    tm = 512

    cvecs = jnp.concatenate(
        [c_ctx[None], c, jnp.zeros((MOD_ROWS - 1 - dec_batch, D_MODEL), F32)], axis=0)
    mod = _modulation(cvecs, w_mod, b_mod)

    rope_tables = _rope_tables(dec_seq)
    caches = [_transposed_cache(t) for t in (cache_a_k, cache_a_v, cache_b_k, cache_b_v)]

    w_in_bf = w_in.astype(BF16)
    wg_bf, wo_bf, ws_bf = w_gate.astype(BF16), w_out.astype(BF16), w_spatial.astype(BF16)
    wpa_bf, wpb_bf, wpc_bf = w_proj_a.astype(BF16), w_proj_b.astype(BF16), w_proj_c.astype(BF16)
    bg3, lg3, lb3 = b_gate[:, None], ln_g[:, None], ln_b[:, None]
    sg3, sb3 = sgu_g[:, None], sgu_b[:, None]
    mod4 = mod[:, :, None]
    qn = jnp.broadcast_to(q_norm[:, :, None], (DEPTH, HEAD_DIM, tm))
    kn = jnp.broadcast_to(k_norm[:, :, None], (DEPTH, HEAD_DIM, tm))
    bs = jnp.repeat(jnp.swapaxes(b_spatial, 1, 2), LANES, axis=2)

    xp = x_prompt.reshape(batch * seq, D_MODEL)
    xs = x_sample.reshape(dec_batch * dec_seq, D_MODEL)
    states = ()
    for l in range(DEPTH):
        *acts, st_ak, st_av, st_bk, st_bv = _front(
            xp, mod4, w_in_bf, qn, kn, sg3, sb3, None, states, layer=l, tm=tm, seq=seq)
        states = [st_ak, st_av, st_bk, st_bv]
        z = _mixer_context(*acts, sink, ws_bf, bs,
                           layer=l, seq=seq, tokens_per_stat=tm)
        xp = _back(xp, z, mod4, wg_bf, bg3, wpa_bf, wpb_bf, wpc_bf, wo_bf, lg3, lb3,
                   layer=l, tm=2 * tm, mod_row=0, rows_per_mod=batch * seq, name="back_context")

        acts = _front(xs, mod4, w_in_bf, qn, kn, sg3, sb3, rope_tables, None,
                      layer=l, tm=tm, seq=dec_seq)
        z = _mixer_latent(*acts, *caches, sink, ws_bf, bs,
                          layer=l, seq=dec_seq, tq=256, tokens_per_stat=tm)
        xs = _back(xs, z, mod4, wg_bf, bg3, wpa_bf, wpb_bf, wpc_bf, wo_bf, lg3, lb3,
                   layer=l, tm=2 * tm, mod_row=1, rows_per_mod=dec_seq, name="back_latent")

    def state(t):
        return jnp.transpose(t.reshape(batch, DEPTH, N_KV, HEAD_DIM, seq), (0, 1, 4, 2, 3))

    return (xp.reshape(batch, seq, D_MODEL), xs.reshape(dec_batch, dec_seq, D_MODEL),
            *[state(t) for t in states])
```
